```python
import math
import numpy as np
import jax
import jax.numpy as jnp
from jax import lax

D_MODEL = 1024
BATCH = 2
SEQ = 8192
DEPTH = 4
DEC_BATCH = 32
DEC_SEQ = 1
PAST_LEN = 8192
PAGE_SIZE = 128

N_BRANCH = 4
MIX_W = D_MODEL // 2
D_FF = 2816
NORM_EPS = 1e-6
GDN_HEADS = 4
GDN_DK = 128
GDN_DV = MIX_W // GDN_HEADS
GDN_CONV = 4
GDN_CHUNK = 64
DIL_PATTERNS = ((128, 1), (512, 4), (2048, 16))
DIL_GROUPS = len(DIL_PATTERNS)
DIL_GROUP_HEADS = 4
DIL_HEAD_DIM = MIX_W // DIL_GROUP_HEADS
DIL_HEADS = DIL_GROUPS * DIL_GROUP_HEADS
DIL_BLOCK = 128
REL_BUCKETS = 32
REL_MAX_DIST = 2048
SC_CONV = 3
LRU_BLOCKS = 8
LRU_BLOCK_W = MIX_W // LRU_BLOCKS
LRU_CONV = 4
LRU_C = 8.0
A_QKV_W = GDN_HEADS * (2 * GDN_DK + GDN_DV)
A_W = A_QKV_W + GDN_HEADS * GDN_DV + 2 * GDN_HEADS
B_W = 3 * DIL_HEADS * DIL_HEAD_DIM
C_W = 3 * MIX_W
D_W = 2 * MIX_W
N_IN = A_W + B_W + C_W + D_W

LAYER_PARAMS = ('norm_ffn1', 'ffn1_w_gu', 'ffn1_w_down', 'norm_mix', 'w_in', 'gdn_conv_w', 'gdn_a_log',
                'gdn_dt_bias', 'gdn_norm_w', 'sc_conv_w', 'lru_conv_w', 'lru_conv_b', 'lru_wa', 'lru_ba',
                'lru_wx', 'lru_bx', 'lru_lambda', 'w_gate', 'b_gate', 'w_branch', 'w_o', 'norm_ffn2',
                'ffn2_w_gu', 'ffn2_w_down')

kernel_name = 'hybrid_gdn_dilated_conv_lru_macaron_step'


def rmsnorm(x, w):
    xf = x.astype(jnp.float32)
    y = xf * lax.rsqrt(jnp.mean(xf * xf, axis=-1, keepdims=True) + NORM_EPS)
    return (y * w.astype(jnp.float32)).astype(x.dtype)


def l2norm(x):
    return x * lax.rsqrt(jnp.sum(x * x, axis=-1, keepdims=True) + NORM_EPS)


def swiglu(x, w_gu, w_down):
    gate, up = jnp.split(x @ w_gu, 2, axis=-1)
    return (jax.nn.silu(gate) * up) @ w_down


def causal_dwconv(x, w, buf, bias=None):
    t = x.shape[1]
    xx = jnp.concatenate([buf.astype(x.dtype), x], axis=1)
    y = xx[:, 0:t] * w[0]
    for i in range(1, w.shape[0]):
        y = y + xx[:, i:i + t] * w[i]
    if bias is not None:
        y = y + bias
    return y, xx[:, t:]


def t5_bucket(dist):
    exact = REL_BUCKETS // 2
    d = np.maximum(dist, 1).astype(np.float32)
    large = exact + (np.log(d / exact) / math.log(REL_MAX_DIST / exact) * (REL_BUCKETS - exact)).astype(np.int32)
    return np.where(dist < exact, dist, np.minimum(large, REL_BUCKETS - 1)).astype(np.int32)


def gated_delta_rule(q, k, v, g, beta, s0):
    b, t, h, dk = q.shape
    dv = v.shape[-1]
    c = GDN_CHUNK
    tp = -(-t // c) * c
    def prep(a):
        a = jnp.pad(a, [(0, 0), (0, tp - t)] + [(0, 0)] * (a.ndim - 2))
        a = jnp.moveaxis(a, 2, 1)
        return a.reshape(a.shape[:2] + (tp // c, c) + a.shape[3:])
    q, k, v, g, beta = (prep(a) for a in (q, k, v, g, beta))
    q = q * dk ** -0.5
    kb = k * beta[..., None]
    vb = v * beta[..., None]
    gc = jnp.cumsum(g, axis=-1)
    causal = jnp.tril(jnp.ones((c, c), bool))
    strict = jnp.tril(jnp.ones((c, c), bool), -1)
    decay = jnp.exp(jnp.where(causal, gc[..., :, None] - gc[..., None, :], -jnp.inf))
    lmat = jnp.where(strict, jnp.einsum('bhnid,bhnjd->bhnij', kb, k) * decay, 0.0)
    rhs = jnp.concatenate([vb, kb * jnp.exp(gc)[..., None]], axis=-1)
    sol = lax.linalg.triangular_solve(jnp.eye(c, dtype=jnp.float32) + lmat, rhs, left_side=True, lower=True)
    u, w = sol[..., :dv], sol[..., dv:]
    attn = jnp.einsum('bhnid,bhnjd->bhnij', q, k) * decay
    xs = tuple(jnp.moveaxis(a, 2, 0) for a in (q, k, u, w, gc, attn))

    def step(s, inp):
        qi, ki, ui, wi, gi, ai = inp
        v_new = ui - jnp.einsum('bhck,bhkv->bhcv', wi, s)
        o = jnp.einsum('bhck,bhkv->bhcv', qi * jnp.exp(gi)[..., None], s) + jnp.einsum('bhij,bhjv->bhiv', ai, v_new)
        g_last = gi[..., -1]
        s = s * jnp.exp(g_last)[..., None, None] + jnp.einsum(
            'bhck,bhcv->bhkv', ki * jnp.exp(g_last[..., None] - gi)[..., None], v_new)
        return s, o

    s, o = lax.scan(step, s0, xs)
    o = jnp.moveaxis(o, 0, 2).reshape(b, h, tp, dv)[:, :, :t]
    return jnp.moveaxis(o, 1, 2), s


def dilated_group_prompt(q, k, v, window, dilation, table_g):
    b, t, h, e = q.shape
    blk = DIL_BLOCK
    n_off = window // dilation
    ln = t // dilation
    lp = -(-ln // blk) * blk
    nb = lp // blk

    def strided(a, front):
        a = jnp.moveaxis(a.reshape(b, ln, dilation, h, e), 2, 1)
        return jnp.pad(a, ((0, 0), (0, 0), (front, lp - ln), (0, 0), (0, 0)))

    qs = strided(q, 0).reshape(b, dilation, nb, blk, h, e)

    def key_blocks(a):
        a = strided(a, blk)
        prev = a[:, :, :lp].reshape(b, dilation, nb, blk, h, e)
        cur = a[:, :, blk:].reshape(b, dilation, nb, blk, h, e)
        return jnp.concatenate([prev, cur], axis=3)

    kb, vb = key_blocks(k), key_blocks(v)
    off = np.arange(blk)[:, None] + blk - np.arange(2 * blk)[None, :]
    key_pos = np.arange(nb)[:, None] * blk - blk + np.arange(2 * blk)[None, :]
    valid = ((off >= 0) & (off <= n_off))[None] & (key_pos >= 0)[:, None, :]
    bias = jnp.moveaxis(table_g[t5_bucket(np.clip(off, 0, n_off) * dilation)], -1, 0).astype(jnp.float32)
    s = jnp.einsum('bdnqhe,bdnkhe->bdnhqk', qs, kb, preferred_element_type=jnp.float32) * e ** -0.5 + bias
    s = jnp.where(valid[:, None], s, -jnp.inf)
    m = jnp.max(s, axis=-1, keepdims=True)
    pr = jnp.exp(s - m)
    l = jnp.sum(pr, axis=-1)
    o = jnp.einsum('bdnhqk,bdnkhe->bdnqhe', pr, vb.astype(jnp.float32)) / jnp.swapaxes(l, -1, -2)[..., None]
    lse = jnp.swapaxes(m[..., 0] + jnp.log(l), -1, -2)

    def unstride(a):
        a = a.reshape((b, dilation, lp) + a.shape[4:])[:, :, :ln]
        return jnp.moveaxis(a, 1, 2).reshape((b, t) + a.shape[3:])

    return unstride(o), unstride(lse)


def dilated_group_step(q, k, v, buf, window, dilation, table_g):
    s_len = q.shape[1]
    rows = buf.shape[2]
    n_off = window // dilation
    keys = jnp.concatenate([buf[:, 0].astype(k.dtype), k], axis=1)
    vals = jnp.concatenate([buf[:, 1].astype(v.dtype), v], axis=1)
    offs = np.arange(n_off + 1)
    idx = rows + np.arange(s_len)[:, None] - offs[None, :] * dilation
    valid = idx >= 0
    idx = np.maximum(idx, 0)
    kg, vg = keys[:, idx], vals[:, idx]
    bias = table_g[t5_bucket(offs * dilation)].T.astype(jnp.float32)
    s = jnp.einsum('bshe,bsjhe->bhsj', q, kg, preferred_element_type=jnp.float32) * q.shape[-1] ** -0.5 + bias[:, None, :]
    s = jnp.where(valid, s, -jnp.inf)
    m = jnp.max(s, axis=-1, keepdims=True)
    pr = jnp.exp(s - m)
    l = jnp.sum(pr, axis=-1)
    o = jnp.einsum('bhsj,bsjhe->bshe', pr, vg.astype(jnp.float32)) / jnp.swapaxes(l, 1, 2)[..., None]
    lse = jnp.swapaxes(m[..., 0] + jnp.log(l), 1, 2)
    new_buf = jnp.stack([keys[:, keys.shape[1] - rows:], vals[:, vals.shape[1] - rows:]], axis=1)
    return o, lse, new_buf


def rg_lru(x, h0, wa, ba, wx, bx, lam):
    b, t, w = x.shape
    xf = x.astype(jnp.float32)
    xb = xf.reshape(b, t, LRU_BLOCKS, LRU_BLOCK_W)
    r = jax.nn.sigmoid(jnp.einsum('btnc,ncd->btnd', xb, wa.astype(jnp.float32)).reshape(b, t, w) + ba)
    i = jax.nn.sigmoid(jnp.einsum('btnc,ncd->btnd', xb, wx.astype(jnp.float32)).reshape(b, t, w) + bx)
    log_a = -LRU_C * r * jax.nn.softplus(-lam.astype(jnp.float32))
    a = jnp.exp(log_a)
    bt = jnp.sqrt(-jnp.expm1(2.0 * log_a)) * i * xf

    def comb(c1, c2):
        return c1[0] * c2[0], c2[0] * c1[1] + c2[1]

    a_cum, b_cum = lax.associative_scan(comb, (a, bt), axis=1)
    hs = a_cum * h0.astype(jnp.float32)[:, None] + b_cum
    return hs.astype(x.dtype), hs[:, -1]


def token_mixing(h, p, rel_bias, st, prompt):
    gdn_s, gdn_conv, dil_bufs, sc_conv, lru_h, lru_conv = st
    b, t, _ = h.shape
    dt = h.dtype
    za, zb, zc, zd = jnp.split(h @ p['w_in'], [A_W, A_W + B_W, A_W + B_W + C_W], axis=-1)

    qkv, zg, beta_raw, a_raw = jnp.split(
        za, [A_QKV_W, A_QKV_W + GDN_HEADS * GDN_DV, A_QKV_W + GDN_HEADS * GDN_DV + GDN_HEADS], axis=-1)
    qkv, gdn_conv_new = causal_dwconv(qkv, p['gdn_conv_w'], gdn_conv)
    qkv = jax.nn.silu(qkv.astype(jnp.float32))
    q, k, v = jnp.split(qkv, [GDN_HEADS * GDN_DK, 2 * GDN_HEADS * GDN_DK], axis=-1)
    q = l2norm(q.reshape(b, t, GDN_HEADS, GDN_DK))
    k = l2norm(k.reshape(b, t, GDN_HEADS, GDN_DK))
    v = v.reshape(b, t, GDN_HEADS, GDN_DV)
    beta = jax.nn.sigmoid(beta_raw.astype(jnp.float32))
    g = -jnp.exp(p['gdn_a_log'].astype(jnp.float32)) * jax.nn.softplus(
        a_raw.astype(jnp.float32) + p['gdn_dt_bias'].astype(jnp.float32))
    o_a, gdn_s_new = gated_delta_rule(q, k, v, g, beta, gdn_s.astype(jnp.float32))
    o_a = rmsnorm(o_a, p['gdn_norm_w']) * jax.nn.silu(zg.astype(jnp.float32)).reshape(b, t, GDN_HEADS, GDN_DV)
    o_a = o_a.reshape(b, t, MIX_W).astype(dt)

    qd, kd, vd = (a.reshape(b, t, DIL_GROUPS, DIL_GROUP_HEADS, DIL_HEAD_DIM) for a in jnp.split(zb, 3, axis=-1))
    outs, lses, dil_new = [], [], []
    for gi, (win, dil) in enumerate(DIL_PATTERNS):
        tab = rel_bias[:, gi * DIL_GROUP_HEADS:(gi + 1) * DIL_GROUP_HEADS]
        qg, kg, vg = qd[:, :, gi], kd[:, :, gi], vd[:, :, gi]
        if prompt:
            o, lse = dilated_group_prompt(qg, kg, vg, win, dil, tab)
            rows = min(win, t)
            nbuf = jnp.stack([kg[:, t - rows:], vg[:, t - rows:]], axis=1)
        else:
            o, lse, nbuf = dilated_group_step(qg, kg, vg, dil_bufs[gi], win, dil, tab)
        outs.append(o)
        lses.append(lse)
        dil_new.append(nbuf)
    wts = jax.nn.softmax(jnp.stack(lses, axis=0), axis=0)
    o_b = jnp.sum(wts[..., None] * jnp.stack(outs, axis=0), axis=0).reshape(b, t, MIX_W).astype(dt)

    gate_b, gate_c, xin = jnp.split(zc, 3, axis=-1)
    u, sc_conv_new = causal_dwconv(gate_c * xin, p['sc_conv_w'], sc_conv)
    o_c = gate_b * u

    xd, gate_d = jnp.split(zd, 2, axis=-1)
    xd, lru_conv_new = causal_dwconv(xd, p['lru_conv_w'], lru_conv, p['lru_conv_b'])
    y_d, lru_h_new = rg_lru(xd, lru_h, p['lru_wa'], p['lru_ba'], p['lru_wx'], p['lru_bx'], p['lru_lambda'])
    o_d = y_d * jax.nn.gelu(gate_d)

    o_all = jnp.stack([o_a, o_b, o_c, o_d], axis=2)
    y_br = jnp.einsum('btnc,ncd->btnd', o_all, p['w_branch'])
    gates = jax.nn.sigmoid(h @ p['w_gate'] + p['b_gate']).reshape(b, t, N_BRANCH, D_MODEL)
    out = jnp.sum(gates * y_br, axis=2) @ p['w_o']
    new = (gdn_s_new.astype(dt), gdn_conv_new, dil_new[0], dil_new[1], dil_new[2], sc_conv_new,
           lru_h_new.astype(dt), lru_conv_new)
    return out, new


def trunk(x, states, params, rel_bias, final_norm, prompt):
    b = x.shape[0]
    collected = []
    for l in range(DEPTH):
        p = {name: params[name][l] for name in LAYER_PARAMS}
        if prompt:
            st = (jnp.zeros((b, GDN_HEADS, GDN_DK, GDN_DV), jnp.float32),
                  jnp.zeros((b, GDN_CONV - 1, A_QKV_W), x.dtype), None,
                  jnp.zeros((b, SC_CONV - 1, MIX_W), x.dtype), jnp.zeros((b, MIX_W), x.dtype),
                  jnp.zeros((b, LRU_CONV - 1, MIX_W), x.dtype))
        else:
            g_s, g_c, d0, d1, d2, sc, lh, lc = (s[l] for s in states)
            st = (g_s, g_c, (d0, d1, d2), sc, lh, lc)
        x = x + 0.5 * swiglu(rmsnorm(x, p['norm_ffn1']), p['ffn1_w_gu'], p['ffn1_w_down'])
        mix, new = token_mixing(rmsnorm(x, p['norm_mix']), p, rel_bias, st, prompt)
        x = x + mix
        x = x + 0.5 * swiglu(rmsnorm(x, p['norm_ffn2']), p['ffn2_w_gu'], p['ffn2_w_down'])
        collected.append(new)
    new_states = tuple(jnp.stack(z, axis=0) for z in zip(*collected))
    return rmsnorm(x, final_norm), new_states


def setup_inputs(seed: int = 0) -> dict:
    key = jax.random.key(seed)
    keys = iter(jax.random.split(key, 48))
    f32 = jnp.float32

    def nrm(shape, scale):
        return jax.random.normal(next(keys), shape, f32) * scale

    def unif(shape, lo, hi):
        return jax.random.uniform(next(keys), shape, f32, lo, hi)

    def gain(shape):
        return 1.0 + nrm(shape, 0.02)

    rows = [min(w, PAST_LEN) for w, _ in DIL_PATTERNS]
    lru_a = unif((DEPTH, MIX_W), 0.9, 0.999) ** (1.0 / LRU_C)
    gdn_dt = jnp.exp(unif((DEPTH, GDN_HEADS), math.log(1e-3), math.log(1e-1)))
    return {
        'x_prompt': nrm((BATCH, SEQ, D_MODEL), 1.0),
        'x_sample': nrm((DEC_BATCH, DEC_SEQ, D_MODEL), 1.0),
        'state_gdn': nrm((DEPTH, DEC_BATCH, GDN_HEADS, GDN_DK, GDN_DV), 0.1),
        'state_gdn_conv': nrm((DEPTH, DEC_BATCH, GDN_CONV - 1, A_QKV_W), 1.0),
        'cache_dil_w128': nrm((DEPTH, DEC_BATCH, 2, rows[0], DIL_GROUP_HEADS, DIL_HEAD_DIM), 1.0),
        'cache_dil_w512': nrm((DEPTH, DEC_BATCH, 2, rows[1], DIL_GROUP_HEADS, DIL_HEAD_DIM), 1.0),
        'cache_dil_w2048': nrm((DEPTH, DEC_BATCH, 2, rows[2], DIL_GROUP_HEADS, DIL_HEAD_DIM), 1.0),
        'state_sc_conv': nrm((DEPTH, DEC_BATCH, SC_CONV - 1, MIX_W), 1.0),
        'state_lru': nrm((DEPTH, DEC_BATCH, MIX_W), 0.5),
        'state_lru_conv': nrm((DEPTH, DEC_BATCH, LRU_CONV - 1, MIX_W), 1.0),
        'norm_ffn1': gain((DEPTH, D_MODEL)),
        'ffn1_w_gu': nrm((DEPTH, D_MODEL, 2 * D_FF), D_MODEL ** -0.5),
        'ffn1_w_down': nrm((DEPTH, D_FF, D_MODEL), D_FF ** -0.5),
        'norm_mix': gain((DEPTH, D_MODEL)),
        'w_in': nrm((DEPTH, D_MODEL, N_IN), D_MODEL ** -0.5),
        'gdn_conv_w': nrm((DEPTH, GDN_CONV, A_QKV_W), GDN_CONV ** -0.5),
        'gdn_a_log': jnp.log(unif((DEPTH, GDN_HEADS), 1.0, 16.0)),
        'gdn_dt_bias': gdn_dt + jnp.log(-jnp.expm1(-gdn_dt)),
        'gdn_norm_w': gain((DEPTH, GDN_DV)),
        'rel_bias': nrm((REL_BUCKETS, DIL_HEADS), 0.5),
        'sc_conv_w': nrm((DEPTH, SC_CONV, MIX_W), SC_CONV ** -0.5),
        'lru_conv_w': nrm((DEPTH, LRU_CONV, MIX_W), LRU_CONV ** -0.5),
        'lru_conv_b': nrm((DEPTH, MIX_W), 0.02),
        'lru_wa': nrm((DEPTH, LRU_BLOCKS, LRU_BLOCK_W, LRU_BLOCK_W), LRU_BLOCK_W ** -0.5),
        'lru_ba': nrm((DEPTH, MIX_W), 0.02),
        'lru_wx': nrm((DEPTH, LRU_BLOCKS, LRU_BLOCK_W, LRU_BLOCK_W), LRU_BLOCK_W ** -0.5),
        'lru_bx': nrm((DEPTH, MIX_W), 0.02),
        'lru_lambda': jnp.log(lru_a) - jnp.log1p(-lru_a),
        'w_gate': nrm((DEPTH, D_MODEL, N_BRANCH * D_MODEL), D_MODEL ** -0.5),
        'b_gate': nrm((DEPTH, N_BRANCH * D_MODEL), 0.02),
        'w_branch': nrm((DEPTH, N_BRANCH, MIX_W, D_MODEL), MIX_W ** -0.5),
        'w_o': nrm((DEPTH, D_MODEL, D_MODEL), D_MODEL ** -0.5),
        'norm_ffn2': gain((DEPTH, D_MODEL)),
        'ffn2_w_gu': nrm((DEPTH, D_MODEL, 2 * D_FF), D_MODEL ** -0.5),
        'ffn2_w_down': nrm((DEPTH, D_FF, D_MODEL), D_FF ** -0.5),
        'final_norm': gain((D_MODEL,)),
    }


def reference(x_prompt, x_sample, state_gdn, state_gdn_conv, cache_dil_w128, cache_dil_w512, cache_dil_w2048,
              state_sc_conv, state_lru, state_lru_conv, norm_ffn1, ffn1_w_gu, ffn1_w_down, norm_mix, w_in,
              gdn_conv_w, gdn_a_log, gdn_dt_bias, gdn_norm_w, rel_bias, sc_conv_w, lru_conv_w, lru_conv_b,
              lru_wa, lru_ba, lru_wx, lru_bx, lru_lambda, w_gate, b_gate, w_branch, w_o, norm_ffn2, ffn2_w_gu,
              ffn2_w_down, final_norm):
    params = dict(norm_ffn1=norm_ffn1, ffn1_w_gu=ffn1_w_gu, ffn1_w_down=ffn1_w_down, norm_mix=norm_mix,
                  w_in=w_in, gdn_conv_w=gdn_conv_w, gdn_a_log=gdn_a_log, gdn_dt_bias=gdn_dt_bias,
                  gdn_norm_w=gdn_norm_w, sc_conv_w=sc_conv_w, lru_conv_w=lru_conv_w, lru_conv_b=lru_conv_b,
                  lru_wa=lru_wa, lru_ba=lru_ba, lru_wx=lru_wx, lru_bx=lru_bx, lru_lambda=lru_lambda,
                  w_gate=w_gate, b_gate=b_gate, w_branch=w_branch, w_o=w_o, norm_ffn2=norm_ffn2,
                  ffn2_w_gu=ffn2_w_gu, ffn2_w_down=ffn2_w_down)
    y_prompt, (p_gdn, p_gdn_conv, p_w128, p_w512, p_w2048, p_sc_conv, p_lru, p_lru_conv) = trunk(
        x_prompt, None, params, rel_bias, final_norm, True)
    sample_states = (state_gdn, state_gdn_conv, cache_dil_w128, cache_dil_w512, cache_dil_w2048,
                     state_sc_conv, state_lru, state_lru_conv)
    y_sample, (s_gdn, s_gdn_conv, s_w128, s_w512, s_w2048, s_sc_conv, s_lru, s_lru_conv) = trunk(
        x_sample, sample_states, params, rel_bias, final_norm, False)
    return (y_prompt, y_sample, p_gdn, s_gdn, p_gdn_conv, s_gdn_conv, p_w128, s_w128, p_w512, s_w512,
            p_w2048, s_w2048, p_sc_conv, s_sc_conv, p_lru, s_lru, p_lru_conv, s_lru_conv)
```

```python
import functools
import math

import numpy as np
import jax
import jax.numpy as jnp
from jax import lax
from jax.experimental import pallas as pl
from jax.experimental.pallas import tpu as pltpu

D_MODEL = 1024
DEPTH = 4
MIX_W = D_MODEL // 2
D_FF = 2816
NORM_EPS = 1e-6
N_BRANCH = 4
GDN_HEADS = 4
GDN_DK = 128
GDN_DV = MIX_W // GDN_HEADS
GDN_CONV = 4
GDN_CHUNK = 64
DIL_PATTERNS = ((128, 1), (512, 4), (2048, 16))
DIL_GROUPS = len(DIL_PATTERNS)
DIL_GROUP_HEADS = 4
DIL_HEAD_DIM = MIX_W // DIL_GROUP_HEADS
DIL_HEADS = DIL_GROUPS * DIL_GROUP_HEADS
DIL_BLOCK = 128
REL_BUCKETS = 32
REL_MAX_DIST = 2048
SC_CONV = 3
LRU_BLOCKS = 8
LRU_BLOCK_W = MIX_W // LRU_BLOCKS
LRU_CONV = 4
LRU_C = 8.0
A_QKV_W = GDN_HEADS * (2 * GDN_DK + GDN_DV)
A_W = A_QKV_W + GDN_HEADS * GDN_DV + 2 * GDN_HEADS
B_W = 3 * DIL_HEADS * DIL_HEAD_DIM
C_W = 3 * MIX_W
D_W = 2 * MIX_W

LANES = 128
SUBLANES = 8
VMEM_LIMIT = 56 * 1024 * 1024

COL_BLK = MIX_W
Z_QKV = 0
Z_ZG = A_QKV_W
Z_B = Z_ZG + MIX_W
Z_C = Z_B + B_W
Z_D = Z_C + C_W
Z_GATE = Z_D + D_W
Z_W = Z_GATE + N_BRANCH * D_MODEL
BA_W = LANES

F32 = jnp.float32
BF16 = jnp.bfloat16
HI = lax.Precision.HIGHEST
NT_DIMS = (((1,), (1,)), ((), ()))
TN_DIMS = (((0,), (0,)), ((), ()))


def _cparams(sem):
    return pltpu.CompilerParams(dimension_semantics=sem, vmem_limit_bytes=VMEM_LIMIT)


def _rms(x, w):
    return x * lax.rsqrt(jnp.mean(x * x, axis=-1, keepdims=True) + NORM_EPS) * w


def _softplus(x):
    return jnp.maximum(x, 0.0) + jnp.log1p(jnp.exp(-jnp.abs(x)))


def _dot(a, b, precision=None):
    return jnp.dot(a, b, preferred_element_type=F32, precision=precision)


def _dot_nt(a, b, precision=None):
    return lax.dot_general(a, b, NT_DIMS, preferred_element_type=F32, precision=precision)


def _dot_tn(a, b, precision=None):
    return lax.dot_general(a, b, TN_DIMS, preferred_element_type=F32, precision=precision)


def _ffn_body(x_ref, nw_ref, wg_ref, wu_ref, wd_ref, o_ref, h_ref, acc_ref):
    j = pl.program_id(1)

    @pl.when(j == 0)
    def _():
        h_ref[...] = _rms(x_ref[...], nw_ref[...]).astype(BF16)
        acc_ref[...] = jnp.zeros_like(acc_ref)

    h = h_ref[...]
    g = _dot(h, wg_ref[...])
    u = _dot(h, wu_ref[...])
    a = (jax.nn.silu(g) * u).astype(BF16)
    acc_ref[...] += _dot(a, wd_ref[...])

    @pl.when(j == pl.num_programs(1) - 1)
    def _():
        o_ref[...] = x_ref[...] + 0.5 * acc_ref[...]


def _ffn(x, nw, w_gu, w_down):
    m = x.shape[0]
    tm = min(m, 512)
    nf = 2
    tf = D_FF // nf
    return pl.pallas_call(
        _ffn_body,
        grid=(m // tm, nf),
        in_specs=[
            pl.BlockSpec((tm, D_MODEL), lambda i, j: (i, 0)),
            pl.BlockSpec((1, D_MODEL), lambda i, j: (0, 0)),
            pl.BlockSpec((D_MODEL, tf), lambda i, j: (0, j)),
            pl.BlockSpec((D_MODEL, tf), lambda i, j: (0, nf + j)),
            pl.BlockSpec((tf, D_MODEL), lambda i, j: (j, 0)),
        ],
        out_specs=pl.BlockSpec((tm, D_MODEL), lambda i, j: (i, 0)),
        out_shape=jax.ShapeDtypeStruct((m, D_MODEL), F32),
        scratch_shapes=[pltpu.VMEM((tm, D_MODEL), BF16), pltpu.VMEM((tm, D_MODEL), F32)],
        compiler_params=_cparams(("parallel", "arbitrary")),
        name="ffn",
    )(x, nw, w_gu, w_gu, w_down)


def _proj_body(x_ref, nw_ref, w_ref, b_ref, o_ref, h_ref, *, sig_from):
    j = pl.program_id(1)

    @pl.when(j == 0)
    def _():
        h_ref[...] = _rms(x_ref[...], nw_ref[...]).astype(BF16)

    z = _dot(h_ref[...], w_ref[...]) + b_ref[...]

    @pl.when(j < sig_from)
    def _():
        o_ref[...] = z

    @pl.when(j >= sig_from)
    def _():
        o_ref[...] = jax.nn.sigmoid(z)


def _proj(x, nw, w, b, tn, sig_cols):
    m = x.shape[0]
    n = w.shape[1]
    tm = min(m, 1024)
    return pl.pallas_call(
        functools.partial(_proj_body, sig_from=sig_cols // tn),
        grid=(m // tm, n // tn),
        in_specs=[
            pl.BlockSpec((tm, D_MODEL), lambda i, j: (i, 0)),
            pl.BlockSpec((1, D_MODEL), lambda i, j: (0, 0)),
            pl.BlockSpec((D_MODEL, tn), lambda i, j: (0, j)),
            pl.BlockSpec((1, tn), lambda i, j: (0, j)),
        ],
        out_specs=pl.BlockSpec((tm, tn), lambda i, j: (i, j)),
        out_shape=jax.ShapeDtypeStruct((m, n), F32),
        scratch_shapes=[pltpu.VMEM((tm, D_MODEL), BF16)],
        compiler_params=_cparams(("parallel", "arbitrary")),
        name="proj",
    )(x, nw, w, b)


def _inv_unit_lower(l_mat, eye):
    x = eye - l_mat
    p = l_mat
    for _ in range(int(math.log2(GDN_CHUNK)) - 1):
        p = _dot(p, p, HI)
        x = x + _dot(x, p, HI)
    return x


def _gdn_body(qkv_ref, zg_ref, ba_ref, cw_ref, alog_ref, dtb_ref, nw_ref, o_ref, s_out_ref,
              xbuf, act, beta_s, gc_s, u_s, w_s, at_s, qg_s, kd_s, s_s, *, tt):
    t = pl.program_id(1)
    c = GDN_CHUNK
    nchunk = tt // c

    @pl.when(t == 0)
    def _():
        xbuf[pl.ds(0, SUBLANES), :] = jnp.zeros((SUBLANES, A_QKV_W), F32)
        s_s[...] = jnp.zeros_like(s_s)

    xbuf[pl.ds(SUBLANES, tt), :] = qkv_ref[...]
    y = cw_ref[0:1, :] * xbuf[pl.ds(SUBLANES - 3, tt), :]
    for i in range(1, GDN_CONV):
        y = y + cw_ref[i:i + 1, :] * xbuf[pl.ds(SUBLANES - 3 + i, tt), :]
    act[...] = jax.nn.silu(y)
    xbuf[pl.ds(0, SUBLANES), :] = xbuf[pl.ds(tt, SUBLANES), :]

    ba = ba_ref[...]
    beta_s[...] = jax.nn.sigmoid(ba)
    gc_s[...] = -jnp.exp(alog_ref[...]) * _softplus(ba + dtb_ref[...])

    ri = lax.broadcasted_iota(jnp.int32, (c, c), 0)
    ci = lax.broadcasted_iota(jnp.int32, (c, c), 1)
    causal = ri >= ci
    strict = ri > ci
    diag = ri == ci
    eye = jnp.where(diag, 1.0, 0.0).astype(F32)
    tril = jnp.where(causal, 1.0, 0.0).astype(F32)

    def prep(ch, carry):
        rows = pl.ds(pl.multiple_of(ch * c, c), c)
        gcum = _dot(tril, gc_s[rows, :], HI)
        gc_s[rows, :] = gcum
        beta_all = beta_s[rows, :]
        for h in range(GDN_HEADS):
            hs = slice(h * GDN_DK, (h + 1) * GDN_DK)
            gcol = gcum[:, GDN_HEADS + h:GDN_HEADS + h + 1]
            grow = jnp.sum(jnp.where(diag, gcol, 0.0), axis=0, keepdims=True)
            decay = jnp.exp(jnp.where(causal, gcol - grow, -jnp.inf))
            q = act[rows, hs]
            k = act[rows, GDN_HEADS * GDN_DK + h * GDN_DK:GDN_HEADS * GDN_DK + (h + 1) * GDN_DK]
            v = act[rows, 2 * GDN_HEADS * GDN_DK + h * GDN_DV:2 * GDN_HEADS * GDN_DK + (h + 1) * GDN_DV]
            q = q * lax.rsqrt(jnp.sum(q * q, axis=-1, keepdims=True) + NORM_EPS) * (GDN_DK ** -0.5)
            k = k * lax.rsqrt(jnp.sum(k * k, axis=-1, keepdims=True) + NORM_EPS)
            beta = beta_all[:, h:h + 1]
            kb = k * beta
            vb = v * beta
            lmat = jnp.where(strict, _dot_nt(kb, k, HI) * decay, 0.0)
            attn = _dot_nt(q, k, HI) * decay
            tinv = _inv_unit_lower(lmat, eye)
            egc = jnp.exp(gcol)
            glast = gcol[c - 1:c, :]
            hv = slice(h * GDN_DV, (h + 1) * GDN_DV)
            u_s[rows, hv] = _dot(tinv, vb, HI)
            w_s[rows, hs] = _dot(tinv, kb * egc, HI)
            at_s[rows, h * LANES:h * LANES + c] = attn
            qg_s[rows, hs] = q * egc
            kd_s[rows, hs] = k * jnp.exp(glast - gcol)
        return carry

    lax.fori_loop(0, nchunk, prep, 0)

    def step(ch, carry):
        rows = pl.ds(pl.multiple_of(ch * c, c), c)
        glast_all = jnp.exp(gc_s[pl.ds(ch * c + c - 1, 1), :])
        for h in range(GDN_HEADS):
            hs = slice(h * GDN_DK, (h + 1) * GDN_DK)
            hv = slice(h * GDN_DV, (h + 1) * GDN_DV)
            s = s_s[h]
            v_new = u_s[rows, hv] - _dot(w_s[rows, hs], s, HI)
            o = _dot(qg_s[rows, hs], s, HI) + _dot(at_s[rows, h * LANES:h * LANES + c], v_new, HI)
            s_s[h] = s * glast_all[:, GDN_HEADS + h:GDN_HEADS + h + 1] + _dot_tn(kd_s[rows, hs], v_new, HI)
            o = _rms(o, nw_ref[...]) * jax.nn.silu(zg_ref[rows, hv])
            o_ref[rows, hv] = o
        return carry

    lax.fori_loop(0, nchunk, step, 0)

    @pl.when(t == pl.num_programs(1) - 1)
    def _():
        s_out_ref[...] = s_s[...]


def _gdn_prompt(z3, ba3, cw, alog_row, dtb_row, nw):
    b, t, _ = z3.shape
    tt = min(t, 512)
    return pl.pallas_call(
        functools.partial(_gdn_body, tt=tt),
        grid=(b, t // tt),
        in_specs=[
            pl.BlockSpec((None, tt, A_QKV_W), lambda i, j: (i, j, Z_QKV // A_QKV_W)),
            pl.BlockSpec((None, tt, MIX_W), lambda i, j: (i, j, Z_ZG // MIX_W)),
            pl.BlockSpec((None, tt, BA_W), lambda i, j: (i, j, 0)),
            pl.BlockSpec((GDN_CONV, A_QKV_W), lambda i, j: (0, 0)),
            pl.BlockSpec((1, BA_W), lambda i, j: (0, 0)),
            pl.BlockSpec((1, BA_W), lambda i, j: (0, 0)),
            pl.BlockSpec((1, GDN_DV), lambda i, j: (0, 0)),
        ],
        out_specs=[
            pl.BlockSpec((None, tt, MIX_W), lambda i, j: (i, j, 0)),
            pl.BlockSpec((None, GDN_HEADS, GDN_DK, GDN_DV), lambda i, j: (i, 0, 0, 0)),
        ],
        out_shape=[
            jax.ShapeDtypeStruct((b, t, MIX_W), F32),
            jax.ShapeDtypeStruct((b, GDN_HEADS, GDN_DK, GDN_DV), F32),
        ],
        scratch_shapes=[
            pltpu.VMEM((tt + SUBLANES, A_QKV_W), F32),
            pltpu.VMEM((tt, A_QKV_W), F32),
            pltpu.VMEM((tt, BA_W), F32),
            pltpu.VMEM((tt, BA_W), F32),
            pltpu.VMEM((tt, MIX_W), F32),
            pltpu.VMEM((tt, MIX_W), F32),
            pltpu.VMEM((tt, GDN_HEADS * LANES), F32),
            pltpu.VMEM((tt, MIX_W), F32),
            pltpu.VMEM((tt, MIX_W), F32),
            pltpu.VMEM((GDN_HEADS, GDN_DK, GDN_DV), F32),
        ],
        compiler_params=_cparams(("parallel", "arbitrary")),
        name="gdn_prompt",
    )(z3, z3, ba3, cw, alog_row, dtb_row, nw)


def _dil_body(q_ref, kp_ref, kc_ref, vp_ref, vc_ref, bp_ref, bc_ref, o_ref, lse_ref):
    n = pl.program_id(2)
    e = DIL_HEAD_DIM
    for h in range(DIL_GROUP_HEADS):
        hs = slice(h * e, (h + 1) * e)
        q = q_ref[:, hs].astype(BF16)
        sp = _dot_nt(q, kp_ref[:, hs].astype(BF16)) * (e ** -0.5) + bp_ref[h]
        sc = _dot_nt(q, kc_ref[:, hs].astype(BF16)) * (e ** -0.5) + bc_ref[h]
        sp = jnp.where(n > 0, sp, -jnp.inf)
        m = jnp.maximum(jnp.max(sp, axis=-1, keepdims=True), jnp.max(sc, axis=-1, keepdims=True))
        pp = jnp.exp(sp - m)
        pc = jnp.exp(sc - m)
        l = jnp.sum(pp, axis=-1, keepdims=True) + jnp.sum(pc, axis=-1, keepdims=True)
        o = _dot(pp.astype(BF16), vp_ref[:, hs].astype(BF16)) + _dot(pc.astype(BF16), vc_ref[:, hs].astype(BF16))
        o_ref[:, hs] = o / l
        lse_ref[:, hs] = jnp.broadcast_to(m + jnp.log(l), (DIL_BLOCK, e))


def _dil_prompt(z3, gi, dilation, bias_prev, bias_cur):
    b, t, _ = z3.shape
    ln = t // dilation
    nb = ln // DIL_BLOCK
    zs = z3.reshape(b, ln, dilation * Z_W)
    per_r = Z_W // COL_BLK
    cq = Z_B // COL_BLK + gi
    ck = cq + DIL_GROUPS
    cv = ck + DIL_GROUPS
    blk = (None, DIL_BLOCK, COL_BLK)
    bias_spec = pl.BlockSpec((DIL_GROUP_HEADS, DIL_BLOCK, DIL_BLOCK), lambda i, r, n: (0, 0, 0))
    o, lse = pl.pallas_call(
        _dil_body,
        grid=(b, dilation, nb),
        in_specs=[
            pl.BlockSpec(blk, lambda i, r, n: (i, n, r * per_r + cq)),
            pl.BlockSpec(blk, lambda i, r, n: (i, jnp.maximum(n - 1, 0), r * per_r + ck)),
            pl.BlockSpec(blk, lambda i, r, n: (i, n, r * per_r + ck)),
            pl.BlockSpec(blk, lambda i, r, n: (i, jnp.maximum(n - 1, 0), r * per_r + cv)),
            pl.BlockSpec(blk, lambda i, r, n: (i, n, r * per_r + cv)),
            bias_spec,
            bias_spec,
        ],
        out_specs=[
            pl.BlockSpec(blk, lambda i, r, n: (i, n, r)),
            pl.BlockSpec(blk, lambda i, r, n: (i, n, r)),
        ],
        out_shape=[
            jax.ShapeDtypeStruct((b, ln, dilation * MIX_W), F32),
            jax.ShapeDtypeStruct((b, ln, dilation * MIX_W), F32),
        ],
        compiler_params=_cparams(("parallel", "parallel", "arbitrary")),
        name=f"dil_prompt_d{dilation}",
    )(zs, zs, zs, zs, zs, bias_prev, bias_cur)
    return o.reshape(b, t, MIX_W), lse.reshape(b, t, MIX_W)


def _lru_gates(xc, wa_ref, ba_ref, wx_ref, bx_ref, lam_ref):
    xb = xc.astype(BF16)
    r = jax.nn.sigmoid(_dot(xb, wa_ref[...]) + ba_ref[...])
    i = jax.nn.sigmoid(_dot(xb, wx_ref[...]) + bx_ref[...])
    log_a = -LRU_C * r * _softplus(-lam_ref[...])
    a = jnp.exp(log_a)
    bt = jnp.sqrt(-jnp.tanh(log_a) * (jnp.exp(2.0 * log_a) + 1.0)) * i * xc
    return a, bt


def _cd_body(gb_ref, gc_ref, xi_ref, xd_ref, gd_ref, scw_ref, lcw_ref, lcb_ref, wa_ref, ba_ref, wx_ref,
             bx_ref, lam_ref, oc_ref, od_ref, scst_ref, lruh_ref, cbuf, dbuf, a_s, b_s, h_s, *, tt):
    t = pl.program_id(1)

    @pl.when(t == 0)
    def _():
        cbuf[pl.ds(0, SUBLANES), :] = jnp.zeros((SUBLANES, MIX_W), F32)
        dbuf[pl.ds(0, SUBLANES), :] = jnp.zeros((SUBLANES, MIX_W), F32)
        h_s[...] = jnp.zeros_like(h_s)

    cbuf[pl.ds(SUBLANES, tt), :] = gc_ref[...] * xi_ref[...]
    u = scw_ref[0:1, :] * cbuf[pl.ds(SUBLANES - 2, tt), :]
    for i in range(1, SC_CONV):
        u = u + scw_ref[i:i + 1, :] * cbuf[pl.ds(SUBLANES - 2 + i, tt), :]
    oc_ref[...] = gb_ref[...] * u
    tail = cbuf[pl.ds(tt, SUBLANES), :]
    scst_ref[...] = tail
    cbuf[pl.ds(0, SUBLANES), :] = tail

    dbuf[pl.ds(SUBLANES, tt), :] = xd_ref[...]
    xc = lcw_ref[0:1, :] * dbuf[pl.ds(SUBLANES - 3, tt), :]
    for i in range(1, LRU_CONV):
        xc = xc + lcw_ref[i:i + 1, :] * dbuf[pl.ds(SUBLANES - 3 + i, tt), :]
    xc = xc + lcb_ref[...]
    dbuf[pl.ds(0, SUBLANES), :] = dbuf[pl.ds(tt, SUBLANES), :]
    a, bt = _lru_gates(xc, wa_ref, ba_ref, wx_ref, bx_ref, lam_ref)
    a_s[...] = a
    b_s[...] = bt

    def scan(g, h):
        rows = pl.ds(pl.multiple_of(g * SUBLANES, SUBLANES), SUBLANES)
        a8 = a_s[rows, :]
        b8 = b_s[rows, :]
        out = []
        for r in range(SUBLANES):
            h = a8[r:r + 1, :] * h + b8[r:r + 1, :]
            out.append(h)
        a_s[rows, :] = jnp.concatenate(out, axis=0)
        return h

    h_last = lax.fori_loop(0, tt // SUBLANES, scan, h_s[0:1, :])
    h_s[...] = jnp.broadcast_to(h_last, h_s.shape)
    lruh_ref[...] = jnp.broadcast_to(h_last, h_s.shape)
    od_ref[...] = a_s[...] * jax.nn.gelu(gd_ref[...])


def _cd_prompt(z3, scw, lcw, lcb, wa, ba, wx, bx, lam):
    b, t, _ = z3.shape
    tt = min(t, 512)
    blk = (None, tt, COL_BLK)
    c0 = Z_C // COL_BLK

    def zspec(c):
        return pl.BlockSpec(blk, lambda i, j: (i, j, c))

    def full(shape):
        return pl.BlockSpec(shape, lambda i, j: (0,) * len(shape))

    st = pl.BlockSpec((None, SUBLANES, MIX_W), lambda i, j: (i, 0, 0))
    return pl.pallas_call(
        functools.partial(_cd_body, tt=tt),
        grid=(b, t // tt),
        in_specs=[zspec(c0), zspec(c0 + 1), zspec(c0 + 2), zspec(c0 + 3), zspec(c0 + 4),
                  full((SC_CONV, MIX_W)), full((LRU_CONV, MIX_W)), full((1, MIX_W)),
                  full((MIX_W, MIX_W)), full((1, MIX_W)), full((MIX_W, MIX_W)), full((1, MIX_W)),
                  full((1, MIX_W))],
        out_specs=[pl.BlockSpec(blk, lambda i, j: (i, j, 0)), pl.BlockSpec(blk, lambda i, j: (i, j, 0)), st, st],
        out_shape=[
            jax.ShapeDtypeStruct((b, t, MIX_W), F32),
            jax.ShapeDtypeStruct((b, t, MIX_W), F32),
            jax.ShapeDtypeStruct((b, SUBLANES, MIX_W), F32),
            jax.ShapeDtypeStruct((b, SUBLANES, MIX_W), F32),
        ],
        scratch_shapes=[
            pltpu.VMEM((tt + SUBLANES, MIX_W), F32),
            pltpu.VMEM((tt + SUBLANES, MIX_W), F32),
            pltpu.VMEM((tt, MIX_W), F32),
            pltpu.VMEM((tt, MIX_W), F32),
            pltpu.VMEM((SUBLANES, MIX_W), F32),
        ],
        compiler_params=_cparams(("parallel", "arbitrary")),
        name="cd_prompt",
    )(z3, z3, z3, z3, z3, scw, lcw, lcb, wa, ba, wx, bx, lam)


def _merge_body(*refs, n_att):
    x_ref, oa_ref = refs[0], refs[1]
    att = refs[2:2 + n_att]
    rest = refs[2 + n_att:]
    oc_ref, od_ref = rest[0], rest[1]
    g_refs = rest[2:2 + N_BRANCH]
    wbr_ref, wo_ref, o_ref = rest[2 + N_BRANCH:]
    if n_att == 1:
        ob = att[0][...]
    else:
        outs = [att[i][...] for i in range(0, n_att, 2)]
        lses = [att[i][...] for i in range(1, n_att, 2)]
        m = functools.reduce(jnp.maximum, lses)
        es = [jnp.exp(l - m) for l in lses]
        den = functools.reduce(lambda p, q: p + q, es)
        ob = functools.reduce(lambda p, q: p + q, [(ei / den) * oi for ei, oi in zip(es, outs)])
    branches = (oa_ref[...], ob, oc_ref[...], od_ref[...])
    y = None
    for nbr, br in enumerate(branches):
        yb = _dot(br.astype(BF16), wbr_ref[nbr]) * g_refs[nbr][...]
        y = yb if y is None else y + yb
    o_ref[...] = x_ref[...] + _dot(y.astype(BF16), wo_ref[...])


def _merge(x, oa, att, oc, od, z, wbr, wo):
    m = x.shape[0]
    tm = min(m, 256)
    row = lambda w: pl.BlockSpec((tm, w), lambda i: (i, 0))
    gate_specs = [pl.BlockSpec((tm, D_MODEL), lambda i, n=n: (i, Z_GATE // D_MODEL + n)) for n in range(N_BRANCH)]
    return pl.pallas_call(
        functools.partial(_merge_body, n_att=len(att)),
        grid=(m // tm,),
        in_specs=[row(D_MODEL), row(MIX_W)] + [row(MIX_W)] * len(att) + [row(MIX_W), row(MIX_W)] + gate_specs +
                 [pl.BlockSpec((N_BRANCH, MIX_W, D_MODEL), lambda i: (0, 0, 0)),
                  pl.BlockSpec((D_MODEL, D_MODEL), lambda i: (0, 0))],
        out_specs=row(D_MODEL),
        out_shape=jax.ShapeDtypeStruct((m, D_MODEL), F32),
        compiler_params=_cparams(("parallel",)),
        name="merge",
    )(x, oa, *att, oc, od, *([z] * N_BRANCH), wbr, wo)


def _norm_body(x_ref, w_ref, o_ref):
    o_ref[...] = _rms(x_ref[...], w_ref[...])


def _final_norm(x, w):
    m = x.shape[0]
    tm = min(m, 1024)
    return pl.pallas_call(
        _norm_body,
        grid=(m // tm,),
        in_specs=[pl.BlockSpec((tm, D_MODEL), lambda i: (i, 0)), pl.BlockSpec((1, D_MODEL), lambda i: (0, 0))],
        out_specs=pl.BlockSpec((tm, D_MODEL), lambda i: (i, 0)),
        out_shape=jax.ShapeDtypeStruct((m, D_MODEL), F32),
        compiler_params=_cparams(("parallel",)),
        name="final_norm",
    )(x, w)


def _spw_body(z_ref, gcs_ref, scs_ref, lcs_ref, lh_ref, gcw_ref, scw_ref, lcw_ref, lcb_ref, wa_ref, ba_ref,
              wx_ref, bx_ref, lam_ref, act_ref, gcs_o, oc_ref, scs_o, od_ref, lh_o, lcs_o):
    w = A_QKV_W
    x = z_ref[:, Z_QKV:Z_QKV + w]
    y = gcw_ref[GDN_CONV - 1:GDN_CONV, :] * x
    for i in range(GDN_CONV - 1):
        y = y + gcw_ref[i:i + 1, :] * gcs_ref[:, i * w:(i + 1) * w]
    act_ref[...] = jax.nn.silu(y)
    gcs_o[:, 0:(GDN_CONV - 2) * w] = gcs_ref[:, w:(GDN_CONV - 1) * w]
    gcs_o[:, (GDN_CONV - 2) * w:(GDN_CONV - 1) * w] = x

    w = MIX_W
    gate_b = z_ref[:, Z_C:Z_C + w]
    ci = z_ref[:, Z_C + w:Z_C + 2 * w] * z_ref[:, Z_C + 2 * w:Z_C + 3 * w]
    u = scw_ref[SC_CONV - 1:SC_CONV, :] * ci
    for i in range(SC_CONV - 1):
        u = u + scw_ref[i:i + 1, :] * scs_ref[:, i * w:(i + 1) * w]
    oc_ref[...] = gate_b * u
    scs_o[:, 0:(SC_CONV - 2) * w] = scs_ref[:, w:(SC_CONV - 1) * w]
    scs_o[:, (SC_CONV - 2) * w:(SC_CONV - 1) * w] = ci

    xd = z_ref[:, Z_D:Z_D + w]
    gate_d = z_ref[:, Z_D + w:Z_D + 2 * w]
    xc = lcw_ref[LRU_CONV - 1:LRU_CONV, :] * xd
    for i in range(LRU_CONV - 1):
        xc = xc + lcw_ref[i:i + 1, :] * lcs_ref[:, i * w:(i + 1) * w]
    xc = xc + lcb_ref[...]
    a, bt = _lru_gates(xc, wa_ref, ba_ref, wx_ref, bx_ref, lam_ref)
    hnew = a * lh_ref[...] + bt
    lh_o[...] = hnew
    od_ref[...] = hnew * jax.nn.gelu(gate_d)
    lcs_o[:, 0:(LRU_CONV - 2) * w] = lcs_ref[:, w:(LRU_CONV - 1) * w]
    lcs_o[:, (LRU_CONV - 2) * w:(LRU_CONV - 1) * w] = xd


def _sample_pointwise(z, gcs, scs, lcs, lh, gcw, scw, lcw, lcb, wa, ba, wx, bx, lam):
    bd = z.shape[0]
    shapes = [(bd, A_QKV_W), gcs.shape, (bd, MIX_W), scs.shape, (bd, MIX_W), lh.shape, lcs.shape]
    return pl.pallas_call(
        _spw_body,
        out_shape=[jax.ShapeDtypeStruct(s, F32) for s in shapes],
        compiler_params=pltpu.CompilerParams(vmem_limit_bytes=VMEM_LIMIT),
        name="sample_pointwise",
    )(z, gcs, scs, lcs, lh, gcw, scw, lcw, lcb, wa, ba, wx, bx, lam)


def _col(row, diag):
    return jnp.sum(jnp.where(diag, row, 0.0), axis=1, keepdims=True)


def _sstep_body(act_ref, z_ref, ba_ref, s_ref, alog_ref, dtb_ref, nw_ref,
                k0_ref, v0_ref, k1_ref, v1_ref, k2_ref, v2_ref, bb_ref, bn_ref,
                oa_ref, ob_ref, s_out_ref):
    e = GDN_DK
    ri = lax.broadcasted_iota(jnp.int32, (e, e), 0)
    ci = lax.broadcasted_iota(jnp.int32, (e, e), 1)
    diag = ri == ci

    ba = ba_ref[...]
    beta_all = jax.nn.sigmoid(ba)
    g_all = -jnp.exp(alog_ref[...]) * _softplus(ba + dtb_ref[...])
    for h in range(GDN_HEADS):
        q = act_ref[:, h * e:(h + 1) * e]
        k = act_ref[:, GDN_HEADS * e + h * e:GDN_HEADS * e + (h + 1) * e]
        v = act_ref[:, 2 * GDN_HEADS * e + h * GDN_DV:2 * GDN_HEADS * e + (h + 1) * GDN_DV]
        q = q * lax.rsqrt(jnp.sum(q * q, axis=-1, keepdims=True) + NORM_EPS) * (e ** -0.5)
        k = k * lax.rsqrt(jnp.sum(k * k, axis=-1, keepdims=True) + NORM_EPS)
        beta = beta_all[:, h:h + 1]
        eg = jnp.exp(g_all[:, GDN_HEADS + h:GDN_HEADS + h + 1])
        s = s_ref[h]
        kcol = _col(k, diag)
        qcol = _col(q, diag)
        ks = jnp.sum(kcol * s, axis=0, keepdims=True)
        qs = jnp.sum(qcol * s, axis=0, keepdims=True)
        v_new = beta * v - (beta * eg) * ks
        qk = jnp.sum(q * k, axis=-1, keepdims=True)
        o = eg * qs + qk * v_new
        s_out_ref[h] = s * eg + kcol * v_new
        hv = slice(h * GDN_DV, (h + 1) * GDN_DV)
        oa_ref[:, hv] = _rms(o, nw_ref[...]) * jax.nn.silu(z_ref[:, Z_ZG + h * GDN_DV:Z_ZG + (h + 1) * GDN_DV])

    kv = ((k0_ref, v0_ref), (k1_ref, v1_ref), (k2_ref, v2_ref))
    for h in range(DIL_GROUP_HEADS):
        hs = slice(h * e, (h + 1) * e)
        outs, lses = [], []
        for g in range(DIL_GROUPS):
            c0 = Z_B + g * MIX_W + h * e
            q = z_ref[:, c0:c0 + e]
            kn = z_ref[:, c0 + B_W // 3:c0 + B_W // 3 + e]
            vn = z_ref[:, c0 + 2 * B_W // 3:c0 + 2 * B_W // 3 + e]
            kb = kv[g][0][:, hs]
            vb = kv[g][1][:, hs]
            col = g * DIL_GROUP_HEADS + h
            sb = jnp.sum(kb * q, axis=-1, keepdims=True) * (e ** -0.5) + bb_ref[:, col:col + 1]
            sn = jnp.sum(kn * q, axis=-1, keepdims=True) * (e ** -0.5) + bn_ref[:, col:col + 1]
            m = jnp.maximum(jnp.max(sb, axis=0, keepdims=True), sn)
            pb = jnp.exp(sb - m)
            pn = jnp.exp(sn - m)
            l = jnp.sum(pb, axis=0, keepdims=True) + pn
            outs.append((jnp.sum(pb * vb, axis=0, keepdims=True) + pn * vn) / l)
            lses.append(m + jnp.log(l))
        m = functools.reduce(jnp.maximum, lses)
        es = [jnp.exp(l - m) for l in lses]
        den = functools.reduce(lambda p, q_: p + q_, es)
        ob_ref[:, hs] = functools.reduce(lambda p, q_: p + q_, [(ei / den) * oi for ei, oi in zip(es, outs)])


def _sample_step(act3, z3, ba3, s_in, alog_row, dtb_row, nw, caches, layer, bias_buf, bias_new):
    bd = act3.shape[0]

    def vec(w):
        return pl.BlockSpec((None, 1, w), lambda i: (i, 0, 0))

    def full(shape):
        return pl.BlockSpec(shape, lambda i: (0,) * len(shape))

    cache_specs, cache_args = [], []
    for cch in caches:
        for kvi in range(2):
            cache_specs.append(pl.BlockSpec((None, None, None, DIL_BLOCK, MIX_W),
                                            lambda i, kvi=kvi: (layer, i, kvi, 0, 0)))
            cache_args.append(cch)
    st = pl.BlockSpec((None, GDN_HEADS, GDN_DK, GDN_DV), lambda i: (i, 0, 0, 0))
    return pl.pallas_call(
        _sstep_body,
        grid=(bd,),
        in_specs=[vec(A_QKV_W), vec(Z_W), vec(BA_W), st,
                  full((1, BA_W)), full((1, BA_W)), full((1, GDN_DV))] + cache_specs +
                 [full((DIL_BLOCK, DIL_HEADS)), full((1, DIL_HEADS))],
        out_specs=[vec(MIX_W), vec(MIX_W), st],
        out_shape=[jax.ShapeDtypeStruct((bd, 1, MIX_W), F32), jax.ShapeDtypeStruct((bd, 1, MIX_W), F32),
                   jax.ShapeDtypeStruct(s_in.shape, F32)],
        compiler_params=_cparams(("parallel",)),
        name="sample_step",
    )(act3, z3, ba3, s_in, alog_row, dtb_row, nw, *cache_args, bias_buf, bias_new)


def _shift_body(a_ref, b_ref, o_ref, *, rb):
    o_ref[...] = pltpu.roll(a_ref[...], rb - 1, 0)
    o_ref[pl.ds(rb - 1, 1), :] = b_ref[pl.ds(0, 1), :]


def _shift_rows(cache):
    s, rows, w = cache.shape
    rb = min(rows, 512)
    nblk8 = rows // SUBLANES
    return pl.pallas_call(
        functools.partial(_shift_body, rb=rb),
        grid=(s, rows // rb),
        in_specs=[
            pl.BlockSpec((None, rb, w), lambda i, j: (i, j, 0)),
            pl.BlockSpec((None, SUBLANES, w), lambda i, j: (i, jnp.minimum((j + 1) * (rb // SUBLANES), nblk8 - 1), 0)),
        ],
        out_specs=pl.BlockSpec((None, rb, w), lambda i, j: (i, j, 0)),
        out_shape=jax.ShapeDtypeStruct(cache.shape, F32),
        compiler_params=_cparams(("parallel", "parallel")),
        name="shift_rows",
    )(cache, cache)


def _setrow_body(old_ref, k_ref, v_ref, o_ref):
    o_ref[...] = old_ref[...]
    o_ref[0, SUBLANES - 1:SUBLANES, :] = k_ref[...]
    o_ref[1, SUBLANES - 1:SUBLANES, :] = v_ref[...]


def _set_last_row(buf, layer, z3, gi):
    _, bd, _, rows, w = buf.shape
    last = rows // SUBLANES - 1
    ck = Z_B // COL_BLK + DIL_GROUPS + gi
    cv = ck + DIL_GROUPS
    blk = pl.BlockSpec((None, None, 2, SUBLANES, w), lambda i: (layer, i, 0, last, 0))
    return pl.pallas_call(
        _setrow_body,
        grid=(bd,),
        in_specs=[blk, pl.BlockSpec((None, 1, w), lambda i: (i, 0, ck)), pl.BlockSpec((None, 1, w), lambda i: (i, 0, cv))],
        out_specs=blk,
        out_shape=jax.ShapeDtypeStruct(buf.shape, F32),
        input_output_aliases={0: 0},
        compiler_params=_cparams(("parallel",)),
        name="set_last_row",
    )(buf, z3, z3)


def _t5_bucket(dist):
    exact = REL_BUCKETS // 2
    d = np.maximum(dist, 1).astype(np.float32)
    large = exact + (np.log(d / exact) / math.log(REL_MAX_DIST / exact) * (REL_BUCKETS - exact)).astype(np.int32)
    return np.where(dist < exact, dist, np.minimum(large, REL_BUCKETS - 1)).astype(np.int32)


def _prompt_bias(rel_bias, gi, window, dilation):
    blk = DIL_BLOCK
    n_off = window // dilation
    off = np.arange(blk)[:, None] + blk - np.arange(2 * blk)[None, :]
    valid = (off >= 0) & (off <= n_off)
    tab = rel_bias[:, gi * DIL_GROUP_HEADS:(gi + 1) * DIL_GROUP_HEADS]
    bias = jnp.moveaxis(tab[_t5_bucket(np.clip(off, 0, n_off) * dilation)], -1, 0).astype(F32)
    bias = jnp.where(valid[None], bias, -jnp.inf)
    return bias[:, :, :blk], bias[:, :, blk:]


def _step_bias(rel_bias):
    cols_buf, cols_new = [], []
    for gi, (window, dilation) in enumerate(DIL_PATTERNS):
        n_off = window // dilation
        tab = rel_bias[:, gi * DIL_GROUP_HEADS:(gi + 1) * DIL_GROUP_HEADS]
        j = n_off - np.arange(n_off)
        cols_buf.append(tab[_t5_bucket(j * dilation)])
        cols_new.append(tab[_t5_bucket(np.zeros((1,), np.int64))])
    return jnp.concatenate(cols_buf, axis=1).astype(F32), jnp.concatenate(cols_new, axis=1).astype(F32)


def _block_diag(w):
    n, c, _ = w.shape
    eye = jnp.eye(n, dtype=w.dtype)
    return (eye[:, None, :, None] * w[:, :, None, :]).reshape(n * c, n * c)


def _layer_weights(p, l):
    w_in = p['w_in'][l]
    a0, b0, c0, d0 = 0, A_W, A_W + B_W, A_W + B_W + C_W
    zg0 = A_QKV_W
    ba0 = A_QKV_W + GDN_HEADS * GDN_DV
    w_all = jnp.concatenate([w_in[:, a0:zg0], w_in[:, zg0:ba0], w_in[:, b0:c0], w_in[:, c0:d0], w_in[:, d0:],
                             p['w_gate'][l]], axis=1).astype(BF16)
    b_all = jnp.concatenate([jnp.zeros((Z_GATE,), F32), p['b_gate'][l]])[None]
    w_ba = jnp.pad(w_in[:, ba0:b0], ((0, 0), (0, BA_W - 2 * GDN_HEADS))).astype(BF16)
    pad_row = lambda v: jnp.pad(v, (GDN_HEADS, BA_W - 2 * GDN_HEADS))[None]
    row = lambda v: v[None].astype(F32)
    return dict(
        n1=row(p['norm_ffn1'][l]), gu1=p['ffn1_w_gu'][l].astype(BF16), dn1=p['ffn1_w_down'][l].astype(BF16),
        nm=row(p['norm_mix'][l]), w_all=w_all, b_all=b_all, w_ba=w_ba, b_ba=jnp.zeros((1, BA_W), F32),
        gcw=p['gdn_conv_w'][l], alog=pad_row(p['gdn_a_log'][l]), dtb=pad_row(p['gdn_dt_bias'][l]),
        gnw=row(p['gdn_norm_w'][l]), scw=p['sc_conv_w'][l], lcw=p['lru_conv_w'][l], lcb=row(p['lru_conv_b'][l]),
        wa=_block_diag(p['lru_wa'][l]).astype(BF16), ba=row(p['lru_ba'][l]),
        wx=_block_diag(p['lru_wx'][l]).astype(BF16), bx=row(p['lru_bx'][l]), lam=row(p['lru_lambda'][l]),
        wbr=p['w_branch'][l].astype(BF16), wo=p['w_o'][l].astype(BF16),
        n2=row(p['norm_ffn2'][l]), gu2=p['ffn2_w_gu'][l].astype(BF16), dn2=p['ffn2_w_down'][l].astype(BF16),
    )


def _prompt_layer(x, w, biases, b, t):
    m = b * t
    x = _ffn(x, w['n1'], w['gu1'], w['dn1'])
    z = _proj(x, w['nm'], w['w_all'], w['b_all'], 1024, Z_GATE)
    ba = _proj(x, w['nm'], w['w_ba'], w['b_ba'], BA_W, BA_W)
    z3 = z.reshape(b, t, Z_W)
    o_a, s_gdn = _gdn_prompt(z3, ba.reshape(b, t, BA_W), w['gcw'], w['alog'], w['dtb'], w['gnw'])
    att = []
    for gi, (window, dilation) in enumerate(DIL_PATTERNS):
        o, lse = _dil_prompt(z3, gi, dilation, *biases[gi])
        att += [o.reshape(m, MIX_W), lse.reshape(m, MIX_W)]
    o_c, o_d, sc_st, lru_h = _cd_prompt(z3, w['scw'], w['lcw'], w['lcb'], w['wa'], w['ba'], w['wx'], w['bx'], w['lam'])
    x = _merge(x, o_a.reshape(m, MIX_W), att, o_c.reshape(m, MIX_W), o_d.reshape(m, MIX_W), z, w['wbr'], w['wo'])
    x = _ffn(x, w['n2'], w['gu2'], w['dn2'])
    gdn_conv = z3[:, t - (GDN_CONV - 1):, Z_QKV:Z_QKV + A_QKV_W]
    bufs = []
    for gi, (window, _) in enumerate(DIL_PATTERNS):
        rows = min(window, t)
        k = z3[:, t - rows:, Z_B + B_W // 3 + gi * MIX_W:Z_B + B_W // 3 + (gi + 1) * MIX_W]
        v = z3[:, t - rows:, Z_B + 2 * B_W // 3 + gi * MIX_W:Z_B + 2 * B_W // 3 + (gi + 1) * MIX_W]
        bufs.append(jnp.stack([k, v], axis=1).reshape(b, 2, rows, DIL_GROUP_HEADS, DIL_HEAD_DIM))
    sc_conv = sc_st[:, SUBLANES - (SC_CONV - 1):]
    lru_conv = z3[:, t - (LRU_CONV - 1):, Z_D:Z_D + MIX_W]
    return x, (s_gdn, gdn_conv, bufs[0], bufs[1], bufs[2], sc_conv, lru_h[:, 0], lru_conv)


def _sample_layer(x, w, l, st, shifted, step_bias):
    bd = x.shape[0]
    s_gdn, gdn_conv, caches, sc_conv, lru_h, lru_conv = st
    x = _ffn(x, w['n1'], w['gu1'], w['dn1'])
    z = _proj(x, w['nm'], w['w_all'], w['b_all'], 1024, Z_GATE)
    ba = _proj(x, w['nm'], w['w_ba'], w['b_ba'], BA_W, BA_W)
    act, gdn_conv_new, o_c, sc_new, o_d, lru_h_new, lru_conv_new = _sample_pointwise(
        z, gdn_conv.reshape(bd, -1), sc_conv.reshape(bd, -1), lru_conv.reshape(bd, -1), lru_h,
        w['gcw'], w['scw'], w['lcw'], w['lcb'], w['wa'], w['ba'], w['wx'], w['bx'], w['lam'])
    z3 = z.reshape(bd, 1, Z_W)
    o_a, o_b, s_new = _sample_step(act.reshape(bd, 1, A_QKV_W), z3, ba.reshape(bd, 1, BA_W), s_gdn,
                                   w['alog'], w['dtb'], w['gnw'], caches, l, *step_bias)
    shifted = [_set_last_row(buf, l, z3, gi) for gi, buf in enumerate(shifted)]
    x = _merge(x, o_a.reshape(bd, MIX_W), [o_b.reshape(bd, MIX_W)], o_c, o_d, z, w['wbr'], w['wo'])
    x = _ffn(x, w['n2'], w['gu2'], w['dn2'])
    new = (s_new, gdn_conv_new.reshape(gdn_conv.shape), sc_new.reshape(sc_conv.shape), lru_h_new,
           lru_conv_new.reshape(lru_conv.shape))
    return x, new, shifted


def kernel(x_prompt, x_sample, state_gdn, state_gdn_conv, cache_dil_w128, cache_dil_w512, cache_dil_w2048,
           state_sc_conv, state_lru, state_lru_conv, norm_ffn1, ffn1_w_gu, ffn1_w_down, norm_mix, w_in,
           gdn_conv_w, gdn_a_log, gdn_dt_bias, gdn_norm_w, rel_bias, sc_conv_w, lru_conv_w, lru_conv_b,
           lru_wa, lru_ba, lru_wx, lru_bx, lru_lambda, w_gate, b_gate, w_branch, w_o, norm_ffn2, ffn2_w_gu,
           ffn2_w_down, final_norm):
    p = dict(norm_ffn1=norm_ffn1, ffn1_w_gu=ffn1_w_gu, ffn1_w_down=ffn1_w_down, norm_mix=norm_mix, w_in=w_in,
             gdn_conv_w=gdn_conv_w, gdn_a_log=gdn_a_log, gdn_dt_bias=gdn_dt_bias, gdn_norm_w=gdn_norm_w,
             sc_conv_w=sc_conv_w, lru_conv_w=lru_conv_w, lru_conv_b=lru_conv_b, lru_wa=lru_wa, lru_ba=lru_ba,
             lru_wx=lru_wx, lru_bx=lru_bx, lru_lambda=lru_lambda, w_gate=w_gate, b_gate=b_gate,
             w_branch=w_branch, w_o=w_o, norm_ffn2=norm_ffn2, ffn2_w_gu=ffn2_w_gu, ffn2_w_down=ffn2_w_down)
    weights = [_layer_weights(p, l) for l in range(DEPTH)]
    fnw = final_norm[None]
    b, t, _ = x_prompt.shape
    bd = x_sample.shape[0]
    biases = [_prompt_bias(rel_bias, gi, window, dilation) for gi, (window, dilation) in enumerate(DIL_PATTERNS)]
    step_bias = _step_bias(rel_bias)

    x = x_prompt.reshape(b * t, D_MODEL)
    p_states = []
    for l in range(DEPTH):
        x, new = _prompt_layer(x, weights[l], biases, b, t)
        p_states.append(new)
    y_prompt = _final_norm(x, fnw).reshape(b, t, D_MODEL)
    p_out = tuple(jnp.stack(zs, axis=0) for zs in zip(*p_states))

    caches = (cache_dil_w128, cache_dil_w512, cache_dil_w2048)
    shifted = []
    for cch in caches:
        rows = cch.shape[3]
        shifted.append(_shift_rows(cch.reshape(DEPTH * bd * 2, rows, MIX_W)).reshape(DEPTH, bd, 2, rows, MIX_W))
    x = x_sample.reshape(bd, D_MODEL)
    strided = tuple(c.reshape(DEPTH, bd, 2, c.shape[3] // d, d * MIX_W) for c, (_, d) in zip(caches, DIL_PATTERNS))
    s_states = []
    for l in range(DEPTH):
        st = (state_gdn[l], state_gdn_conv[l], strided, state_sc_conv[l], state_lru[l], state_lru_conv[l])
        x, new, shifted = _sample_layer(x, weights[l], l, st, shifted, step_bias)
        s_states.append(new)
    y_sample = _final_norm(x, fnw).reshape(bd, 1, D_MODEL)
    s_gdn, s_gdn_conv, s_sc_conv, s_lru, s_lru_conv = (jnp.stack(zs, axis=0) for zs in zip(*s_states))
    s_w = [buf.reshape(c.shape) for buf, c in zip(shifted, caches)]

    p_gdn, p_gdn_conv, p_w128, p_w512, p_w2048, p_sc_conv, p_lru, p_lru_conv = p_out
    return (y_prompt, y_sample, p_gdn, s_gdn, p_gdn_conv, s_gdn_conv, p_w128, s_w[0], p_w512, s_w[1],
            p_w2048, s_w[2], p_sc_conv, s_sc_conv, p_lru, s_lru, p_lru_conv, s_lru_conv)
```

```python
import functools
import math

import numpy as np
import jax
import jax.numpy as jnp
from jax import lax
from jax.experimental import pallas as pl
from jax.experimental.pallas import tpu as pltpu

D_MODEL = 1024
DEPTH = 4
MIX_W = D_MODEL // 2
D_FF = 2816
NORM_EPS = 1e-6
N_BRANCH = 4
GDN_HEADS = 4
GDN_DK = 128
GDN_DV = MIX_W // GDN_HEADS
GDN_CONV = 4
GDN_CHUNK = 64
DIL_PATTERNS = ((128, 1), (512, 4), (2048, 16))
DIL_GROUPS = len(DIL_PATTERNS)
DIL_GROUP_HEADS = 4
DIL_HEAD_DIM = MIX_W // DIL_GROUP_HEADS
DIL_HEADS = DIL_GROUPS * DIL_GROUP_HEADS
DIL_BLOCK = 128
REL_BUCKETS = 32
REL_MAX_DIST = 2048
SC_CONV = 3
LRU_BLOCKS = 8
LRU_BLOCK_W = MIX_W // LRU_BLOCKS
LRU_CONV = 4
LRU_C = 8.0
A_QKV_W = GDN_HEADS * (2 * GDN_DK + GDN_DV)
A_W = A_QKV_W + GDN_HEADS * GDN_DV + 2 * GDN_HEADS
B_W = 3 * DIL_HEADS * DIL_HEAD_DIM
C_W = 3 * MIX_W
D_W = 2 * MIX_W

LANES = 128
SUBLANES = 8
VMEM_LIMIT = 56 * 1024 * 1024

ZA_W = A_QKV_W + MIX_W
ZB_Q, ZB_K, ZB_V = 0, B_W // 3, 2 * B_W // 3
ZB_C = B_W
ZB_D = B_W + C_W
ZB_W = B_W + C_W + D_W
G_W = N_BRANCH * D_MODEL
PROJ_TN = 1024
BA_W = LANES
DIL_TB = DIL_BLOCK * max(d for _, d in DIL_PATTERNS)
DIL_UNROLL = 4

F32 = jnp.float32
BF16 = jnp.bfloat16
NT_DIMS = (((1,), (1,)), ((), ()))
TN_DIMS = (((0,), (0,)), ((), ()))


def _cparams(sem):
    return pltpu.CompilerParams(dimension_semantics=sem, vmem_limit_bytes=VMEM_LIMIT)


def _rms(x, w):
    return x * lax.rsqrt(jnp.mean(x * x, axis=-1, keepdims=True) + NORM_EPS) * w


def _softplus(x):
    return jnp.maximum(x, 0.0) + jnp.log1p(jnp.exp(-jnp.abs(x)))


def _dot(a, b):
    return jnp.dot(a.astype(BF16), b.astype(BF16), preferred_element_type=F32)


def _dot_nt(a, b):
    return lax.dot_general(a.astype(BF16), b.astype(BF16), NT_DIMS, preferred_element_type=F32)


def _dot_tn(a, b):
    return lax.dot_general(a.astype(BF16), b.astype(BF16), TN_DIMS, preferred_element_type=F32)


def _ffn_body(x_ref, nw_ref, wg_ref, wu_ref, wd_ref, o_ref, h_ref, acc_ref):
    j = pl.program_id(1)

    @pl.when(j == 0)
    def _():
        h_ref[...] = _rms(x_ref[...], nw_ref[...]).astype(BF16)
        acc_ref[...] = jnp.zeros_like(acc_ref)

    h = h_ref[...]
    g = _dot(h, wg_ref[...])
    u = _dot(h, wu_ref[...])
    acc_ref[...] += _dot(jax.nn.silu(g) * u, wd_ref[...])

    @pl.when(j == pl.num_programs(1) - 1)
    def _():
        o_ref[...] = x_ref[...] + 0.5 * acc_ref[...]


def _ffn(x, nw, w_gu, w_down):
    m = x.shape[0]
    tm = min(m, 512)
    nf = 2
    tf = D_FF // nf
    return pl.pallas_call(
        _ffn_body,
        grid=(m // tm, nf),
        in_specs=[
            pl.BlockSpec((tm, D_MODEL), lambda i, j: (i, 0)),
            pl.BlockSpec((1, D_MODEL), lambda i, j: (0, 0)),
            pl.BlockSpec((D_MODEL, tf), lambda i, j: (0, j)),
            pl.BlockSpec((D_MODEL, tf), lambda i, j: (0, nf + j)),
            pl.BlockSpec((tf, D_MODEL), lambda i, j: (j, 0)),
        ],
        out_specs=pl.BlockSpec((tm, D_MODEL), lambda i, j: (i, 0)),
        out_shape=jax.ShapeDtypeStruct((m, D_MODEL), F32),
        scratch_shapes=[pltpu.VMEM((tm, D_MODEL), BF16), pltpu.VMEM((tm, D_MODEL), F32)],
        compiler_params=_cparams(("parallel", "arbitrary")),
        name="ffn",
    )(x, nw, w_gu, w_gu, w_down)


def _proj_body(x_ref, nw_ref, w_ref, b_ref, za_ref, zb_ref, g_ref, h_ref, *, ja, jb):
    j = pl.program_id(1)

    @pl.when(j == 0)
    def _():
        h_ref[...] = _rms(x_ref[...], nw_ref[...]).astype(BF16)

    z = _dot(h_ref[...], w_ref[...]) + b_ref[...]

    @pl.when(j < ja)
    def _():
        za_ref[...] = z

    @pl.when((j >= ja) & (j < jb))
    def _():
        zb_ref[...] = z

    @pl.when(j >= jb)
    def _():
        g_ref[...] = jax.nn.sigmoid(z).astype(BF16)


def _proj(x, nw, w, b):
    m = x.shape[0]
    tm = min(m, 1024)
    tn = PROJ_TN
    ja, jb = ZA_W // tn, (ZA_W + ZB_W) // tn
    nj = (ZA_W + ZB_W + G_W) // tn
    return pl.pallas_call(
        functools.partial(_proj_body, ja=ja, jb=jb),
        grid=(m // tm, nj),
        in_specs=[
            pl.BlockSpec((tm, D_MODEL), lambda i, j: (i, 0)),
            pl.BlockSpec((1, D_MODEL), lambda i, j: (0, 0)),
            pl.BlockSpec((D_MODEL, tn), lambda i, j: (0, j)),
            pl.BlockSpec((1, tn), lambda i, j: (0, j)),
        ],
        out_specs=[
            pl.BlockSpec((tm, tn), lambda i, j: (i, jnp.minimum(j, ja - 1))),
            pl.BlockSpec((tm, tn), lambda i, j: (i, jnp.clip(j - ja, 0, jb - ja - 1))),
            pl.BlockSpec((tm, tn), lambda i, j: (i, jnp.maximum(j - jb, 0))),
        ],
        out_shape=[
            jax.ShapeDtypeStruct((m, ZA_W), F32),
            jax.ShapeDtypeStruct((m, ZB_W), F32),
            jax.ShapeDtypeStruct((m, G_W), BF16),
        ],
        scratch_shapes=[pltpu.VMEM((tm, D_MODEL), BF16)],
        compiler_params=_cparams(("parallel", "arbitrary")),
        name="proj",
    )(x, nw, w, b)


def _proj_ba_body(x_ref, nw_ref, w_ref, o_ref):
    o_ref[...] = _dot(_rms(x_ref[...], nw_ref[...]), w_ref[...])


def _proj_ba(x, nw, w):
    m = x.shape[0]
    tm = min(m, 1024)
    return pl.pallas_call(
        _proj_ba_body,
        grid=(m // tm,),
        in_specs=[
            pl.BlockSpec((tm, D_MODEL), lambda i: (i, 0)),
            pl.BlockSpec((1, D_MODEL), lambda i: (0, 0)),
            pl.BlockSpec((D_MODEL, BA_W), lambda i: (0, 0)),
        ],
        out_specs=pl.BlockSpec((tm, BA_W), lambda i: (i, 0)),
        out_shape=jax.ShapeDtypeStruct((m, BA_W), F32),
        compiler_params=_cparams(("parallel",)),
        name="proj_ba",
    )(x, nw, w)


def _gdn_body(qkv_ref, zg_ref, ba_ref, cw_ref, alog_ref, dtb_ref, nw_ref, o_ref, s_out_ref,
              xbuf, act, beta_s, gc_s, u_s, wq_s, at_s, kd_s, s_s, *, tt):
    t = pl.program_id(1)
    c = GDN_CHUNK
    nchunk = tt // c

    @pl.when(t == 0)
    def _():
        xbuf[pl.ds(0, SUBLANES), :] = jnp.zeros((SUBLANES, A_QKV_W), F32)
        s_s[...] = jnp.zeros_like(s_s)

    xbuf[pl.ds(SUBLANES, tt), :] = qkv_ref[...]
    y = cw_ref[0:1, :] * xbuf[pl.ds(SUBLANES - 3, tt), :]
    for i in range(1, GDN_CONV):
        y = y + cw_ref[i:i + 1, :] * xbuf[pl.ds(SUBLANES - 3 + i, tt), :]
    act[...] = jax.nn.silu(y)
    xbuf[pl.ds(0, SUBLANES), :] = xbuf[pl.ds(tt, SUBLANES), :]

    ba = ba_ref[...]
    beta_s[...] = jax.nn.sigmoid(ba)
    g = -jnp.exp(alog_ref[...]) * _softplus(ba + dtb_ref[...])
    rowmod = lax.broadcasted_iota(jnp.int32, (tt, BA_W), 0) & (c - 1)
    sh = 1
    while sh < c:
        g = g + jnp.where(rowmod >= sh, pltpu.roll(g, sh, 0), 0.0)
        sh *= 2
    gc_s[...] = g

    ri = lax.broadcasted_iota(jnp.int32, (c, c), 0)
    ci = lax.broadcasted_iota(jnp.int32, (c, c), 1)
    causal = ri >= ci
    strict = ri > ci
    diag = ri == ci
    eye = jnp.where(diag, 1.0, 0.0).astype(F32)

    heads = range(GDN_HEADS)

    def prep(i, carry):
        chains = [(2 * i + j, h) for j in range(2) for h in heads]
        n = len(chains)
        rows_l, gcol_l, decay_l, q_l, k_l, kb_l, vb_l = [], [], [], [], [], [], []
        for ch, h in chains:
            rows = pl.ds(pl.multiple_of(ch * c, c), c)
            gcol = gc_s[rows, GDN_HEADS + h:GDN_HEADS + h + 1]
            grow = jnp.sum(jnp.where(diag, gcol, 0.0), axis=0, keepdims=True)
            decay = jnp.exp(jnp.where(causal, gcol - grow, -jnp.inf))
            q = act[rows, h * GDN_DK:(h + 1) * GDN_DK]
            k = act[rows, GDN_HEADS * GDN_DK + h * GDN_DK:GDN_HEADS * GDN_DK + (h + 1) * GDN_DK]
            v = act[rows, 2 * GDN_HEADS * GDN_DK + h * GDN_DV:2 * GDN_HEADS * GDN_DK + (h + 1) * GDN_DV]
            q = q * lax.rsqrt(jnp.sum(q * q, axis=-1, keepdims=True) + NORM_EPS) * (GDN_DK ** -0.5)
            k = k * lax.rsqrt(jnp.sum(k * k, axis=-1, keepdims=True) + NORM_EPS)
            beta = beta_s[rows, h:h + 1]
            rows_l.append(rows)
            gcol_l.append(gcol)
            decay_l.append(decay)
            q_l.append(q)
            k_l.append(k)
            kb_l.append(k * beta)
            vb_l.append(v * beta)
        kq_l = [_dot_nt(jnp.concatenate([kb_l[j], q_l[j]], axis=0), k_l[j]) for j in range(n)]
        for j, (ch, h) in enumerate(chains):
            at_s[rows_l[j], h * LANES:h * LANES + c] = kq_l[j][c:] * decay_l[j]
        qq = [-jnp.where(strict, kq_l[j][:c] * decay_l[j], 0.0) for j in range(n)]
        yy = [eye + qq[j] for j in range(n)]
        qq = [_dot(qq[j], qq[j]) for j in range(n)]
        for _ in range(int(math.log2(c)) - 2):
            yq = [_dot(jnp.concatenate([yy[j], qq[j]], axis=0), qq[j]) for j in range(n)]
            yy = [yy[j] + yq[j][:c] for j in range(n)]
            qq = [yq[j][c:] for j in range(n)]
        yq = [_dot(yy[j], qq[j]) for j in range(n)]
        tinv = [yy[j] + yq[j] for j in range(n)]
        egc = [jnp.exp(gcol_l[j]) for j in range(n)]
        sol = [_dot(tinv[j], jnp.concatenate([vb_l[j], kb_l[j] * egc[j]], axis=1)) for j in range(n)]
        for j, (ch, h) in enumerate(chains):
            hs = slice(h * GDN_DK, (h + 1) * GDN_DK)
            u_s[rows_l[j], h * GDN_DV:(h + 1) * GDN_DV] = sol[j][:, :GDN_DV]
            wq_s[pl.ds(pl.multiple_of(ch * 2 * c, 2 * c), c), hs] = sol[j][:, GDN_DV:]
            wq_s[pl.ds(pl.multiple_of(ch * 2 * c + c, c), c), hs] = q_l[j] * egc[j]
            kd_s[rows_l[j], hs] = k_l[j] * jnp.exp(gcol_l[j][c - 1:c, :] - gcol_l[j])
        return carry

    lax.fori_loop(0, nchunk // 2, prep, 0)

    def step(ch, carry):
        rows = pl.ds(pl.multiple_of(ch * c, c), c)
        glast_all = jnp.exp(gc_s[pl.ds(ch * c + c - 1, 1), :])
        hsl = [slice(h * GDN_DK, (h + 1) * GDN_DK) for h in heads]
        hvl = [slice(h * GDN_DV, (h + 1) * GDN_DV) for h in heads]
        st = [s_s[h] for h in heads]
        wq_rows = pl.ds(pl.multiple_of(ch * 2 * c, 2 * c), 2 * c)
        ws = [_dot(wq_s[wq_rows, hsl[h]], st[h]) for h in heads]
        v_new = [u_s[rows, hvl[h]] - ws[h][:c] for h in heads]
        av = [_dot(at_s[rows, h * LANES:h * LANES + c], v_new[h]) for h in heads]
        kv = [_dot_tn(kd_s[rows, hsl[h]], v_new[h]) for h in heads]
        for h in heads:
            s_s[h] = st[h] * glast_all[:, GDN_HEADS + h:GDN_HEADS + h + 1] + kv[h]
            o = _rms(ws[h][c:] + av[h], nw_ref[...]) * jax.nn.silu(zg_ref[rows, hvl[h]])
            o_ref[rows, hvl[h]] = o
        return carry

    lax.fori_loop(0, nchunk, step, 0)

    @pl.when(t == pl.num_programs(1) - 1)
    def _():
        s_out_ref[...] = s_s[...]


def _gdn_prompt(za3, ba3, cw, alog_row, dtb_row, nw):
    b, t, _ = za3.shape
    tt = min(t, 512)
    return pl.pallas_call(
        functools.partial(_gdn_body, tt=tt),
        grid=(b, t // tt),
        in_specs=[
            pl.BlockSpec((None, tt, A_QKV_W), lambda i, j: (i, j, 0)),
            pl.BlockSpec((None, tt, MIX_W), lambda i, j: (i, j, A_QKV_W // MIX_W)),
            pl.BlockSpec((None, tt, BA_W), lambda i, j: (i, j, 0)),
            pl.BlockSpec((GDN_CONV, A_QKV_W), lambda i, j: (0, 0)),
            pl.BlockSpec((1, BA_W), lambda i, j: (0, 0)),
            pl.BlockSpec((1, BA_W), lambda i, j: (0, 0)),
            pl.BlockSpec((1, GDN_DV), lambda i, j: (0, 0)),
        ],
        out_specs=[
            pl.BlockSpec((None, tt, MIX_W), lambda i, j: (i, j, 0)),
            pl.BlockSpec((None, GDN_HEADS, GDN_DK, GDN_DV), lambda i, j: (i, 0, 0, 0)),
        ],
        out_shape=[
            jax.ShapeDtypeStruct((b, t, MIX_W), F32),
            jax.ShapeDtypeStruct((b, GDN_HEADS, GDN_DK, GDN_DV), F32),
        ],
        scratch_shapes=[
            pltpu.VMEM((tt + SUBLANES, A_QKV_W), F32),
            pltpu.VMEM((tt, A_QKV_W), F32),
            pltpu.VMEM((tt, BA_W), F32),
            pltpu.VMEM((tt, BA_W), F32),
            pltpu.VMEM((tt, MIX_W), F32),
            pltpu.VMEM((2 * tt, MIX_W), F32),
            pltpu.VMEM((tt, GDN_HEADS * LANES), F32),
            pltpu.VMEM((tt, MIX_W), F32),
            pltpu.VMEM((GDN_HEADS, GDN_DK, GDN_DV), F32),
        ],
        compiler_params=_cparams(("parallel", "arbitrary")),
        name="gdn_prompt",
    )(za3, za3, ba3, cw, alog_row, dtb_row, nw)


def _dil_body(*refs):
    qkv = refs[:3 * DIL_GROUPS]
    bias_ref, o_ref = refs[3 * DIL_GROUPS], refs[3 * DIL_GROUPS + 1]
    kvbuf = refs[3 * DIL_GROUPS + 2:3 * DIL_GROUPS + 2 + 2 * DIL_GROUPS]
    og, lg = refs[-2], refs[-1]
    n = pl.program_id(2)
    e = DIL_HEAD_DIM
    blk = DIL_BLOCK
    tb = DIL_TB
    for g, (_, d) in enumerate(DIL_PATTERNS):
        q_ref, k_ref, v_ref = qkv[3 * g:3 * g + 3]
        kb, vb = kvbuf[2 * g], kvbuf[2 * g + 1]
        tail = blk * d

        @pl.when(n == 0)
        def _(kb=kb, vb=vb, tail=tail):
            kb[pl.ds(0, tail), :] = jnp.zeros((tail, e), F32)
            vb[pl.ds(0, tail), :] = jnp.zeros((tail, e), F32)

        kb[pl.ds(tail, tb), :] = k_ref[...]
        vb[pl.ds(tail, tb), :] = v_ref[...]
        bp = bias_ref[g, :, 0:blk]
        bc = bias_ref[g, :, blk:2 * blk]
        shift = int(math.log2(d))

        def rows_at(base, d=d):
            return pl.ds(base, blk) if d == 1 else pl.ds(base, blk, stride=d)

        def body(it, carry, d=d, g=g, q_ref=q_ref, kb=kb, vb=vb, bp=bp, bc=bc, shift=shift, tail=tail,
                 rows_at=rows_at):
            us = range(DIL_UNROLL)
            idx = [it * DIL_UNROLL + u for u in us]
            sub = [i >> shift for i in idx]
            base = [sub[u] * tail + (idx[u] & (d - 1)) for u in us]
            q = [q_ref[rows_at(base[u]), :].astype(BF16) for u in us]
            sp = [_dot_nt(q[u], kb[rows_at(base[u]), :]) for u in us]
            sc = [_dot_nt(q[u], kb[rows_at(base[u] + tail), :]) for u in us]
            pp, pc, l, lse = [], [], [], []
            for u in us:
                spu = jnp.where((n > 0) | (sub[u] > 0), sp[u] * (e ** -0.5) + bp, -jnp.inf)
                scu = sc[u] * (e ** -0.5) + bc
                m = jnp.maximum(jnp.max(spu, axis=-1, keepdims=True), jnp.max(scu, axis=-1, keepdims=True))
                ppu = jnp.exp(spu - m)
                pcu = jnp.exp(scu - m)
                lu = jnp.sum(ppu, axis=-1, keepdims=True) + jnp.sum(pcu, axis=-1, keepdims=True)
                pp.append(ppu)
                pc.append(pcu)
                l.append(lu)
                lse.append(m + jnp.log(lu))
            op = [_dot(pp[u], vb[rows_at(base[u]), :]) for u in us]
            oc = [_dot(pc[u], vb[rows_at(base[u] + tail), :]) for u in us]
            for u in us:
                og[g, rows_at(base[u]), :] = (op[u] + oc[u]) / l[u]
                lg[g, rows_at(base[u]), :] = jnp.broadcast_to(lse[u], (blk, e))
            return carry

        lax.fori_loop(0, tb // blk // DIL_UNROLL, body, 0)
        kb[pl.ds(0, tail), :] = kb[pl.ds(tb, tail), :]
        vb[pl.ds(0, tail), :] = vb[pl.ds(tb, tail), :]

    lses = [lg[g] for g in range(DIL_GROUPS)]
    m = functools.reduce(jnp.maximum, lses)
    es = [jnp.exp(l - m) for l in lses]
    den = functools.reduce(lambda p, q_: p + q_, es)
    o_ref[...] = functools.reduce(lambda p, q_: p + q_, [(es[g] / den) * og[g] for g in range(DIL_GROUPS)])


def _dil_prompt(zb3, bias):
    b, t, _ = zb3.shape
    tb = DIL_TB
    e = DIL_HEAD_DIM
    specs, args = [], []
    for g in range(DIL_GROUPS):
        for off in (ZB_Q, ZB_K, ZB_V):
            cb = (off + g * MIX_W) // e
            specs.append(pl.BlockSpec((None, tb, e), lambda i, h, n, cb=cb: (i, n, cb + h)))
            args.append(zb3)
    specs.append(pl.BlockSpec((DIL_GROUPS, None, DIL_BLOCK, 2 * DIL_BLOCK), lambda i, h, n: (0, h, 0, 0)))
    scratch = []
    for _, d in DIL_PATTERNS:
        scratch += [pltpu.VMEM((DIL_BLOCK * d + tb, e), F32)] * 2
    scratch += [pltpu.VMEM((DIL_GROUPS, tb, e), F32)] * 2
    return pl.pallas_call(
        _dil_body,
        grid=(b, DIL_GROUP_HEADS, t // tb),
        in_specs=specs,
        out_specs=pl.BlockSpec((None, tb, e), lambda i, h, n: (i, n, h)),
        out_shape=jax.ShapeDtypeStruct((b, t, MIX_W), F32),
        scratch_shapes=scratch,
        compiler_params=_cparams(("parallel", "parallel", "arbitrary")),
        name="dil_prompt",
    )(*args, bias)


def _lru_gates(xc, wa_ref, ba_ref, wx_ref, bx_ref, lam_ref):
    xb = xc.astype(BF16)
    r = jax.nn.sigmoid(_dot(xb, wa_ref[...]) + ba_ref[...])
    i = jax.nn.sigmoid(_dot(xb, wx_ref[...]) + bx_ref[...])
    log_a = -LRU_C * r * _softplus(-lam_ref[...])
    a = jnp.exp(log_a)
    bt = jnp.sqrt(-jnp.tanh(log_a) * (jnp.exp(2.0 * log_a) + 1.0)) * i * xc
    return a, bt


def _cd_body(gb_ref, gc_ref, xi_ref, xd_ref, gd_ref, scw_ref, lcw_ref, lcb_ref, wa_ref, ba_ref, wx_ref,
             bx_ref, lam_ref, oc_ref, od_ref, scst_ref, lruh_ref, cbuf, dbuf, a_s, b_s, h_s, *, tt):
    t = pl.program_id(1)

    @pl.when(t == 0)
    def _():
        cbuf[pl.ds(0, SUBLANES), :] = jnp.zeros((SUBLANES, MIX_W), F32)
        dbuf[pl.ds(0, SUBLANES), :] = jnp.zeros((SUBLANES, MIX_W), F32)
        h_s[...] = jnp.zeros_like(h_s)

    cbuf[pl.ds(SUBLANES, tt), :] = gc_ref[...] * xi_ref[...]
    u = scw_ref[0:1, :] * cbuf[pl.ds(SUBLANES - 2, tt), :]
    for i in range(1, SC_CONV):
        u = u + scw_ref[i:i + 1, :] * cbuf[pl.ds(SUBLANES - 2 + i, tt), :]
    oc_ref[...] = gb_ref[...] * u
    tail = cbuf[pl.ds(tt, SUBLANES), :]
    scst_ref[...] = tail
    cbuf[pl.ds(0, SUBLANES), :] = tail

    dbuf[pl.ds(SUBLANES, tt), :] = xd_ref[...]
    xc = lcw_ref[0:1, :] * dbuf[pl.ds(SUBLANES - 3, tt), :]
    for i in range(1, LRU_CONV):
        xc = xc + lcw_ref[i:i + 1, :] * dbuf[pl.ds(SUBLANES - 3 + i, tt), :]
    xc = xc + lcb_ref[...]
    dbuf[pl.ds(0, SUBLANES), :] = dbuf[pl.ds(tt, SUBLANES), :]
    a, bt = _lru_gates(xc, wa_ref, ba_ref, wx_ref, bx_ref, lam_ref)
    a_s[...] = a
    b_s[...] = bt

    def scan(g, h):
        rows = pl.ds(pl.multiple_of(g * SUBLANES, SUBLANES), SUBLANES)
        a8 = a_s[rows, :]
        b8 = b_s[rows, :]
        out = []
        for r in range(SUBLANES):
            h = a8[r:r + 1, :] * h + b8[r:r + 1, :]
            out.append(h)
        a_s[rows, :] = jnp.concatenate(out, axis=0)
        return h

    h_last = lax.fori_loop(0, tt // SUBLANES, scan, h_s[0:1, :])
    h_s[...] = jnp.broadcast_to(h_last, h_s.shape)
    lruh_ref[...] = jnp.broadcast_to(h_last, h_s.shape)
    od_ref[...] = a_s[...] * jax.nn.gelu(gd_ref[...])


def _cd_prompt(zb3, scw, lcw, lcb, wa, ba, wx, bx, lam):
    b, t, _ = zb3.shape
    tt = min(t, 512)
    blk = (None, tt, MIX_W)
    c0 = ZB_C // MIX_W

    def zspec(c):
        return pl.BlockSpec(blk, lambda i, j: (i, j, c))

    def full(shape):
        return pl.BlockSpec(shape, lambda i, j: (0,) * len(shape))

    st = pl.BlockSpec((None, SUBLANES, MIX_W), lambda i, j: (i, 0, 0))
    return pl.pallas_call(
        functools.partial(_cd_body, tt=tt),
        grid=(b, t // tt),
        in_specs=[zspec(c0), zspec(c0 + 1), zspec(c0 + 2), zspec(c0 + 3), zspec(c0 + 4),
                  full((SC_CONV, MIX_W)), full((LRU_CONV, MIX_W)), full((1, MIX_W)),
                  full((MIX_W, MIX_W)), full((1, MIX_W)), full((MIX_W, MIX_W)), full((1, MIX_W)),
                  full((1, MIX_W))],
        out_specs=[pl.BlockSpec(blk, lambda i, j: (i, j, 0)), pl.BlockSpec(blk, lambda i, j: (i, j, 0)), st, st],
        out_shape=[
            jax.ShapeDtypeStruct((b, t, MIX_W), F32),
            jax.ShapeDtypeStruct((b, t, MIX_W), F32),
            jax.ShapeDtypeStruct((b, SUBLANES, MIX_W), F32),
            jax.ShapeDtypeStruct((b, SUBLANES, MIX_W), F32),
        ],
        scratch_shapes=[
            pltpu.VMEM((tt + SUBLANES, MIX_W), F32),
            pltpu.VMEM((tt + SUBLANES, MIX_W), F32),
            pltpu.VMEM((tt, MIX_W), F32),
            pltpu.VMEM((tt, MIX_W), F32),
            pltpu.VMEM((SUBLANES, MIX_W), F32),
        ],
        compiler_params=_cparams(("parallel", "arbitrary")),
        name="cd_prompt",
    )(zb3, zb3, zb3, zb3, zb3, scw, lcw, lcb, wa, ba, wx, bx, lam)


def _merge_body(x_ref, oa_ref, ob_ref, oc_ref, od_ref, g0_ref, g1_ref, g2_ref, g3_ref, wbr_ref, wo_ref, o_ref):
    y = None
    for nbr, (br, g_ref) in enumerate(zip((oa_ref, ob_ref, oc_ref, od_ref), (g0_ref, g1_ref, g2_ref, g3_ref))):
        yb = _dot(br[...], wbr_ref[nbr]) * g_ref[...].astype(F32)
        y = yb if y is None else y + yb
    o_ref[...] = x_ref[...] + _dot(y, wo_ref[...])


def _merge(x, oa, ob, oc, od, gates, wbr, wo):
    m = x.shape[0]
    tm = min(m, 512)
    row = lambda w: pl.BlockSpec((tm, w), lambda i: (i, 0))
    gate_specs = [pl.BlockSpec((tm, D_MODEL), lambda i, n=n: (i, n)) for n in range(N_BRANCH)]
    return pl.pallas_call(
        _merge_body,
        grid=(m // tm,),
        in_specs=[row(D_MODEL)] + [row(MIX_W)] * N_BRANCH + gate_specs +
                 [pl.BlockSpec((N_BRANCH, MIX_W, D_MODEL), lambda i: (0, 0, 0)),
                  pl.BlockSpec((D_MODEL, D_MODEL), lambda i: (0, 0))],
        out_specs=row(D_MODEL),
        out_shape=jax.ShapeDtypeStruct((m, D_MODEL), F32),
        compiler_params=_cparams(("parallel",)),
        name="merge",
    )(x, oa, ob, oc, od, *([gates] * N_BRANCH), wbr, wo)


def _norm_body(x_ref, w_ref, o_ref):
    o_ref[...] = _rms(x_ref[...], w_ref[...])


def _final_norm(x, w):
    m = x.shape[0]
    tm = min(m, 1024)
    return pl.pallas_call(
        _norm_body,
        grid=(m // tm,),
        in_specs=[pl.BlockSpec((tm, D_MODEL), lambda i: (i, 0)), pl.BlockSpec((1, D_MODEL), lambda i: (0, 0))],
        out_specs=pl.BlockSpec((tm, D_MODEL), lambda i: (i, 0)),
        out_shape=jax.ShapeDtypeStruct((m, D_MODEL), F32),
        compiler_params=_cparams(("parallel",)),
        name="final_norm",
    )(x, w)


def _spw_body(za_ref, zb_ref, gcs_ref, scs_ref, lcs_ref, lh_ref, gcw_ref, scw_ref, lcw_ref, lcb_ref, wa_ref,
              ba_ref, wx_ref, bx_ref, lam_ref, act_ref, gcs_o, oc_ref, scs_o, od_ref, lh_o, lcs_o):
    w = A_QKV_W
    x = za_ref[:, 0:w]
    y = gcw_ref[GDN_CONV - 1:GDN_CONV, :] * x
    for i in range(GDN_CONV - 1):
        y = y + gcw_ref[i:i + 1, :] * gcs_ref[:, i * w:(i + 1) * w]
    act_ref[...] = jax.nn.silu(y)
    gcs_o[:, 0:(GDN_CONV - 2) * w] = gcs_ref[:, w:(GDN_CONV - 1) * w]
    gcs_o[:, (GDN_CONV - 2) * w:(GDN_CONV - 1) * w] = x

    w = MIX_W
    gate_b = zb_ref[:, ZB_C:ZB_C + w]
    ci = zb_ref[:, ZB_C + w:ZB_C + 2 * w] * zb_ref[:, ZB_C + 2 * w:ZB_C + 3 * w]
    u = scw_ref[SC_CONV - 1:SC_CONV, :] * ci
    for i in range(SC_CONV - 1):
        u = u + scw_ref[i:i + 1, :] * scs_ref[:, i * w:(i + 1) * w]
    oc_ref[...] = gate_b * u
    scs_o[:, 0:(SC_CONV - 2) * w] = scs_ref[:, w:(SC_CONV - 1) * w]
    scs_o[:, (SC_CONV - 2) * w:(SC_CONV - 1) * w] = ci

    xd = zb_ref[:, ZB_D:ZB_D + w]
    gate_d = zb_ref[:, ZB_D + w:ZB_D + 2 * w]
    xc = lcw_ref[LRU_CONV - 1:LRU_CONV, :] * xd
    for i in range(LRU_CONV - 1):
        xc = xc + lcw_ref[i:i + 1, :] * lcs_ref[:, i * w:(i + 1) * w]
    xc = xc + lcb_ref[...]
    a, bt = _lru_gates(xc, wa_ref, ba_ref, wx_ref, bx_ref, lam_ref)
    hnew = a * lh_ref[...] + bt
    lh_o[...] = hnew
    od_ref[...] = hnew * jax.nn.gelu(gate_d)
    lcs_o[:, 0:(LRU_CONV - 2) * w] = lcs_ref[:, w:(LRU_CONV - 1) * w]
    lcs_o[:, (LRU_CONV - 2) * w:(LRU_CONV - 1) * w] = xd


def _sample_pointwise(za, zb, gcs, scs, lcs, lh, gcw, scw, lcw, lcb, wa, ba, wx, bx, lam):
    bd = za.shape[0]
    shapes = [(bd, A_QKV_W), gcs.shape, (bd, MIX_W), scs.shape, (bd, MIX_W), lh.shape, lcs.shape]
    return pl.pallas_call(
        _spw_body,
        out_shape=[jax.ShapeDtypeStruct(s, F32) for s in shapes],
        compiler_params=pltpu.CompilerParams(vmem_limit_bytes=VMEM_LIMIT),
        name="sample_pointwise",
    )(za, zb, gcs, scs, lcs, lh, gcw, scw, lcw, lcb, wa, ba, wx, bx, lam)


def _col(row, diag):
    return jnp.sum(jnp.where(diag, row, 0.0), axis=1, keepdims=True)


def _sstep_body(act_ref, za_ref, zb_ref, ba_ref, s_ref, alog_ref, dtb_ref, nw_ref,
                k0_ref, v0_ref, k1_ref, v1_ref, k2_ref, v2_ref, bb_ref, bn_ref,
                oa_ref, ob_ref, s_out_ref):
    e = GDN_DK
    ri = lax.broadcasted_iota(jnp.int32, (e, e), 0)
    ci = lax.broadcasted_iota(jnp.int32, (e, e), 1)
    diag = ri == ci

    ba = ba_ref[...]
    beta_all = jax.nn.sigmoid(ba)
    g_all = -jnp.exp(alog_ref[...]) * _softplus(ba + dtb_ref[...])
    for h in range(GDN_HEADS):
        q = act_ref[:, h * e:(h + 1) * e]
        k = act_ref[:, GDN_HEADS * e + h * e:GDN_HEADS * e + (h + 1) * e]
        v = act_ref[:, 2 * GDN_HEADS * e + h * GDN_DV:2 * GDN_HEADS * e + (h + 1) * GDN_DV]
        q = q * lax.rsqrt(jnp.sum(q * q, axis=-1, keepdims=True) + NORM_EPS) * (e ** -0.5)
        k = k * lax.rsqrt(jnp.sum(k * k, axis=-1, keepdims=True) + NORM_EPS)
        beta = beta_all[:, h:h + 1]
        eg = jnp.exp(g_all[:, GDN_HEADS + h:GDN_HEADS + h + 1])
        s = s_ref[h]
        kcol = _col(k, diag)
        qcol = _col(q, diag)
        ks = jnp.sum(kcol * s, axis=0, keepdims=True)
        qs = jnp.sum(qcol * s, axis=0, keepdims=True)
        v_new = beta * v - (beta * eg) * ks
        qk = jnp.sum(q * k, axis=-1, keepdims=True)
        o = eg * qs + qk * v_new
        s_out_ref[h] = s * eg + kcol * v_new
        hv = slice(h * GDN_DV, (h + 1) * GDN_DV)
        oa_ref[:, hv] = _rms(o, nw_ref[...]) * jax.nn.silu(za_ref[:, A_QKV_W + h * GDN_DV:A_QKV_W + (h + 1) * GDN_DV])

    kv = ((k0_ref, v0_ref), (k1_ref, v1_ref), (k2_ref, v2_ref))
    for h in range(DIL_GROUP_HEADS):
        hs = slice(h * e, (h + 1) * e)
        outs, lses = [], []
        for g in range(DIL_GROUPS):
            c0 = g * MIX_W + h * e
            q = zb_ref[:, ZB_Q + c0:ZB_Q + c0 + e]
            kn = zb_ref[:, ZB_K + c0:ZB_K + c0 + e]
            vn = zb_ref[:, ZB_V + c0:ZB_V + c0 + e]
            kb = kv[g][0][:, h, :]
            vb = kv[g][1][:, h, :]
            col = g * DIL_GROUP_HEADS + h
            sb = jnp.sum(kb * q, axis=-1, keepdims=True) * (e ** -0.5) + bb_ref[:, col:col + 1]
            sn = jnp.sum(kn * q, axis=-1, keepdims=True) * (e ** -0.5) + bn_ref[:, col:col + 1]
            m = jnp.maximum(jnp.max(sb, axis=0, keepdims=True), sn)
            pb = jnp.exp(sb - m)
            pn = jnp.exp(sn - m)
            l = jnp.sum(pb, axis=0, keepdims=True) + pn
            outs.append((jnp.sum(pb * vb, axis=0, keepdims=True) + pn * vn) / l)
            lses.append(m + jnp.log(l))
        m = functools.reduce(jnp.maximum, lses)
        es = [jnp.exp(l - m) for l in lses]
        den = functools.reduce(lambda p, q_: p + q_, es)
        ob_ref[:, hs] = functools.reduce(lambda p, q_: p + q_, [(ei / den) * oi for ei, oi in zip(es, outs)])


def _sample_step(act3, za3, zb3, ba3, s_in, alog_row, dtb_row, nw, caches, layer, bias_buf, bias_new):
    bd = act3.shape[0]

    def vec(w):
        return pl.BlockSpec((None, 1, w), lambda i: (i, 0, 0))

    def full(shape):
        return pl.BlockSpec(shape, lambda i: (0,) * len(shape))

    cache_specs, cache_args = [], []
    for cch in caches:
        for kvi in range(2):
            cache_specs.append(pl.BlockSpec((None, None, None, DIL_BLOCK, None, DIL_GROUP_HEADS, DIL_HEAD_DIM),
                                            lambda i, kvi=kvi: (layer, i, kvi, 0, 0, 0, 0)))
            cache_args.append(cch)
    st = pl.BlockSpec((None, GDN_HEADS, GDN_DK, GDN_DV), lambda i: (i, 0, 0, 0))
    return pl.pallas_call(
        _sstep_body,
        grid=(bd,),
        in_specs=[vec(A_QKV_W), vec(ZA_W), vec(ZB_W), vec(BA_W), st,
                  full((1, BA_W)), full((1, BA_W)), full((1, GDN_DV))] + cache_specs +
                 [full((DIL_BLOCK, DIL_HEADS)), full((1, DIL_HEADS))],
        out_specs=[vec(MIX_W), vec(MIX_W), st],
        out_shape=[jax.ShapeDtypeStruct((bd, 1, MIX_W), F32), jax.ShapeDtypeStruct((bd, 1, MIX_W), F32),
                   jax.ShapeDtypeStruct(s_in.shape, F32)],
        compiler_params=_cparams(("parallel",)),
        name="sample_step",
    )(act3, za3, zb3, ba3, s_in, alog_row, dtb_row, nw, *cache_args, bias_buf, bias_new)


def _shift_copies(srcs, dsts, sem):
    copies = []
    for src, dst in zip(srcs, dsts):
        bd, rows = src.shape[1], src.shape[3]
        for l in range(DEPTH):
            copies.append((src.at[l, pl.ds(0, bd), pl.ds(0, 2), pl.ds(1, rows - 1)],
                           dst.at[l, pl.ds(0, bd), pl.ds(0, 2), pl.ds(0, rows - 1)]))
        copies.append((src.at[pl.ds(0, DEPTH), pl.ds(0, bd), pl.ds(0, 2), pl.ds(rows - 1, 1)],
                       dst.at[pl.ds(0, DEPTH), pl.ds(0, bd), pl.ds(0, 2), pl.ds(rows - 1, 1)]))
    return [pltpu.make_async_copy(s, d, sem.at[i]) for i, (s, d) in enumerate(copies)]


def _shift_body(c0, c1, c2, o0, o1, o2, sem):
    copies = _shift_copies((c0, c1, c2), (o0, o1, o2), sem)
    for cp in copies:
        cp.start()
    for cp in copies:
        cp.wait()


def _shift_caches(caches):
    n_copies = DIL_GROUPS * (DEPTH + 1)
    any_spec = pl.BlockSpec(memory_space=pl.ANY)
    return pl.pallas_call(
        _shift_body,
        in_specs=[any_spec] * DIL_GROUPS,
        out_specs=[any_spec] * DIL_GROUPS,
        out_shape=[jax.ShapeDtypeStruct(c.shape, F32) for c in caches],
        scratch_shapes=[pltpu.SemaphoreType.DMA((n_copies,))],
        name="shift_caches",
    )(*caches)


def _setrow_body(buf_ref, k_ref, v_ref, o_ref):
    del buf_ref
    e = DIL_HEAD_DIM
    for h in range(DIL_GROUP_HEADS):
        o_ref[0, 0, h:h + 1, :] = k_ref[:, h * e:(h + 1) * e]
        o_ref[1, 0, h:h + 1, :] = v_ref[:, h * e:(h + 1) * e]


def _set_last_row(buf, layer, zb3, gi):
    _, bd, _, rows, nh, e = buf.shape
    ck = (ZB_K + gi * MIX_W) // MIX_W
    cv = (ZB_V + gi * MIX_W) // MIX_W
    return pl.pallas_call(
        _setrow_body,
        grid=(bd,),
        in_specs=[pl.BlockSpec(memory_space=pl.ANY),
                  pl.BlockSpec((None, 1, MIX_W), lambda i: (i, 0, ck)),
                  pl.BlockSpec((None, 1, MIX_W), lambda i: (i, 0, cv))],
        out_specs=pl.BlockSpec((None, None, 2, 1, nh, e), lambda i: (layer, i, 0, rows - 1, 0, 0)),
        out_shape=jax.ShapeDtypeStruct(buf.shape, F32),
        input_output_aliases={0: 0},
        compiler_params=_cparams(("parallel",)),
        name="set_last_row",
    )(buf, zb3, zb3)


def _t5_bucket(dist):
    exact = REL_BUCKETS // 2
    d = np.maximum(dist, 1).astype(np.float32)
    large = exact + (np.log(d / exact) / math.log(REL_MAX_DIST / exact) * (REL_BUCKETS - exact)).astype(np.int32)
    return np.where(dist < exact, dist, np.minimum(large, REL_BUCKETS - 1)).astype(np.int32)


def _prompt_bias(rel_bias):
    blk = DIL_BLOCK
    period = 4 * blk
    out = []
    for gi, (window, dilation) in enumerate(DIL_PATTERNS):
        n_off = window // dilation
        tab = rel_bias[:, gi * DIL_GROUP_HEADS:(gi + 1) * DIL_GROUP_HEADS]
        vals = tab[_t5_bucket(np.arange(n_off + 1) * dilation)].T.astype(F32)
        w = jnp.full((DIL_GROUP_HEADS, period), -jnp.inf, F32)
        w = lax.dynamic_update_slice(w, vals, (0, blk - 1))
        r = jnp.tile(w, (1, blk + 1))[:, :blk * (period + 1)].reshape(DIL_GROUP_HEADS, blk, period + 1)
        out.append(r[:, :, 0:2 * blk][:, :, ::-1])
    return jnp.stack(out, axis=0)


def _step_bias(rel_bias):
    cols_buf, cols_new = [], []
    for gi, (window, dilation) in enumerate(DIL_PATTERNS):
        n_off = window // dilation
        tab = rel_bias[:, gi * DIL_GROUP_HEADS:(gi + 1) * DIL_GROUP_HEADS]
        j = n_off - np.arange(n_off)
        cols_buf.append(tab[_t5_bucket(j * dilation)])
        cols_new.append(tab[_t5_bucket(np.zeros((1,), np.int64))])
    return jnp.concatenate(cols_buf, axis=1).astype(F32), jnp.concatenate(cols_new, axis=1).astype(F32)


def _block_diag(w):
    n, c, _ = w.shape
    eye = jnp.eye(n, dtype=w.dtype)
    return (eye[:, None, :, None] * w[:, :, None, :]).reshape(n * c, n * c)


def _layer_weights(p, l):
    w_in = p['w_in'][l]
    b0, c0 = A_W, A_W + B_W
    zg0 = A_QKV_W
    ba0 = A_QKV_W + GDN_HEADS * GDN_DV
    w_all = jnp.concatenate([w_in[:, 0:ba0], w_in[:, b0:], p['w_gate'][l]], axis=1).astype(BF16)
    b_all = jnp.concatenate([jnp.zeros((ZA_W + ZB_W,), F32), p['b_gate'][l]])[None]
    w_ba = jnp.pad(w_in[:, ba0:b0], ((0, 0), (0, BA_W - 2 * GDN_HEADS))).astype(BF16)
    pad_row = lambda v: jnp.pad(v, (GDN_HEADS, BA_W - 2 * GDN_HEADS))[None]
    row = lambda v: v[None].astype(F32)
    del zg0, c0
    return dict(
        n1=row(p['norm_ffn1'][l]), gu1=p['ffn1_w_gu'][l].astype(BF16), dn1=p['ffn1_w_down'][l].astype(BF16),
        nm=row(p['norm_mix'][l]), w_all=w_all, b_all=b_all, w_ba=w_ba,
        gcw=p['gdn_conv_w'][l], alog=pad_row(p['gdn_a_log'][l]), dtb=pad_row(p['gdn_dt_bias'][l]),
        gnw=row(p['gdn_norm_w'][l]), scw=p['sc_conv_w'][l], lcw=p['lru_conv_w'][l], lcb=row(p['lru_conv_b'][l]),
        wa=_block_diag(p['lru_wa'][l]).astype(BF16), ba=row(p['lru_ba'][l]),
        wx=_block_diag(p['lru_wx'][l]).astype(BF16), bx=row(p['lru_bx'][l]), lam=row(p['lru_lambda'][l]),
        wbr=p['w_branch'][l].astype(BF16), wo=p['w_o'][l].astype(BF16),
        n2=row(p['norm_ffn2'][l]), gu2=p['ffn2_w_gu'][l].astype(BF16), dn2=p['ffn2_w_down'][l].astype(BF16),
    )


def _prompt_layer(x, w, bias, b, t):
    m = b * t
    x = _ffn(x, w['n1'], w['gu1'], w['dn1'])
    za, zb, gates = _proj(x, w['nm'], w['w_all'], w['b_all'])
    ba = _proj_ba(x, w['nm'], w['w_ba'])
    za3 = za.reshape(b, t, ZA_W)
    zb3 = zb.reshape(b, t, ZB_W)
    o_a, s_gdn = _gdn_prompt(za3, ba.reshape(b, t, BA_W), w['gcw'], w['alog'], w['dtb'], w['gnw'])
    o_b = _dil_prompt(zb3, bias)
    o_c, o_d, sc_st, lru_h = _cd_prompt(zb3, w['scw'], w['lcw'], w['lcb'], w['wa'], w['ba'], w['wx'], w['bx'], w['lam'])
    x = _merge(x, o_a.reshape(m, MIX_W), o_b.reshape(m, MIX_W), o_c.reshape(m, MIX_W), o_d.reshape(m, MIX_W),
               gates, w['wbr'], w['wo'])
    x = _ffn(x, w['n2'], w['gu2'], w['dn2'])
    gdn_conv = za3[:, t - (GDN_CONV - 1):, 0:A_QKV_W]
    bufs = []
    for gi, (window, _) in enumerate(DIL_PATTERNS):
        rows = min(window, t)
        k = zb3[:, t - rows:, ZB_K + gi * MIX_W:ZB_K + (gi + 1) * MIX_W]
        v = zb3[:, t - rows:, ZB_V + gi * MIX_W:ZB_V + (gi + 1) * MIX_W]
        bufs.append(jnp.stack([k, v], axis=1).reshape(b, 2, rows, DIL_GROUP_HEADS, DIL_HEAD_DIM))
    sc_conv = sc_st[:, SUBLANES - (SC_CONV - 1):]
    lru_conv = zb3[:, t - (LRU_CONV - 1):, ZB_D:ZB_D + MIX_W]
    return x, (s_gdn, gdn_conv, bufs[0], bufs[1], bufs[2], sc_conv, lru_h[:, 0], lru_conv)


def _sample_layer(x, w, l, st, shifted, step_bias):
    bd = x.shape[0]
    s_gdn, gdn_conv, caches, sc_conv, lru_h, lru_conv = st
    x = _ffn(x, w['n1'], w['gu1'], w['dn1'])
    za, zb, gates = _proj(x, w['nm'], w['w_all'], w['b_all'])
    ba = _proj_ba(x, w['nm'], w['w_ba'])
    act, gdn_conv_new, o_c, sc_new, o_d, lru_h_new, lru_conv_new = _sample_pointwise(
        za, zb, gdn_conv.reshape(bd, -1), sc_conv.reshape(bd, -1), lru_conv.reshape(bd, -1), lru_h,
        w['gcw'], w['scw'], w['lcw'], w['lcb'], w['wa'], w['ba'], w['wx'], w['bx'], w['lam'])
    zb3 = zb.reshape(bd, 1, ZB_W)
    o_a, o_b, s_new = _sample_step(act.reshape(bd, 1, A_QKV_W), za.reshape(bd, 1, ZA_W), zb3,
                                   ba.reshape(bd, 1, BA_W), s_gdn, w['alog'], w['dtb'], w['gnw'], caches, l,
                                   *step_bias)
    shifted = [_set_last_row(buf, l, zb3, gi) for gi, buf in enumerate(shifted)]
    x = _merge(x, o_a.reshape(bd, MIX_W), o_b.reshape(bd, MIX_W), o_c, o_d, gates, w['wbr'], w['wo'])
    x = _ffn(x, w['n2'], w['gu2'], w['dn2'])
    new = (s_new, gdn_conv_new.reshape(gdn_conv.shape), sc_new.reshape(sc_conv.shape), lru_h_new,
           lru_conv_new.reshape(lru_conv.shape))
    return x, new, shifted


def kernel(x_prompt, x_sample, state_gdn, state_gdn_conv, cache_dil_w128, cache_dil_w512, cache_dil_w2048,
           state_sc_conv, state_lru, state_lru_conv, norm_ffn1, ffn1_w_gu, ffn1_w_down, norm_mix, w_in,
           gdn_conv_w, gdn_a_log, gdn_dt_bias, gdn_norm_w, rel_bias, sc_conv_w, lru_conv_w, lru_conv_b,
           lru_wa, lru_ba, lru_wx, lru_bx, lru_lambda, w_gate, b_gate, w_branch, w_o, norm_ffn2, ffn2_w_gu,
           ffn2_w_down, final_norm):
    p = dict(norm_ffn1=norm_ffn1, ffn1_w_gu=ffn1_w_gu, ffn1_w_down=ffn1_w_down, norm_mix=norm_mix, w_in=w_in,
             gdn_conv_w=gdn_conv_w, gdn_a_log=gdn_a_log, gdn_dt_bias=gdn_dt_bias, gdn_norm_w=gdn_norm_w,
             sc_conv_w=sc_conv_w, lru_conv_w=lru_conv_w, lru_conv_b=lru_conv_b, lru_wa=lru_wa, lru_ba=lru_ba,
             lru_wx=lru_wx, lru_bx=lru_bx, lru_lambda=lru_lambda, w_gate=w_gate, b_gate=b_gate,
             w_branch=w_branch, w_o=w_o, norm_ffn2=norm_ffn2, ffn2_w_gu=ffn2_w_gu, ffn2_w_down=ffn2_w_down)
    weights = [_layer_weights(p, l) for l in range(DEPTH)]
    fnw = final_norm[None]
    b, t, _ = x_prompt.shape
    bd = x_sample.shape[0]
    bias = _prompt_bias(rel_bias)
    step_bias = _step_bias(rel_bias)

    x = x_prompt.reshape(b * t, D_MODEL)
    p_states = []
    for l in range(DEPTH):
        x, new = _prompt_layer(x, weights[l], bias, b, t)
        p_states.append(new)
    y_prompt = _final_norm(x, fnw).reshape(b, t, D_MODEL)
    p_out = tuple(jnp.stack(zs, axis=0) for zs in zip(*p_states))

    caches = (cache_dil_w128, cache_dil_w512, cache_dil_w2048)
    shifted = _shift_caches(caches)
    strided = tuple(c.reshape(DEPTH, bd, 2, c.shape[3] // d, d, DIL_GROUP_HEADS, DIL_HEAD_DIM)
                    for c, (_, d) in zip(caches, DIL_PATTERNS))
    x = x_sample.reshape(bd, D_MODEL)
    s_states = []
    for l in range(DEPTH):
        st = (state_gdn[l], state_gdn_conv[l], strided, state_sc_conv[l], state_lru[l], state_lru_conv[l])
        x, new, shifted = _sample_layer(x, weights[l], l, st, shifted, step_bias)
        s_states.append(new)
    y_sample = _final_norm(x, fnw).reshape(bd, 1, D_MODEL)
    s_gdn, s_gdn_conv, s_sc_conv, s_lru, s_lru_conv = (jnp.stack(zs, axis=0) for zs in zip(*s_states))

    p_gdn, p_gdn_conv, p_w128, p_w512, p_w2048, p_sc_conv, p_lru, p_lru_conv = p_out
    return (y_prompt, y_sample, p_gdn, s_gdn, p_gdn_conv, s_gdn_conv, p_w128, shifted[0], p_w512, shifted[1],
            p_w2048, shifted[2], p_sc_conv, s_sc_conv, p_lru, s_lru, p_lru_conv, s_lru_conv)
```

```python
import functools
import math

import numpy as np
import jax
import jax.numpy as jnp
from jax import lax
from jax.experimental import pallas as pl
from jax.experimental.pallas import tpu as pltpu

D_MODEL = 1024
DEPTH = 4
MIX_W = D_MODEL // 2
D_FF = 2816
NORM_EPS = 1e-6
N_BRANCH = 4
GDN_HEADS = 4
GDN_DK = 128
GDN_DV = MIX_W // GDN_HEADS
GDN_CONV = 4
GDN_CHUNK = 64
DIL_PATTERNS = ((128, 1), (512, 4), (2048, 16))
DIL_GROUPS = len(DIL_PATTERNS)
DIL_GROUP_HEADS = 4
DIL_HEAD_DIM = MIX_W // DIL_GROUP_HEADS
DIL_HEADS = DIL_GROUPS * DIL_GROUP_HEADS
DIL_BLOCK = 128
REL_BUCKETS = 32
REL_MAX_DIST = 2048
SC_CONV = 3
LRU_BLOCKS = 8
LRU_BLOCK_W = MIX_W // LRU_BLOCKS
LRU_CONV = 4
LRU_C = 8.0
A_QKV_W = GDN_HEADS * (2 * GDN_DK + GDN_DV)
A_W = A_QKV_W + GDN_HEADS * GDN_DV + 2 * GDN_HEADS
B_W = 3 * DIL_HEADS * DIL_HEAD_DIM
C_W = 3 * MIX_W
D_W = 2 * MIX_W

LANES = 128
SUBLANES = 8
VMEM_LIMIT = 56 * 1024 * 1024

ZA_W = A_QKV_W + MIX_W
ZB_Q, ZB_K, ZB_V = 0, B_W // 3, 2 * B_W // 3
ZB_C = B_W
ZB_D = B_W + C_W
ZB_W = B_W + C_W + D_W
G_W = N_BRANCH * D_MODEL
PROJ_TN = 1024
BA_W = LANES
DIL_TB = DIL_BLOCK * max(d for _, d in DIL_PATTERNS)
DIL_UNROLL = 4

F32 = jnp.float32
BF16 = jnp.bfloat16
NT_DIMS = (((1,), (1,)), ((), ()))
TN_DIMS = (((0,), (0,)), ((), ()))


def _cparams(sem):
    return pltpu.CompilerParams(dimension_semantics=sem, vmem_limit_bytes=VMEM_LIMIT)


def _rms(x, w):
    return x * lax.rsqrt(jnp.mean(x * x, axis=-1, keepdims=True) + NORM_EPS) * w


def _softplus(x):
    return jnp.maximum(x, 0.0) + jnp.log1p(jnp.exp(-jnp.abs(x)))


def _dot(a, b):
    return jnp.dot(a.astype(BF16), b.astype(BF16), preferred_element_type=F32)


def _dot_nt(a, b):
    return lax.dot_general(a.astype(BF16), b.astype(BF16), NT_DIMS, preferred_element_type=F32)


def _dot_tn(a, b):
    return lax.dot_general(a.astype(BF16), b.astype(BF16), TN_DIMS, preferred_element_type=F32)


def _ffn_body(x_ref, nw_ref, wg_ref, wu_ref, wd_ref, o_ref, h_ref, acc_ref):
    j = pl.program_id(1)

    @pl.when(j == 0)
    def _():
        h_ref[...] = _rms(x_ref[...], nw_ref[...]).astype(BF16)
        acc_ref[...] = jnp.zeros_like(acc_ref)

    h = h_ref[...]
    g = _dot(h, wg_ref[...])
    u = _dot(h, wu_ref[...])
    acc_ref[...] += _dot(jax.nn.silu(g) * u, wd_ref[...])

    @pl.when(j == pl.num_programs(1) - 1)
    def _():
        o_ref[...] = x_ref[...] + 0.5 * acc_ref[...]


def _ffn(x, nw, w_gu, w_down):
    m = x.shape[0]
    tm = min(m, 512)
    nf = 2
    tf = D_FF // nf
    return pl.pallas_call(
        _ffn_body,
        grid=(m // tm, nf),
        in_specs=[
            pl.BlockSpec((tm, D_MODEL), lambda i, j: (i, 0)),
            pl.BlockSpec((1, D_MODEL), lambda i, j: (0, 0)),
            pl.BlockSpec((D_MODEL, tf), lambda i, j: (0, j)),
            pl.BlockSpec((D_MODEL, tf), lambda i, j: (0, nf + j)),
            pl.BlockSpec((tf, D_MODEL), lambda i, j: (j, 0)),
        ],
        out_specs=pl.BlockSpec((tm, D_MODEL), lambda i, j: (i, 0)),
        out_shape=jax.ShapeDtypeStruct((m, D_MODEL), F32),
        scratch_shapes=[pltpu.VMEM((tm, D_MODEL), BF16), pltpu.VMEM((tm, D_MODEL), F32)],
        compiler_params=_cparams(("parallel", "arbitrary")),
        name="ffn",
    )(x, nw, w_gu, w_gu, w_down)


def _proj_body(x_ref, nw_ref, w_ref, b_ref, za_ref, zb_ref, g_ref, h_ref, *, ja, jb):
    j = pl.program_id(1)

    @pl.when(j == 0)
    def _():
        h_ref[...] = _rms(x_ref[...], nw_ref[...]).astype(BF16)

    @pl.when(j < ja)
    def _():
        za_ref[...] = _dot(h_ref[...], w_ref[...])

    @pl.when((j >= ja) & (j < jb))
    def _():
        zb_ref[...] = _dot(h_ref[...], w_ref[...])

    @pl.when(j >= jb)
    def _():
        g_ref[...] = jax.nn.sigmoid(_dot(h_ref[...], w_ref[...]) + b_ref[...]).astype(BF16)


def _proj(x, nw, w, b):
    m = x.shape[0]
    tm = min(m, 1024)
    tn = PROJ_TN
    ja, jb = ZA_W // tn, (ZA_W + ZB_W) // tn
    nj = (ZA_W + ZB_W + G_W) // tn
    return pl.pallas_call(
        functools.partial(_proj_body, ja=ja, jb=jb),
        grid=(m // tm, nj),
        in_specs=[
            pl.BlockSpec((tm, D_MODEL), lambda i, j: (i, 0)),
            pl.BlockSpec((1, D_MODEL), lambda i, j: (0, 0)),
            pl.BlockSpec((D_MODEL, tn), lambda i, j: (0, j)),
            pl.BlockSpec((1, tn), lambda i, j: (0, jnp.maximum(j - jb, 0))),
        ],
        out_specs=[
            pl.BlockSpec((tm, tn), lambda i, j: (i, jnp.minimum(j, ja - 1))),
            pl.BlockSpec((tm, tn), lambda i, j: (i, jnp.clip(j - ja, 0, jb - ja - 1))),
            pl.BlockSpec((tm, tn), lambda i, j: (i, jnp.maximum(j - jb, 0))),
        ],
        out_shape=[
            jax.ShapeDtypeStruct((m, ZA_W), F32),
            jax.ShapeDtypeStruct((m, ZB_W), F32),
            jax.ShapeDtypeStruct((m, G_W), BF16),
        ],
        scratch_shapes=[pltpu.VMEM((tm, D_MODEL), BF16)],
        compiler_params=_cparams(("parallel", "arbitrary")),
        name="proj",
    )(x, nw, w, b)


def _proj_ba_body(x_ref, nw_ref, w_ref, o_ref):
    o_ref[...] = _dot(_rms(x_ref[...], nw_ref[...]), w_ref[...])


def _proj_ba(x, nw, w):
    m = x.shape[0]
    tm = min(m, 1024)
    return pl.pallas_call(
        _proj_ba_body,
        grid=(m // tm,),
        in_specs=[
            pl.BlockSpec((tm, D_MODEL), lambda i: (i, 0)),
            pl.BlockSpec((1, D_MODEL), lambda i: (0, 0)),
            pl.BlockSpec((D_MODEL, BA_W), lambda i: (0, 0)),
        ],
        out_specs=pl.BlockSpec((tm, BA_W), lambda i: (i, 0)),
        out_shape=jax.ShapeDtypeStruct((m, BA_W), F32),
        compiler_params=_cparams(("parallel",)),
        name="proj_ba",
    )(x, nw, w)


def _gdn_body(qkv_ref, zg_ref, ba_ref, cw_ref, alog_ref, dtb_ref, nw_ref, o_ref, s_out_ref,
              xbuf, act, beta_s, gc_s, u_s, wq_s, at_s, kd_s, s_s, *, tt):
    t = pl.program_id(1)
    c = GDN_CHUNK
    nchunk = tt // c

    @pl.when(t == 0)
    def _():
        xbuf[pl.ds(0, SUBLANES), :] = jnp.zeros((SUBLANES, A_QKV_W), F32)
        s_s[...] = jnp.zeros_like(s_s)

    xbuf[pl.ds(SUBLANES, tt), :] = qkv_ref[...]
    y = cw_ref[0:1, :] * xbuf[pl.ds(SUBLANES - 3, tt), :]
    for i in range(1, GDN_CONV):
        y = y + cw_ref[i:i + 1, :] * xbuf[pl.ds(SUBLANES - 3 + i, tt), :]
    act[...] = jax.nn.silu(y)
    xbuf[pl.ds(0, SUBLANES), :] = xbuf[pl.ds(tt, SUBLANES), :]

    ba = ba_ref[...]
    beta_s[...] = jax.nn.sigmoid(ba)
    g = -jnp.exp(alog_ref[...]) * _softplus(ba + dtb_ref[...])
    rowmod = lax.broadcasted_iota(jnp.int32, (tt, BA_W), 0) & (c - 1)
    sh = 1
    while sh < c:
        g = g + jnp.where(rowmod >= sh, pltpu.roll(g, sh, 0), 0.0)
        sh *= 2
    gc_s[...] = g

    ri = lax.broadcasted_iota(jnp.int32, (c, c), 0)
    ci = lax.broadcasted_iota(jnp.int32, (c, c), 1)
    causal = ri >= ci
    strict = ri > ci
    diag = ri == ci
    eye = jnp.where(diag, 1.0, 0.0).astype(F32)

    heads = range(GDN_HEADS)

    def prep(i, carry):
        chains = [(2 * i + j, h) for j in range(2) for h in heads]
        n = len(chains)
        rows_l, gcol_l, decay_l, q_l, k_l, kb_l, vb_l = [], [], [], [], [], [], []
        for ch, h in chains:
            rows = pl.ds(pl.multiple_of(ch * c, c), c)
            gcol = gc_s[rows, GDN_HEADS + h:GDN_HEADS + h + 1]
            grow = jnp.sum(jnp.where(diag, gcol, 0.0), axis=0, keepdims=True)
            decay = jnp.exp(jnp.where(causal, gcol - grow, -jnp.inf))
            q = act[rows, h * GDN_DK:(h + 1) * GDN_DK]
            k = act[rows, GDN_HEADS * GDN_DK + h * GDN_DK:GDN_HEADS * GDN_DK + (h + 1) * GDN_DK]
            v = act[rows, 2 * GDN_HEADS * GDN_DK + h * GDN_DV:2 * GDN_HEADS * GDN_DK + (h + 1) * GDN_DV]
            q = q * lax.rsqrt(jnp.sum(q * q, axis=-1, keepdims=True) + NORM_EPS) * (GDN_DK ** -0.5)
            k = k * lax.rsqrt(jnp.sum(k * k, axis=-1, keepdims=True) + NORM_EPS)
            beta = beta_s[rows, h:h + 1]
            rows_l.append(rows)
            gcol_l.append(gcol)
            decay_l.append(decay)
            q_l.append(q)
            k_l.append(k)
            kb_l.append(k * beta)
            vb_l.append(v * beta)
        kq_l = [_dot_nt(jnp.concatenate([kb_l[j], q_l[j]], axis=0), k_l[j]) for j in range(n)]
        for j, (ch, h) in enumerate(chains):
            at_s[rows_l[j], h * LANES:h * LANES + c] = kq_l[j][c:] * decay_l[j]
        qq = [-jnp.where(strict, kq_l[j][:c] * decay_l[j], 0.0) for j in range(n)]
        yy = [eye + qq[j] for j in range(n)]
        qq = [_dot(qq[j], qq[j]) for j in range(n)]
        for _ in range(int(math.log2(c)) - 2):
            yq = [_dot(jnp.concatenate([yy[j], qq[j]], axis=0), qq[j]) for j in range(n)]
            yy = [yy[j] + yq[j][:c] for j in range(n)]
            qq = [yq[j][c:] for j in range(n)]
        yq = [_dot(yy[j], qq[j]) for j in range(n)]
        tinv = [yy[j] + yq[j] for j in range(n)]
        egc = [jnp.exp(gcol_l[j]) for j in range(n)]
        sol = [_dot(tinv[j], jnp.concatenate([vb_l[j], kb_l[j] * egc[j]], axis=1)) for j in range(n)]
        for j, (ch, h) in enumerate(chains):
            hs = slice(h * GDN_DK, (h + 1) * GDN_DK)
            u_s[rows_l[j], h * GDN_DV:(h + 1) * GDN_DV] = sol[j][:, :GDN_DV]
            wq_s[pl.ds(pl.multiple_of(ch * 2 * c, 2 * c), c), hs] = sol[j][:, GDN_DV:]
            wq_s[pl.ds(pl.multiple_of(ch * 2 * c + c, c), c), hs] = q_l[j] * egc[j]
            kd_s[rows_l[j], hs] = k_l[j] * jnp.exp(gcol_l[j][c - 1:c, :] - gcol_l[j])
        return carry

    lax.fori_loop(0, nchunk // 2, prep, 0)

    def step(ch, carry):
        rows = pl.ds(pl.multiple_of(ch * c, c), c)
        glast_all = jnp.exp(gc_s[pl.ds(ch * c + c - 1, 1), :])
        hsl = [slice(h * GDN_DK, (h + 1) * GDN_DK) for h in heads]
        hvl = [slice(h * GDN_DV, (h + 1) * GDN_DV) for h in heads]
        st = [s_s[h] for h in heads]
        wq_rows = pl.ds(pl.multiple_of(ch * 2 * c, 2 * c), 2 * c)
        ws = [_dot(wq_s[wq_rows, hsl[h]], st[h]) for h in heads]
        v_new = [u_s[rows, hvl[h]] - ws[h][:c] for h in heads]
        av = [_dot(at_s[rows, h * LANES:h * LANES + c], v_new[h]) for h in heads]
        kv = [_dot_tn(kd_s[rows, hsl[h]], v_new[h]) for h in heads]
        for h in heads:
            s_s[h] = st[h] * glast_all[:, GDN_HEADS + h:GDN_HEADS + h + 1] + kv[h]
            o = _rms(ws[h][c:] + av[h], nw_ref[...]) * jax.nn.silu(zg_ref[rows, hvl[h]])
            o_ref[rows, hvl[h]] = o
        return carry

    lax.fori_loop(0, nchunk, step, 0)

    @pl.when(t == pl.num_programs(1) - 1)
    def _():
        s_out_ref[...] = s_s[...]


def _gdn_prompt(za3, ba3, cw, alog_row, dtb_row, nw):
    b, t, _ = za3.shape
    tt = min(t, 512)
    return pl.pallas_call(
        functools.partial(_gdn_body, tt=tt),
        grid=(b, t // tt),
        in_specs=[
            pl.BlockSpec((None, tt, A_QKV_W), lambda i, j: (i, j, 0)),
            pl.BlockSpec((None, tt, MIX_W), lambda i, j: (i, j, A_QKV_W // MIX_W)),
            pl.BlockSpec((None, tt, BA_W), lambda i, j: (i, j, 0)),
            pl.BlockSpec((GDN_CONV, A_QKV_W), lambda i, j: (0, 0)),
            pl.BlockSpec((1, BA_W), lambda i, j: (0, 0)),
            pl.BlockSpec((1, BA_W), lambda i, j: (0, 0)),
            pl.BlockSpec((1, GDN_DV), lambda i, j: (0, 0)),
        ],
        out_specs=[
            pl.BlockSpec((None, tt, MIX_W), lambda i, j: (i, j, 0)),
            pl.BlockSpec((None, GDN_HEADS, GDN_DK, GDN_DV), lambda i, j: (i, 0, 0, 0)),
        ],
        out_shape=[
            jax.ShapeDtypeStruct((b, t, MIX_W), F32),
            jax.ShapeDtypeStruct((b, GDN_HEADS, GDN_DK, GDN_DV), F32),
        ],
        scratch_shapes=[
            pltpu.VMEM((tt + SUBLANES, A_QKV_W), F32),
            pltpu.VMEM((tt, A_QKV_W), F32),
            pltpu.VMEM((tt, BA_W), F32),
            pltpu.VMEM((tt, BA_W), F32),
            pltpu.VMEM((tt, MIX_W), F32),
            pltpu.VMEM((2 * tt, MIX_W), F32),
            pltpu.VMEM((tt, GDN_HEADS * LANES), F32),
            pltpu.VMEM((tt, MIX_W), F32),
            pltpu.VMEM((GDN_HEADS, GDN_DK, GDN_DV), F32),
        ],
        compiler_params=_cparams(("parallel", "arbitrary")),
        name="gdn_prompt",
    )(za3, za3, ba3, cw, alog_row, dtb_row, nw)


def _dil_body(*refs):
    qkv = refs[:3 * DIL_GROUPS]
    bias_ref, o_ref = refs[3 * DIL_GROUPS], refs[3 * DIL_GROUPS + 1]
    kvbuf = refs[3 * DIL_GROUPS + 2:3 * DIL_GROUPS + 2 + 2 * DIL_GROUPS]
    og, lg = refs[-2], refs[-1]
    n = pl.program_id(2)
    e = DIL_HEAD_DIM
    blk = DIL_BLOCK
    tb = DIL_TB
    for g, (_, d) in enumerate(DIL_PATTERNS):
        q_ref, k_ref, v_ref = qkv[3 * g:3 * g + 3]
        kb, vb = kvbuf[2 * g], kvbuf[2 * g + 1]
        tail = blk * d

        @pl.when(n == 0)
        def _(kb=kb, vb=vb, tail=tail):
            kb[pl.ds(0, tail), :] = jnp.zeros((tail, e), F32)
            vb[pl.ds(0, tail), :] = jnp.zeros((tail, e), F32)

        kb[pl.ds(tail, tb), :] = k_ref[...]
        vb[pl.ds(tail, tb), :] = v_ref[...]
        bp = bias_ref[g, :, 0:blk]
        bc = bias_ref[g, :, blk:2 * blk]
        shift = int(math.log2(d))

        def rows_at(base, d=d):
            return pl.ds(base, blk) if d == 1 else pl.ds(base, blk, stride=d)

        def body(it, carry, d=d, g=g, q_ref=q_ref, kb=kb, vb=vb, bp=bp, bc=bc, shift=shift, tail=tail,
                 rows_at=rows_at):
            us = range(DIL_UNROLL)
            idx = [it * DIL_UNROLL + u for u in us]
            sub = [i >> shift for i in idx]
            base = [sub[u] * tail + (idx[u] & (d - 1)) for u in us]
            q = [q_ref[rows_at(base[u]), :].astype(BF16) for u in us]
            sp = [_dot_nt(q[u], kb[rows_at(base[u]), :]) for u in us]
            sc = [_dot_nt(q[u], kb[rows_at(base[u] + tail), :]) for u in us]
            pp, pc, l, lse = [], [], [], []
            for u in us:
                spu = jnp.where((n > 0) | (sub[u] > 0), sp[u] * (e ** -0.5) + bp, -jnp.inf)
                scu = sc[u] * (e ** -0.5) + bc
                m = jnp.maximum(jnp.max(spu, axis=-1, keepdims=True), jnp.max(scu, axis=-1, keepdims=True))
                ppu = jnp.exp(spu - m)
                pcu = jnp.exp(scu - m)
                lu = jnp.sum(ppu, axis=-1, keepdims=True) + jnp.sum(pcu, axis=-1, keepdims=True)
                pp.append(ppu)
                pc.append(pcu)
                l.append(lu)
                lse.append(m + jnp.log(lu))
            op = [_dot(pp[u], vb[rows_at(base[u]), :]) for u in us]
            oc = [_dot(pc[u], vb[rows_at(base[u] + tail), :]) for u in us]
            for u in us:
                og[g, rows_at(base[u]), :] = (op[u] + oc[u]) / l[u]
                lg[g, rows_at(base[u]), :] = jnp.broadcast_to(lse[u], (blk, e))
            return carry

        lax.fori_loop(0, tb // blk // DIL_UNROLL, body, 0)
        kb[pl.ds(0, tail), :] = kb[pl.ds(tb, tail), :]
        vb[pl.ds(0, tail), :] = vb[pl.ds(tb, tail), :]

    lses = [lg[g] for g in range(DIL_GROUPS)]
    m = functools.reduce(jnp.maximum, lses)
    es = [jnp.exp(l - m) for l in lses]
    den = functools.reduce(lambda p, q_: p + q_, es)
    o_ref[...] = functools.reduce(lambda p, q_: p + q_, [(es[g] / den) * og[g] for g in range(DIL_GROUPS)])


def _dil_prompt(zb3, bias):
    b, t, _ = zb3.shape
    tb = DIL_TB
    e = DIL_HEAD_DIM
    specs, args = [], []
    for g in range(DIL_GROUPS):
        for off in (ZB_Q, ZB_K, ZB_V):
            cb = (off + g * MIX_W) // e
            specs.append(pl.BlockSpec((None, tb, e), lambda i, h, n, cb=cb: (i, n, cb + h)))
            args.append(zb3)
    specs.append(pl.BlockSpec((DIL_GROUPS, None, DIL_BLOCK, 2 * DIL_BLOCK), lambda i, h, n: (0, h, 0, 0)))
    scratch = []
    for _, d in DIL_PATTERNS:
        scratch += [pltpu.VMEM((DIL_BLOCK * d + tb, e), F32)] * 2
    scratch += [pltpu.VMEM((DIL_GROUPS, tb, e), F32)] * 2
    return pl.pallas_call(
        _dil_body,
        grid=(b, DIL_GROUP_HEADS, t // tb),
        in_specs=specs,
        out_specs=pl.BlockSpec((None, tb, e), lambda i, h, n: (i, n, h)),
        out_shape=jax.ShapeDtypeStruct((b, t, MIX_W), F32),
        scratch_shapes=scratch,
        compiler_params=_cparams(("parallel", "parallel", "arbitrary")),
        name="dil_prompt",
    )(*args, bias)


def _lru_gates(xc, wa_ref, ba_ref, wx_ref, bx_ref, lam_ref):
    xb = xc.astype(BF16)
    r = jax.nn.sigmoid(_dot(xb, wa_ref[...]) + ba_ref[...])
    i = jax.nn.sigmoid(_dot(xb, wx_ref[...]) + bx_ref[...])
    log_a = -LRU_C * r * _softplus(-lam_ref[...])
    a = jnp.exp(log_a)
    bt = jnp.sqrt(-jnp.tanh(log_a) * (jnp.exp(2.0 * log_a) + 1.0)) * i * xc
    return a, bt


def _cd_body(gb_ref, gc_ref, xi_ref, xd_ref, gd_ref, scw_ref, lcw_ref, lcb_ref, wa_ref, ba_ref, wx_ref,
             bx_ref, lam_ref, oc_ref, od_ref, scst_ref, lruh_ref, cbuf, dbuf, a_s, b_s, h_s, *, tt):
    t = pl.program_id(1)

    @pl.when(t == 0)
    def _():
        cbuf[pl.ds(0, SUBLANES), :] = jnp.zeros((SUBLANES, MIX_W), F32)
        dbuf[pl.ds(0, SUBLANES), :] = jnp.zeros((SUBLANES, MIX_W), F32)
        h_s[...] = jnp.zeros_like(h_s)

    cbuf[pl.ds(SUBLANES, tt), :] = gc_ref[...] * xi_ref[...]
    u = scw_ref[0:1, :] * cbuf[pl.ds(SUBLANES - 2, tt), :]
    for i in range(1, SC_CONV):
        u = u + scw_ref[i:i + 1, :] * cbuf[pl.ds(SUBLANES - 2 + i, tt), :]
    oc_ref[...] = gb_ref[...] * u
    tail = cbuf[pl.ds(tt, SUBLANES), :]
    scst_ref[...] = tail
    cbuf[pl.ds(0, SUBLANES), :] = tail

    dbuf[pl.ds(SUBLANES, tt), :] = xd_ref[...]
    xc = lcw_ref[0:1, :] * dbuf[pl.ds(SUBLANES - 3, tt), :]
    for i in range(1, LRU_CONV):
        xc = xc + lcw_ref[i:i + 1, :] * dbuf[pl.ds(SUBLANES - 3 + i, tt), :]
    xc = xc + lcb_ref[...]
    dbuf[pl.ds(0, SUBLANES), :] = dbuf[pl.ds(tt, SUBLANES), :]
    a, bt = _lru_gates(xc, wa_ref, ba_ref, wx_ref, bx_ref, lam_ref)
    a_s[...] = a
    b_s[...] = bt

    def scan(g, h):
        rows = pl.ds(pl.multiple_of(g * SUBLANES, SUBLANES), SUBLANES)
        a8 = a_s[rows, :]
        b8 = b_s[rows, :]
        out = []
        for r in range(SUBLANES):
            h = a8[r:r + 1, :] * h + b8[r:r + 1, :]
            out.append(h)
        a_s[rows, :] = jnp.concatenate(out, axis=0)
        return h

    h_last = lax.fori_loop(0, tt // SUBLANES, scan, h_s[0:1, :])
    h_s[...] = jnp.broadcast_to(h_last, h_s.shape)
    lruh_ref[...] = jnp.broadcast_to(h_last, h_s.shape)
    od_ref[...] = a_s[...] * jax.nn.gelu(gd_ref[...])


def _cd_prompt(zb3, scw, lcw, lcb, wa, ba, wx, bx, lam):
    b, t, _ = zb3.shape
    tt = min(t, 512)
    blk = (None, tt, MIX_W)
    c0 = ZB_C // MIX_W

    def zspec(c):
        return pl.BlockSpec(blk, lambda i, j: (i, j, c))

    def full(shape):
        return pl.BlockSpec(shape, lambda i, j: (0,) * len(shape))

    st = pl.BlockSpec((None, SUBLANES, MIX_W), lambda i, j: (i, 0, 0))
    return pl.pallas_call(
        functools.partial(_cd_body, tt=tt),
        grid=(b, t // tt),
        in_specs=[zspec(c0), zspec(c0 + 1), zspec(c0 + 2), zspec(c0 + 3), zspec(c0 + 4),
                  full((SC_CONV, MIX_W)), full((LRU_CONV, MIX_W)), full((1, MIX_W)),
                  full((MIX_W, MIX_W)), full((1, MIX_W)), full((MIX_W, MIX_W)), full((1, MIX_W)),
                  full((1, MIX_W))],
        out_specs=[pl.BlockSpec(blk, lambda i, j: (i, j, 0)), pl.BlockSpec(blk, lambda i, j: (i, j, 0)), st, st],
        out_shape=[
            jax.ShapeDtypeStruct((b, t, MIX_W), F32),
            jax.ShapeDtypeStruct((b, t, MIX_W), F32),
            jax.ShapeDtypeStruct((b, SUBLANES, MIX_W), F32),
            jax.ShapeDtypeStruct((b, SUBLANES, MIX_W), F32),
        ],
        scratch_shapes=[
            pltpu.VMEM((tt + SUBLANES, MIX_W), F32),
            pltpu.VMEM((tt + SUBLANES, MIX_W), F32),
            pltpu.VMEM((tt, MIX_W), F32),
            pltpu.VMEM((tt, MIX_W), F32),
            pltpu.VMEM((SUBLANES, MIX_W), F32),
        ],
        compiler_params=_cparams(("parallel", "arbitrary")),
        name="cd_prompt",
    )(zb3, zb3, zb3, zb3, zb3, scw, lcw, lcb, wa, ba, wx, bx, lam)


def _merge_body(x_ref, oa_ref, ob_ref, oc_ref, od_ref, g0_ref, g1_ref, g2_ref, g3_ref, wbr_ref, wo_ref, o_ref):
    y = None
    for nbr, (br, g_ref) in enumerate(zip((oa_ref, ob_ref, oc_ref, od_ref), (g0_ref, g1_ref, g2_ref, g3_ref))):
        yb = _dot(br[...], wbr_ref[nbr]) * g_ref[...].astype(F32)
        y = yb if y is None else y + yb
    o_ref[...] = x_ref[...] + _dot(y, wo_ref[...])


def _merge(x, oa, ob, oc, od, gates, wbr, wo):
    m = x.shape[0]
    tm = min(m, 512)
    row = lambda w: pl.BlockSpec((tm, w), lambda i: (i, 0))
    gate_specs = [pl.BlockSpec((tm, D_MODEL), lambda i, n=n: (i, n)) for n in range(N_BRANCH)]
    return pl.pallas_call(
        _merge_body,
        grid=(m // tm,),
        in_specs=[row(D_MODEL)] + [row(MIX_W)] * N_BRANCH + gate_specs +
                 [pl.BlockSpec((N_BRANCH, MIX_W, D_MODEL), lambda i: (0, 0, 0)),
                  pl.BlockSpec((D_MODEL, D_MODEL), lambda i: (0, 0))],
        out_specs=row(D_MODEL),
        out_shape=jax.ShapeDtypeStruct((m, D_MODEL), F32),
        compiler_params=_cparams(("parallel",)),
        name="merge",
    )(x, oa, ob, oc, od, *([gates] * N_BRANCH), wbr, wo)


def _norm_body(x_ref, w_ref, o_ref):
    o_ref[...] = _rms(x_ref[...], w_ref[...])


def _final_norm(x, w):
    m = x.shape[0]
    tm = min(m, 1024)
    return pl.pallas_call(
        _norm_body,
        grid=(m // tm,),
        in_specs=[pl.BlockSpec((tm, D_MODEL), lambda i: (i, 0)), pl.BlockSpec((1, D_MODEL), lambda i: (0, 0))],
        out_specs=pl.BlockSpec((tm, D_MODEL), lambda i: (i, 0)),
        out_shape=jax.ShapeDtypeStruct((m, D_MODEL), F32),
        compiler_params=_cparams(("parallel",)),
        name="final_norm",
    )(x, w)


def _spw_body(za_ref, zb_ref, gcs_ref, scs_ref, lcs_ref, lh_ref, gcw_ref, scw_ref, lcw_ref, lcb_ref, wa_ref,
              ba_ref, wx_ref, bx_ref, lam_ref, act_ref, gcs_o, oc_ref, scs_o, od_ref, lh_o, lcs_o):
    w = A_QKV_W
    x = za_ref[:, 0:w]
    y = gcw_ref[GDN_CONV - 1:GDN_CONV, :] * x
    for i in range(GDN_CONV - 1):
        y = y + gcw_ref[i:i + 1, :] * gcs_ref[:, i * w:(i + 1) * w]
    act_ref[...] = jax.nn.silu(y)
    gcs_o[:, 0:(GDN_CONV - 2) * w] = gcs_ref[:, w:(GDN_CONV - 1) * w]
    gcs_o[:, (GDN_CONV - 2) * w:(GDN_CONV - 1) * w] = x

    w = MIX_W
    gate_b = zb_ref[:, ZB_C:ZB_C + w]
    ci = zb_ref[:, ZB_C + w:ZB_C + 2 * w] * zb_ref[:, ZB_C + 2 * w:ZB_C + 3 * w]
    u = scw_ref[SC_CONV - 1:SC_CONV, :] * ci
    for i in range(SC_CONV - 1):
        u = u + scw_ref[i:i + 1, :] * scs_ref[:, i * w:(i + 1) * w]
    oc_ref[...] = gate_b * u
    scs_o[:, 0:(SC_CONV - 2) * w] = scs_ref[:, w:(SC_CONV - 1) * w]
    scs_o[:, (SC_CONV - 2) * w:(SC_CONV - 1) * w] = ci

    xd = zb_ref[:, ZB_D:ZB_D + w]
    gate_d = zb_ref[:, ZB_D + w:ZB_D + 2 * w]
    xc = lcw_ref[LRU_CONV - 1:LRU_CONV, :] * xd
    for i in range(LRU_CONV - 1):
        xc = xc + lcw_ref[i:i + 1, :] * lcs_ref[:, i * w:(i + 1) * w]
    xc = xc + lcb_ref[...]
    a, bt = _lru_gates(xc, wa_ref, ba_ref, wx_ref, bx_ref, lam_ref)
    hnew = a * lh_ref[...] + bt
    lh_o[...] = hnew
    od_ref[...] = hnew * jax.nn.gelu(gate_d)
    lcs_o[:, 0:(LRU_CONV - 2) * w] = lcs_ref[:, w:(LRU_CONV - 1) * w]
    lcs_o[:, (LRU_CONV - 2) * w:(LRU_CONV - 1) * w] = xd


def _sample_pointwise(za, zb, gcs, scs, lcs, lh, gcw, scw, lcw, lcb, wa, ba, wx, bx, lam):
    bd = za.shape[0]
    shapes = [(bd, A_QKV_W), gcs.shape, (bd, MIX_W), scs.shape, (bd, MIX_W), lh.shape, lcs.shape]
    return pl.pallas_call(
        _spw_body,
        out_shape=[jax.ShapeDtypeStruct(s, F32) for s in shapes],
        compiler_params=pltpu.CompilerParams(vmem_limit_bytes=VMEM_LIMIT),
        name="sample_pointwise",
    )(za, zb, gcs, scs, lcs, lh, gcw, scw, lcw, lcb, wa, ba, wx, bx, lam)


def _col(row, diag):
    return jnp.sum(jnp.where(diag, row, 0.0), axis=1, keepdims=True)


def _sstep_body(act_ref, za_ref, zb_ref, ba_ref, s_ref, alog_ref, dtb_ref, nw_ref,
                k0_ref, v0_ref, k1_ref, v1_ref, k2_ref, v2_ref, bb_ref, bn_ref,
                oa_ref, ob_ref, s_out_ref):
    e = GDN_DK
    ri = lax.broadcasted_iota(jnp.int32, (e, e), 0)
    ci = lax.broadcasted_iota(jnp.int32, (e, e), 1)
    diag = ri == ci

    ba = ba_ref[...]
    beta_all = jax.nn.sigmoid(ba)
    g_all = -jnp.exp(alog_ref[...]) * _softplus(ba + dtb_ref[...])
    for h in range(GDN_HEADS):
        q = act_ref[:, h * e:(h + 1) * e]
        k = act_ref[:, GDN_HEADS * e + h * e:GDN_HEADS * e + (h + 1) * e]
        v = act_ref[:, 2 * GDN_HEADS * e + h * GDN_DV:2 * GDN_HEADS * e + (h + 1) * GDN_DV]
        q = q * lax.rsqrt(jnp.sum(q * q, axis=-1, keepdims=True) + NORM_EPS) * (e ** -0.5)
        k = k * lax.rsqrt(jnp.sum(k * k, axis=-1, keepdims=True) + NORM_EPS)
        beta = beta_all[:, h:h + 1]
        eg = jnp.exp(g_all[:, GDN_HEADS + h:GDN_HEADS + h + 1])
        s = s_ref[h]
        kcol = _col(k, diag)
        qcol = _col(q, diag)
        ks = jnp.sum(kcol * s, axis=0, keepdims=True)
        qs = jnp.sum(qcol * s, axis=0, keepdims=True)
        v_new = beta * v - (beta * eg) * ks
        qk = jnp.sum(q * k, axis=-1, keepdims=True)
        o = eg * qs + qk * v_new
        s_out_ref[h] = s * eg + kcol * v_new
        hv = slice(h * GDN_DV, (h + 1) * GDN_DV)
        oa_ref[:, hv] = _rms(o, nw_ref[...]) * jax.nn.silu(za_ref[:, A_QKV_W + h * GDN_DV:A_QKV_W + (h + 1) * GDN_DV])

    kv = ((k0_ref, v0_ref), (k1_ref, v1_ref), (k2_ref, v2_ref))
    for h in range(DIL_GROUP_HEADS):
        hs = slice(h * e, (h + 1) * e)
        outs, lses = [], []
        for g in range(DIL_GROUPS):
            c0 = g * MIX_W + h * e
            q = zb_ref[:, ZB_Q + c0:ZB_Q + c0 + e]
            kn = zb_ref[:, ZB_K + c0:ZB_K + c0 + e]
            vn = zb_ref[:, ZB_V + c0:ZB_V + c0 + e]
            kb = kv[g][0][:, h, :]
            vb = kv[g][1][:, h, :]
            col = g * DIL_GROUP_HEADS + h
            sb = jnp.sum(kb * q, axis=-1, keepdims=True) * (e ** -0.5) + bb_ref[:, col:col + 1]
            sn = jnp.sum(kn * q, axis=-1, keepdims=True) * (e ** -0.5) + bn_ref[:, col:col + 1]
            m = jnp.maximum(jnp.max(sb, axis=0, keepdims=True), sn)
            pb = jnp.exp(sb - m)
            pn = jnp.exp(sn - m)
            l = jnp.sum(pb, axis=0, keepdims=True) + pn
            outs.append((jnp.sum(pb * vb, axis=0, keepdims=True) + pn * vn) / l)
            lses.append(m + jnp.log(l))
        m = functools.reduce(jnp.maximum, lses)
        es = [jnp.exp(l - m) for l in lses]
        den = functools.reduce(lambda p, q_: p + q_, es)
        ob_ref[:, hs] = functools.reduce(lambda p, q_: p + q_, [(ei / den) * oi for ei, oi in zip(es, outs)])


def _sample_step(act3, za3, zb3, ba3, s_in, alog_row, dtb_row, nw, caches, layer, bias_buf, bias_new):
    bd = act3.shape[0]

    def vec(w):
        return pl.BlockSpec((None, 1, w), lambda i: (i, 0, 0))

    def full(shape):
        return pl.BlockSpec(shape, lambda i: (0,) * len(shape))

    cache_specs, cache_args = [], []
    for cch in caches:
        for kvi in range(2):
            cache_specs.append(pl.BlockSpec((None, None, None, DIL_BLOCK, None, DIL_GROUP_HEADS, DIL_HEAD_DIM),
                                            lambda i, kvi=kvi: (layer, i, kvi, 0, 0, 0, 0)))
            cache_args.append(cch)
    st = pl.BlockSpec((None, GDN_HEADS, GDN_DK, GDN_DV), lambda i: (i, 0, 0, 0))
    return pl.pallas_call(
        _sstep_body,
        grid=(bd,),
        in_specs=[vec(A_QKV_W), vec(ZA_W), vec(ZB_W), vec(BA_W), st,
                  full((1, BA_W)), full((1, BA_W)), full((1, GDN_DV))] + cache_specs +
                 [full((DIL_BLOCK, DIL_HEADS)), full((1, DIL_HEADS))],
        out_specs=[vec(MIX_W), vec(MIX_W), st],
        out_shape=[jax.ShapeDtypeStruct((bd, 1, MIX_W), F32), jax.ShapeDtypeStruct((bd, 1, MIX_W), F32),
                   jax.ShapeDtypeStruct(s_in.shape, F32)],
        compiler_params=_cparams(("parallel",)),
        name="sample_step",
    )(act3, za3, zb3, ba3, s_in, alog_row, dtb_row, nw, *cache_args, bias_buf, bias_new)


def _shift_body(a_ref, b_ref, o_ref, *, rb):
    o_ref[:, pl.ds(0, rb - 1)] = a_ref[:, pl.ds(1, rb - 1)]
    o_ref[:, pl.ds(rb - 1, 1)] = b_ref[...]


def _shift_cache(cache):
    depth, bd, _, rows, nh, e = cache.shape
    rb = min(rows, 512)
    return pl.pallas_call(
        functools.partial(_shift_body, rb=rb),
        grid=(depth, bd, rows // rb),
        in_specs=[
            pl.BlockSpec((None, None, 2, rb, nh, e), lambda l, i, j: (l, i, 0, j, 0, 0)),
            pl.BlockSpec((None, None, 2, 1, nh, e), lambda l, i, j: (l, i, 0, jnp.minimum((j + 1) * rb, rows - 1), 0, 0)),
        ],
        out_specs=pl.BlockSpec((None, None, 2, rb, nh, e), lambda l, i, j: (l, i, 0, j, 0, 0)),
        out_shape=jax.ShapeDtypeStruct(cache.shape, F32),
        compiler_params=_cparams(("parallel", "parallel", "parallel")),
        name="shift_cache",
    )(cache, cache)


def _setrow_body(buf_ref, k_ref, v_ref, o_ref):
    del buf_ref
    e = DIL_HEAD_DIM
    for h in range(DIL_GROUP_HEADS):
        o_ref[0, 0, h:h + 1, :] = k_ref[:, h * e:(h + 1) * e]
        o_ref[1, 0, h:h + 1, :] = v_ref[:, h * e:(h + 1) * e]


def _set_last_row(buf, layer, zb3, gi):
    _, bd, _, rows, nh, e = buf.shape
    ck = (ZB_K + gi * MIX_W) // MIX_W
    cv = (ZB_V + gi * MIX_W) // MIX_W
    return pl.pallas_call(
        _setrow_body,
        grid=(bd,),
        in_specs=[pl.BlockSpec(memory_space=pl.ANY),
                  pl.BlockSpec((None, 1, MIX_W), lambda i: (i, 0, ck)),
                  pl.BlockSpec((None, 1, MIX_W), lambda i: (i, 0, cv))],
        out_specs=pl.BlockSpec((None, None, 2, 1, nh, e), lambda i: (layer, i, 0, rows - 1, 0, 0)),
        out_shape=jax.ShapeDtypeStruct(buf.shape, F32),
        input_output_aliases={0: 0},
        compiler_params=_cparams(("parallel",)),
        name="set_last_row",
    )(buf, zb3, zb3)


def _t5_bucket(dist):
    exact = REL_BUCKETS // 2
    d = np.maximum(dist, 1).astype(np.float32)
    large = exact + (np.log(d / exact) / math.log(REL_MAX_DIST / exact) * (REL_BUCKETS - exact)).astype(np.int32)
    return np.where(dist < exact, dist, np.minimum(large, REL_BUCKETS - 1)).astype(np.int32)


def _prompt_bias(rel_bias):
    blk = DIL_BLOCK
    period = 4 * blk
    out = []
    for gi, (window, dilation) in enumerate(DIL_PATTERNS):
        n_off = window // dilation
        tab = rel_bias[:, gi * DIL_GROUP_HEADS:(gi + 1) * DIL_GROUP_HEADS]
        vals = tab[_t5_bucket(np.arange(n_off + 1) * dilation)].T.astype(F32)
        w = jnp.full((DIL_GROUP_HEADS, period), -jnp.inf, F32)
        w = lax.dynamic_update_slice(w, vals, (0, blk - 1))
        r = jnp.tile(w, (1, blk + 1))[:, :blk * (period + 1)].reshape(DIL_GROUP_HEADS, blk, period + 1)
        out.append(r[:, :, 0:2 * blk][:, :, ::-1])
    return jnp.stack(out, axis=0)


def _step_bias(rel_bias):
    cols_buf, cols_new = [], []
    for gi, (window, dilation) in enumerate(DIL_PATTERNS):
        n_off = window // dilation
        tab = rel_bias[:, gi * DIL_GROUP_HEADS:(gi + 1) * DIL_GROUP_HEADS]
        j = n_off - np.arange(n_off)
        cols_buf.append(tab[_t5_bucket(j * dilation)])
        cols_new.append(tab[_t5_bucket(np.zeros((1,), np.int64))])
    return jnp.concatenate(cols_buf, axis=1).astype(F32), jnp.concatenate(cols_new, axis=1).astype(F32)


def _block_diag(w):
    n, c, _ = w.shape
    eye = jnp.eye(n, dtype=w.dtype)
    return (eye[:, None, :, None] * w[:, :, None, :]).reshape(n * c, n * c)


def _layer_weights(p, l):
    w_in = p['w_in'][l]
    b0, c0 = A_W, A_W + B_W
    zg0 = A_QKV_W
    ba0 = A_QKV_W + GDN_HEADS * GDN_DV
    w_all = jnp.concatenate([w_in[:, 0:ba0], w_in[:, b0:], p['w_gate'][l]], axis=1).astype(BF16)
    b_all = p['b_gate'][l][None]
    w_ba = jnp.pad(w_in[:, ba0:b0], ((0, 0), (0, BA_W - 2 * GDN_HEADS))).astype(BF16)
    pad_row = lambda v: jnp.pad(v, (GDN_HEADS, BA_W - 2 * GDN_HEADS))[None]
    row = lambda v: v[None].astype(F32)
    del zg0, c0
    return dict(
        n1=row(p['norm_ffn1'][l]), gu1=p['ffn1_w_gu'][l].astype(BF16), dn1=p['ffn1_w_down'][l].astype(BF16),
        nm=row(p['norm_mix'][l]), w_all=w_all, b_all=b_all, w_ba=w_ba,
        gcw=p['gdn_conv_w'][l], alog=pad_row(p['gdn_a_log'][l]), dtb=pad_row(p['gdn_dt_bias'][l]),
        gnw=row(p['gdn_norm_w'][l]), scw=p['sc_conv_w'][l], lcw=p['lru_conv_w'][l], lcb=row(p['lru_conv_b'][l]),
        wa=_block_diag(p['lru_wa'][l]).astype(BF16), ba=row(p['lru_ba'][l]),
        wx=_block_diag(p['lru_wx'][l]).astype(BF16), bx=row(p['lru_bx'][l]), lam=row(p['lru_lambda'][l]),
        wbr=p['w_branch'][l].astype(BF16), wo=p['w_o'][l].astype(BF16),
        n2=row(p['norm_ffn2'][l]), gu2=p['ffn2_w_gu'][l].astype(BF16), dn2=p['ffn2_w_down'][l].astype(BF16),
    )


def _prompt_layer(x, w, bias, b, t):
    m = b * t
    x = _ffn(x, w['n1'], w['gu1'], w['dn1'])
    za, zb, gates = _proj(x, w['nm'], w['w_all'], w['b_all'])
    ba = _proj_ba(x, w['nm'], w['w_ba'])
    za3 = za.reshape(b, t, ZA_W)
    zb3 = zb.reshape(b, t, ZB_W)
    o_a, s_gdn = _gdn_prompt(za3, ba.reshape(b, t, BA_W), w['gcw'], w['alog'], w['dtb'], w['gnw'])
    o_b = _dil_prompt(zb3, bias)
    o_c, o_d, sc_st, lru_h = _cd_prompt(zb3, w['scw'], w['lcw'], w['lcb'], w['wa'], w['ba'], w['wx'], w['bx'], w['lam'])
    x = _merge(x, o_a.reshape(m, MIX_W), o_b.reshape(m, MIX_W), o_c.reshape(m, MIX_W), o_d.reshape(m, MIX_W),
               gates, w['wbr'], w['wo'])
    x = _ffn(x, w['n2'], w['gu2'], w['dn2'])
    gdn_conv = za3[:, t - (GDN_CONV - 1):, 0:A_QKV_W]
    bufs = []
    for gi, (window, _) in enumerate(DIL_PATTERNS):
        rows = min(window, t)
        k = zb3[:, t - rows:, ZB_K + gi * MIX_W:ZB_K + (gi + 1) * MIX_W]
        v = zb3[:, t - rows:, ZB_V + gi * MIX_W:ZB_V + (gi + 1) * MIX_W]
        bufs.append(jnp.stack([k, v], axis=1).reshape(b, 2, rows, DIL_GROUP_HEADS, DIL_HEAD_DIM))
    sc_conv = sc_st[:, SUBLANES - (SC_CONV - 1):]
    lru_conv = zb3[:, t - (LRU_CONV - 1):, ZB_D:ZB_D + MIX_W]
    return x, (s_gdn, gdn_conv, bufs[0], bufs[1], bufs[2], sc_conv, lru_h[:, 0], lru_conv)


def _sample_layer(x, w, l, st, shifted, step_bias):
    bd = x.shape[0]
    s_gdn, gdn_conv, caches, sc_conv, lru_h, lru_conv = st
    x = _ffn(x, w['n1'], w['gu1'], w['dn1'])
    za, zb, gates = _proj(x, w['nm'], w['w_all'], w['b_all'])
    ba = _proj_ba(x, w['nm'], w['w_ba'])
    act, gdn_conv_new, o_c, sc_new, o_d, lru_h_new, lru_conv_new = _sample_pointwise(
        za, zb, gdn_conv.reshape(bd, -1), sc_conv.reshape(bd, -1), lru_conv.reshape(bd, -1), lru_h,
        w['gcw'], w['scw'], w['lcw'], w['lcb'], w['wa'], w['ba'], w['wx'], w['bx'], w['lam'])
    zb3 = zb.reshape(bd, 1, ZB_W)
    o_a, o_b, s_new = _sample_step(act.reshape(bd, 1, A_QKV_W), za.reshape(bd, 1, ZA_W), zb3,
                                   ba.reshape(bd, 1, BA_W), s_gdn, w['alog'], w['dtb'], w['gnw'], caches, l,
                                   *step_bias)
    shifted = [_set_last_row(buf, l, zb3, gi) for gi, buf in enumerate(shifted)]
    x = _merge(x, o_a.reshape(bd, MIX_W), o_b.reshape(bd, MIX_W), o_c, o_d, gates, w['wbr'], w['wo'])
    x = _ffn(x, w['n2'], w['gu2'], w['dn2'])
    new = (s_new, gdn_conv_new.reshape(gdn_conv.shape), sc_new.reshape(sc_conv.shape), lru_h_new,
           lru_conv_new.reshape(lru_conv.shape))
    return x, new, shifted


def kernel(x_prompt, x_sample, state_gdn, state_gdn_conv, cache_dil_w128, cache_dil_w512, cache_dil_w2048,
           state_sc_conv, state_lru, state_lru_conv, norm_ffn1, ffn1_w_gu, ffn1_w_down, norm_mix, w_in,
           gdn_conv_w, gdn_a_log, gdn_dt_bias, gdn_norm_w, rel_bias, sc_conv_w, lru_conv_w, lru_conv_b,
           lru_wa, lru_ba, lru_wx, lru_bx, lru_lambda, w_gate, b_gate, w_branch, w_o, norm_ffn2, ffn2_w_gu,
           ffn2_w_down, final_norm):
    p = dict(norm_ffn1=norm_ffn1, ffn1_w_gu=ffn1_w_gu, ffn1_w_down=ffn1_w_down, norm_mix=norm_mix, w_in=w_in,
             gdn_conv_w=gdn_conv_w, gdn_a_log=gdn_a_log, gdn_dt_bias=gdn_dt_bias, gdn_norm_w=gdn_norm_w,
             sc_conv_w=sc_conv_w, lru_conv_w=lru_conv_w, lru_conv_b=lru_conv_b, lru_wa=lru_wa, lru_ba=lru_ba,
             lru_wx=lru_wx, lru_bx=lru_bx, lru_lambda=lru_lambda, w_gate=w_gate, b_gate=b_gate,
             w_branch=w_branch, w_o=w_o, norm_ffn2=norm_ffn2, ffn2_w_gu=ffn2_w_gu, ffn2_w_down=ffn2_w_down)
    weights = [_layer_weights(p, l) for l in range(DEPTH)]
    fnw = final_norm[None]
    b, t, _ = x_prompt.shape
    bd = x_sample.shape[0]
    bias = _prompt_bias(rel_bias)
    step_bias = _step_bias(rel_bias)

    x = x_prompt.reshape(b * t, D_MODEL)
    p_states = []
    for l in range(DEPTH):
        x, new = _prompt_layer(x, weights[l], bias, b, t)
        p_states.append(new)
    y_prompt = _final_norm(x, fnw).reshape(b, t, D_MODEL)
    p_out = tuple(jnp.stack(zs, axis=0) for zs in zip(*p_states))

    caches = (cache_dil_w128, cache_dil_w512, cache_dil_w2048)
    shifted = [_shift_cache(c) for c in caches]
    strided = tuple(c.reshape(DEPTH, bd, 2, c.shape[3] // d, d, DIL_GROUP_HEADS, DIL_HEAD_DIM)
                    for c, (_, d) in zip(caches, DIL_PATTERNS))
    x = x_sample.reshape(bd, D_MODEL)
    s_states = []
    for l in range(DEPTH):
        st = (state_gdn[l], state_gdn_conv[l], strided, state_sc_conv[l], state_lru[l], state_lru_conv[l])
        x, new, shifted = _sample_layer(x, weights[l], l, st, shifted, step_bias)
        s_states.append(new)
    y_sample = _final_norm(x, fnw).reshape(bd, 1, D_MODEL)
    s_gdn, s_gdn_conv, s_sc_conv, s_lru, s_lru_conv = (jnp.stack(zs, axis=0) for zs in zip(*s_states))

    p_gdn, p_gdn_conv, p_w128, p_w512, p_w2048, p_sc_conv, p_lru, p_lru_conv = p_out
    return (y_prompt, y_sample, p_gdn, s_gdn, p_gdn_conv, s_gdn_conv, p_w128, shifted[0], p_w512, shifted[1],
            p_w2048, shifted[2], p_sc_conv, s_sc_conv, p_lru, s_lru, p_lru_conv, s_lru_conv)
```

```python
import functools
import math

import numpy as np
import jax
import jax.numpy as jnp
from jax import lax
from jax.experimental import pallas as pl
from jax.experimental.pallas import tpu as pltpu

D_MODEL = 1024
DEPTH = 4
MIX_W = D_MODEL // 2
D_FF = 2816
NORM_EPS = 1e-6
N_BRANCH = 4
GDN_HEADS = 4
GDN_DK = 128
GDN_DV = MIX_W // GDN_HEADS
GDN_CONV = 4
GDN_CHUNK = 64
DIL_PATTERNS = ((128, 1), (512, 4), (2048, 16))
DIL_GROUPS = len(DIL_PATTERNS)
DIL_GROUP_HEADS = 4
DIL_HEAD_DIM = MIX_W // DIL_GROUP_HEADS
DIL_HEADS = DIL_GROUPS * DIL_GROUP_HEADS
DIL_BLOCK = 128
REL_BUCKETS = 32
REL_MAX_DIST = 2048
SC_CONV = 3
LRU_BLOCKS = 8
LRU_BLOCK_W = MIX_W // LRU_BLOCKS
LRU_CONV = 4
LRU_C = 8.0
A_QKV_W = GDN_HEADS * (2 * GDN_DK + GDN_DV)
A_W = A_QKV_W + GDN_HEADS * GDN_DV + 2 * GDN_HEADS
B_W = 3 * DIL_HEADS * DIL_HEAD_DIM
C_W = 3 * MIX_W
D_W = 2 * MIX_W

LANES = 128
SUBLANES = 8
VMEM_LIMIT = 56 * 1024 * 1024

ZA_W = A_QKV_W + MIX_W
ZB_Q, ZB_K, ZB_V = 0, B_W // 3, 2 * B_W // 3
ZB_C = B_W
ZB_D = B_W + C_W
ZB_W = B_W + C_W + D_W
G_W = N_BRANCH * D_MODEL
PROJ_TN = 1024
BA_W = LANES
DIL_TB = DIL_BLOCK * max(d for _, d in DIL_PATTERNS)
DIL_UNROLL = 8

F32 = jnp.float32
BF16 = jnp.bfloat16
NT_DIMS = (((1,), (1,)), ((), ()))
TN_DIMS = (((0,), (0,)), ((), ()))


def _cparams(sem):
    return pltpu.CompilerParams(dimension_semantics=sem, vmem_limit_bytes=VMEM_LIMIT)


def _rms(x, w):
    return x * lax.rsqrt(jnp.mean(x * x, axis=-1, keepdims=True) + NORM_EPS) * w


def _softplus(x):
    return jnp.maximum(x, 0.0) + jnp.log1p(jnp.exp(-jnp.abs(x)))


def _dot(a, b):
    return jnp.dot(a.astype(BF16), b.astype(BF16), preferred_element_type=F32)


def _dot_nt(a, b):
    return lax.dot_general(a.astype(BF16), b.astype(BF16), NT_DIMS, preferred_element_type=F32)


def _dot_tn(a, b):
    return lax.dot_general(a.astype(BF16), b.astype(BF16), TN_DIMS, preferred_element_type=F32)


def _ffn_body(x_ref, nw_ref, wg_ref, wu_ref, wd_ref, o_ref, h_ref, acc_ref):
    j = pl.program_id(1)

    @pl.when(j == 0)
    def _():
        h_ref[...] = _rms(x_ref[...], nw_ref[...]).astype(BF16)
        acc_ref[...] = jnp.zeros_like(acc_ref)

    h = h_ref[...]
    g = _dot(h, wg_ref[...])
    u = _dot(h, wu_ref[...])
    acc_ref[...] += _dot(jax.nn.silu(g) * u, wd_ref[...])

    @pl.when(j == pl.num_programs(1) - 1)
    def _():
        o_ref[...] = x_ref[...] + 0.5 * acc_ref[...]


def _ffn(x, nw, w_gu, w_down):
    m = x.shape[0]
    tm = min(m, 512)
    nf = 2
    tf = D_FF // nf
    return pl.pallas_call(
        _ffn_body,
        grid=(m // tm, nf),
        in_specs=[
            pl.BlockSpec((tm, D_MODEL), lambda i, j: (i, 0)),
            pl.BlockSpec((1, D_MODEL), lambda i, j: (0, 0)),
            pl.BlockSpec((D_MODEL, tf), lambda i, j: (0, j)),
            pl.BlockSpec((D_MODEL, tf), lambda i, j: (0, nf + j)),
            pl.BlockSpec((tf, D_MODEL), lambda i, j: (j, 0)),
        ],
        out_specs=pl.BlockSpec((tm, D_MODEL), lambda i, j: (i, 0)),
        out_shape=jax.ShapeDtypeStruct((m, D_MODEL), F32),
        scratch_shapes=[pltpu.VMEM((tm, D_MODEL), BF16), pltpu.VMEM((tm, D_MODEL), F32)],
        compiler_params=_cparams(("parallel", "arbitrary")),
        name="ffn",
    )(x, nw, w_gu, w_gu, w_down)


def _proj_body(x_ref, nw_ref, w_ref, b_ref, za_ref, zb_ref, g_ref, h_ref, *, ja, jb):
    j = pl.program_id(1)

    @pl.when(j == 0)
    def _():
        h_ref[...] = _rms(x_ref[...], nw_ref[...]).astype(BF16)

    @pl.when(j < ja)
    def _():
        za_ref[...] = _dot(h_ref[...], w_ref[...])

    @pl.when((j >= ja) & (j < jb))
    def _():
        zb_ref[...] = _dot(h_ref[...], w_ref[...])

    @pl.when(j >= jb)
    def _():
        g_ref[...] = jax.nn.sigmoid(_dot(h_ref[...], w_ref[...]) + b_ref[...]).astype(BF16)


def _proj(x, nw, w, b):
    m = x.shape[0]
    tm = min(m, 1024)
    tn = PROJ_TN
    ja, jb = ZA_W // tn, (ZA_W + ZB_W) // tn
    nj = (ZA_W + ZB_W + G_W) // tn
    return pl.pallas_call(
        functools.partial(_proj_body, ja=ja, jb=jb),
        grid=(m // tm, nj),
        in_specs=[
            pl.BlockSpec((tm, D_MODEL), lambda i, j: (i, 0)),
            pl.BlockSpec((1, D_MODEL), lambda i, j: (0, 0)),
            pl.BlockSpec((D_MODEL, tn), lambda i, j: (0, j)),
            pl.BlockSpec((1, tn), lambda i, j: (0, jnp.maximum(j - jb, 0))),
        ],
        out_specs=[
            pl.BlockSpec((tm, tn), lambda i, j: (i, jnp.minimum(j, ja - 1))),
            pl.BlockSpec((tm, tn), lambda i, j: (i, jnp.clip(j - ja, 0, jb - ja - 1))),
            pl.BlockSpec((tm, tn), lambda i, j: (i, jnp.maximum(j - jb, 0))),
        ],
        out_shape=[
            jax.ShapeDtypeStruct((m, ZA_W), F32),
            jax.ShapeDtypeStruct((m, ZB_W), F32),
            jax.ShapeDtypeStruct((m, G_W), BF16),
        ],
        scratch_shapes=[pltpu.VMEM((tm, D_MODEL), BF16)],
        compiler_params=_cparams(("parallel", "arbitrary")),
        name="proj",
    )(x, nw, w, b)


def _proj_ba_body(x_ref, nw_ref, w_ref, o_ref):
    o_ref[...] = _dot(_rms(x_ref[...], nw_ref[...]), w_ref[...])


def _proj_ba(x, nw, w):
    m = x.shape[0]
    tm = min(m, 1024)
    return pl.pallas_call(
        _proj_ba_body,
        grid=(m // tm,),
        in_specs=[
            pl.BlockSpec((tm, D_MODEL), lambda i: (i, 0)),
            pl.BlockSpec((1, D_MODEL), lambda i: (0, 0)),
            pl.BlockSpec((D_MODEL, BA_W), lambda i: (0, 0)),
        ],
        out_specs=pl.BlockSpec((tm, BA_W), lambda i: (i, 0)),
        out_shape=jax.ShapeDtypeStruct((m, BA_W), F32),
        compiler_params=_cparams(("parallel",)),
        name="proj_ba",
    )(x, nw, w)


def _gdn_body(qkv_ref, zg_ref, ba_ref, cw_ref, alog_ref, dtb_ref, nw_ref, o_ref, s_out_ref,
              xbuf, act, beta_s, gc_s, u_s, wq_s, at_s, kd_s, s_s, *, tt, nb):
    t = pl.program_id(0)
    c = GDN_CHUNK
    nchunk = tt // c

    @pl.when(t == 0)
    def _():
        for b in range(nb):
            xbuf[b, pl.ds(0, SUBLANES), :] = jnp.zeros((SUBLANES, A_QKV_W), F32)
        s_s[...] = jnp.zeros_like(s_s)

    rowmod = lax.broadcasted_iota(jnp.int32, (tt, BA_W), 0) & (c - 1)
    for b in range(nb):
        xbuf[b, pl.ds(SUBLANES, tt), :] = qkv_ref[b]
        y = cw_ref[0:1, :] * xbuf[b, pl.ds(SUBLANES - 3, tt), :]
        for i in range(1, GDN_CONV):
            y = y + cw_ref[i:i + 1, :] * xbuf[b, pl.ds(SUBLANES - 3 + i, tt), :]
        act[b] = jax.nn.silu(y)
        xbuf[b, pl.ds(0, SUBLANES), :] = xbuf[b, pl.ds(tt, SUBLANES), :]

        ba = ba_ref[b]
        beta_s[b] = jax.nn.sigmoid(ba)
        g = -jnp.exp(alog_ref[...]) * _softplus(ba + dtb_ref[...])
        sh = 1
        while sh < c:
            g = g + jnp.where(rowmod >= sh, pltpu.roll(g, sh, 0), 0.0)
            sh *= 2
        gc_s[b] = g

    ri = lax.broadcasted_iota(jnp.int32, (c, c), 0)
    ci = lax.broadcasted_iota(jnp.int32, (c, c), 1)
    causal = ri >= ci
    strict = ri > ci
    diag = ri == ci
    eye = jnp.where(diag, 1.0, 0.0).astype(F32)

    heads = range(GDN_HEADS)

    def prep(i, carry):
        chains = [(b, 2 * i + j, h) for b in range(nb) for j in range(2) for h in heads]
        n = len(chains)
        rows_l, gcol_l, decay_l, q_l, k_l, kb_l, vb_l = [], [], [], [], [], [], []
        for b, ch, h in chains:
            rows = pl.ds(pl.multiple_of(ch * c, c), c)
            gcol = gc_s[b, rows, GDN_HEADS + h:GDN_HEADS + h + 1]
            grow = jnp.sum(jnp.where(diag, gcol, 0.0), axis=0, keepdims=True)
            decay = jnp.exp(jnp.where(causal, gcol - grow, -jnp.inf))
            q = act[b, rows, h * GDN_DK:(h + 1) * GDN_DK]
            k = act[b, rows, GDN_HEADS * GDN_DK + h * GDN_DK:GDN_HEADS * GDN_DK + (h + 1) * GDN_DK]
            v = act[b, rows, 2 * GDN_HEADS * GDN_DK + h * GDN_DV:2 * GDN_HEADS * GDN_DK + (h + 1) * GDN_DV]
            q = q * lax.rsqrt(jnp.sum(q * q, axis=-1, keepdims=True) + NORM_EPS) * (GDN_DK ** -0.5)
            k = k * lax.rsqrt(jnp.sum(k * k, axis=-1, keepdims=True) + NORM_EPS)
            beta = beta_s[b, rows, h:h + 1]
            rows_l.append(rows)
            gcol_l.append(gcol)
            decay_l.append(decay)
            q_l.append(q)
            k_l.append(k)
            kb_l.append(k * beta)
            vb_l.append(v * beta)
        kq_l = [_dot_nt(jnp.concatenate([kb_l[j], q_l[j]], axis=0), k_l[j]) for j in range(n)]
        for j, (b, ch, h) in enumerate(chains):
            at_s[b, rows_l[j], h * LANES:h * LANES + c] = kq_l[j][c:] * decay_l[j]
        qq = [-jnp.where(strict, kq_l[j][:c] * decay_l[j], 0.0) for j in range(n)]
        yy = [eye + qq[j] for j in range(n)]
        qq = [_dot(qq[j], qq[j]) for j in range(n)]
        for _ in range(int(math.log2(c)) - 2):
            yq = [_dot(jnp.concatenate([yy[j], qq[j]], axis=0), qq[j]) for j in range(n)]
            yy = [yy[j] + yq[j][:c] for j in range(n)]
            qq = [yq[j][c:] for j in range(n)]
        yq = [_dot(yy[j], qq[j]) for j in range(n)]
        tinv = [yy[j] + yq[j] for j in range(n)]
        egc = [jnp.exp(gcol_l[j]) for j in range(n)]
        sol = [_dot(tinv[j], jnp.concatenate([vb_l[j], kb_l[j] * egc[j]], axis=1)) for j in range(n)]
        for j, (b, ch, h) in enumerate(chains):
            hs = slice(h * GDN_DK, (h + 1) * GDN_DK)
            u_s[b, rows_l[j], h * GDN_DV:(h + 1) * GDN_DV] = sol[j][:, :GDN_DV]
            wq_s[b, pl.ds(pl.multiple_of(ch * 2 * c, 2 * c), c), hs] = sol[j][:, GDN_DV:]
            wq_s[b, pl.ds(pl.multiple_of(ch * 2 * c + c, c), c), hs] = q_l[j] * egc[j]
            kd_s[b, rows_l[j], hs] = k_l[j] * jnp.exp(gcol_l[j][c - 1:c, :] - gcol_l[j])
        return carry

    lax.fori_loop(0, nchunk // 2, prep, 0)

    def step(ch, carry):
        rows = pl.ds(pl.multiple_of(ch * c, c), c)
        wq_rows = pl.ds(pl.multiple_of(ch * 2 * c, 2 * c), 2 * c)
        chains = [(b, h) for b in range(nb) for h in heads]
        hs = {h: slice(h * GDN_DK, (h + 1) * GDN_DK) for h in heads}
        hv = {h: slice(h * GDN_DV, (h + 1) * GDN_DV) for h in heads}
        glast = [jnp.exp(gc_s[b, pl.ds(ch * c + c - 1, 1), :]) for b in range(nb)]
        st = [s_s[b, h] for b, h in chains]
        ws = [_dot(wq_s[b, wq_rows, hs[h]], st[j]) for j, (b, h) in enumerate(chains)]
        v_new = [u_s[b, rows, hv[h]] - ws[j][:c] for j, (b, h) in enumerate(chains)]
        av = [_dot(at_s[b, rows, h * LANES:h * LANES + c], v_new[j]) for j, (b, h) in enumerate(chains)]
        kv = [_dot_tn(kd_s[b, rows, hs[h]], v_new[j]) for j, (b, h) in enumerate(chains)]
        for j, (b, h) in enumerate(chains):
            s_s[b, h] = st[j] * glast[b][:, GDN_HEADS + h:GDN_HEADS + h + 1] + kv[j]
            o = _rms(ws[j][c:] + av[j], nw_ref[...]) * jax.nn.silu(zg_ref[b, rows, hv[h]])
            o_ref[b, rows, hv[h]] = o
        return carry

    lax.fori_loop(0, nchunk, step, 0)

    @pl.when(t == pl.num_programs(0) - 1)
    def _():
        s_out_ref[...] = s_s[...]


def _gdn_prompt(za3, ba3, cw, alog_row, dtb_row, nw):
    b, t, _ = za3.shape
    tt = min(t, 256)
    return pl.pallas_call(
        functools.partial(_gdn_body, tt=tt, nb=b),
        grid=(t // tt,),
        in_specs=[
            pl.BlockSpec((b, tt, A_QKV_W), lambda j: (0, j, 0)),
            pl.BlockSpec((b, tt, MIX_W), lambda j: (0, j, A_QKV_W // MIX_W)),
            pl.BlockSpec((b, tt, BA_W), lambda j: (0, j, 0)),
            pl.BlockSpec((GDN_CONV, A_QKV_W), lambda j: (0, 0)),
            pl.BlockSpec((1, BA_W), lambda j: (0, 0)),
            pl.BlockSpec((1, BA_W), lambda j: (0, 0)),
            pl.BlockSpec((1, GDN_DV), lambda j: (0, 0)),
        ],
        out_specs=[
            pl.BlockSpec((b, tt, MIX_W), lambda j: (0, j, 0)),
            pl.BlockSpec((b, GDN_HEADS, GDN_DK, GDN_DV), lambda j: (0, 0, 0, 0)),
        ],
        out_shape=[
            jax.ShapeDtypeStruct((b, t, MIX_W), F32),
            jax.ShapeDtypeStruct((b, GDN_HEADS, GDN_DK, GDN_DV), F32),
        ],
        scratch_shapes=[
            pltpu.VMEM((b, tt + SUBLANES, A_QKV_W), F32),
            pltpu.VMEM((b, tt, A_QKV_W), F32),
            pltpu.VMEM((b, tt, BA_W), F32),
            pltpu.VMEM((b, tt, BA_W), F32),
            pltpu.VMEM((b, tt, MIX_W), F32),
            pltpu.VMEM((b, 2 * tt, MIX_W), F32),
            pltpu.VMEM((b, tt, GDN_HEADS * LANES), F32),
            pltpu.VMEM((b, tt, MIX_W), F32),
            pltpu.VMEM((b, GDN_HEADS, GDN_DK, GDN_DV), F32),
        ],
        compiler_params=_cparams(("arbitrary",)),
        name="gdn_prompt",
    )(za3, za3, ba3, cw, alog_row, dtb_row, nw)


def _dil_body(*refs):
    qkv = refs[:3 * DIL_GROUPS]
    bias_ref, o_ref = refs[3 * DIL_GROUPS], refs[3 * DIL_GROUPS + 1]
    kvbuf = refs[3 * DIL_GROUPS + 2:3 * DIL_GROUPS + 2 + 2 * DIL_GROUPS]
    og, lg = refs[-2], refs[-1]
    n = pl.program_id(2)
    e = DIL_HEAD_DIM
    blk = DIL_BLOCK
    tb = DIL_TB
    for g, (_, d) in enumerate(DIL_PATTERNS):
        q_ref, k_ref, v_ref = qkv[3 * g:3 * g + 3]
        kb, vb = kvbuf[2 * g], kvbuf[2 * g + 1]
        tail = blk * d

        @pl.when(n == 0)
        def _(kb=kb, vb=vb, tail=tail):
            kb[pl.ds(0, tail), :] = jnp.zeros((tail, e), F32)
            vb[pl.ds(0, tail), :] = jnp.zeros((tail, e), F32)

        kb[pl.ds(tail, tb), :] = k_ref[...]
        vb[pl.ds(tail, tb), :] = v_ref[...]
        bp = bias_ref[g, :, 0:blk]
        bc = bias_ref[g, :, blk:2 * blk]
        shift = int(math.log2(d))

        def rows_at(base, d=d):
            return pl.ds(base, blk) if d == 1 else pl.ds(base, blk, stride=d)

        def body(it, carry, d=d, g=g, q_ref=q_ref, kb=kb, vb=vb, bp=bp, bc=bc, shift=shift, tail=tail,
                 rows_at=rows_at):
            us = range(DIL_UNROLL)
            idx = [it * DIL_UNROLL + u for u in us]
            sub = [i >> shift for i in idx]
            base = [sub[u] * tail + (idx[u] & (d - 1)) for u in us]
            q = [q_ref[rows_at(base[u]), :].astype(BF16) for u in us]
            sp = [_dot_nt(q[u], kb[rows_at(base[u]), :]) for u in us]
            sc = [_dot_nt(q[u], kb[rows_at(base[u] + tail), :]) for u in us]
            pp, pc, l, lse = [], [], [], []
            for u in us:
                spu = jnp.where((n > 0) | (sub[u] > 0), sp[u] * (e ** -0.5) + bp, -jnp.inf)
                scu = sc[u] * (e ** -0.5) + bc
                m = jnp.maximum(jnp.max(spu, axis=-1, keepdims=True), jnp.max(scu, axis=-1, keepdims=True))
                ppu = jnp.exp(spu - m)
                pcu = jnp.exp(scu - m)
                lu = jnp.sum(ppu, axis=-1, keepdims=True) + jnp.sum(pcu, axis=-1, keepdims=True)
                pp.append(ppu)
                pc.append(pcu)
                l.append(lu)
                lse.append(m + jnp.log(lu))
            op = [_dot(pp[u], vb[rows_at(base[u]), :]) for u in us]
            oc = [_dot(pc[u], vb[rows_at(base[u] + tail), :]) for u in us]
            for u in us:
                og[g, rows_at(base[u]), :] = (op[u] + oc[u]) / l[u]
                lg[g, rows_at(base[u]), :] = jnp.broadcast_to(lse[u], (blk, e))
            return carry

        lax.fori_loop(0, tb // blk // DIL_UNROLL, body, 0)
        kb[pl.ds(0, tail), :] = kb[pl.ds(tb, tail), :]
        vb[pl.ds(0, tail), :] = vb[pl.ds(tb, tail), :]

    lses = [lg[g] for g in range(DIL_GROUPS)]
    m = functools.reduce(jnp.maximum, lses)
    es = [jnp.exp(l - m) for l in lses]
    den = functools.reduce(lambda p, q_: p + q_, es)
    o_ref[...] = functools.reduce(lambda p, q_: p + q_, [(es[g] / den) * og[g] for g in range(DIL_GROUPS)])


def _dil_prompt(zb3, bias):
    b, t, _ = zb3.shape
    tb = DIL_TB
    e = DIL_HEAD_DIM
    specs, args = [], []
    for g in range(DIL_GROUPS):
        for off in (ZB_Q, ZB_K, ZB_V):
            cb = (off + g * MIX_W) // e
            specs.append(pl.BlockSpec((None, tb, e), lambda i, h, n, cb=cb: (i, n, cb + h)))
            args.append(zb3)
    specs.append(pl.BlockSpec((DIL_GROUPS, None, DIL_BLOCK, 2 * DIL_BLOCK), lambda i, h, n: (0, h, 0, 0)))
    scratch = []
    for _, d in DIL_PATTERNS:
        scratch += [pltpu.VMEM((DIL_BLOCK * d + tb, e), F32)] * 2
    scratch += [pltpu.VMEM((DIL_GROUPS, tb, e), F32)] * 2
    return pl.pallas_call(
        _dil_body,
        grid=(b, DIL_GROUP_HEADS, t // tb),
        in_specs=specs,
        out_specs=pl.BlockSpec((None, tb, e), lambda i, h, n: (i, n, h)),
        out_shape=jax.ShapeDtypeStruct((b, t, MIX_W), F32),
        scratch_shapes=scratch,
        compiler_params=_cparams(("parallel", "parallel", "arbitrary")),
        name="dil_prompt",
    )(*args, bias)


def _lru_gates(xc, wa_ref, ba_ref, wx_ref, bx_ref, lam_ref):
    xb = xc.astype(BF16)
    r = jax.nn.sigmoid(_dot(xb, wa_ref[...]) + ba_ref[...])
    i = jax.nn.sigmoid(_dot(xb, wx_ref[...]) + bx_ref[...])
    log_a = -LRU_C * r * _softplus(-lam_ref[...])
    a = jnp.exp(log_a)
    bt = jnp.sqrt(-jnp.tanh(log_a) * (jnp.exp(2.0 * log_a) + 1.0)) * i * xc
    return a, bt


def _cd_body(gb_ref, gc_ref, xi_ref, xd_ref, gd_ref, scw_ref, lcw_ref, lcb_ref, wa_ref, ba_ref, wx_ref,
             bx_ref, lam_ref, oc_ref, od_ref, scst_ref, lruh_ref, cbuf, dbuf, a_s, b_s, h_s, *, tt):
    t = pl.program_id(1)

    @pl.when(t == 0)
    def _():
        cbuf[pl.ds(0, SUBLANES), :] = jnp.zeros((SUBLANES, MIX_W), F32)
        dbuf[pl.ds(0, SUBLANES), :] = jnp.zeros((SUBLANES, MIX_W), F32)
        h_s[...] = jnp.zeros_like(h_s)

    cbuf[pl.ds(SUBLANES, tt), :] = gc_ref[...] * xi_ref[...]
    u = scw_ref[0:1, :] * cbuf[pl.ds(SUBLANES - 2, tt), :]
    for i in range(1, SC_CONV):
        u = u + scw_ref[i:i + 1, :] * cbuf[pl.ds(SUBLANES - 2 + i, tt), :]
    oc_ref[...] = gb_ref[...] * u
    tail = cbuf[pl.ds(tt, SUBLANES), :]
    scst_ref[...] = tail
    cbuf[pl.ds(0, SUBLANES), :] = tail

    dbuf[pl.ds(SUBLANES, tt), :] = xd_ref[...]
    xc = lcw_ref[0:1, :] * dbuf[pl.ds(SUBLANES - 3, tt), :]
    for i in range(1, LRU_CONV):
        xc = xc + lcw_ref[i:i + 1, :] * dbuf[pl.ds(SUBLANES - 3 + i, tt), :]
    xc = xc + lcb_ref[...]
    dbuf[pl.ds(0, SUBLANES), :] = dbuf[pl.ds(tt, SUBLANES), :]
    a, bt = _lru_gates(xc, wa_ref, ba_ref, wx_ref, bx_ref, lam_ref)
    a_s[...] = a
    b_s[...] = bt

    def scan(g, h):
        rows = pl.ds(pl.multiple_of(g * SUBLANES, SUBLANES), SUBLANES)
        a8 = a_s[rows, :]
        b8 = b_s[rows, :]
        out = []
        for r in range(SUBLANES):
            h = a8[r:r + 1, :] * h + b8[r:r + 1, :]
            out.append(h)
        a_s[rows, :] = jnp.concatenate(out, axis=0)
        return h

    h_last = lax.fori_loop(0, tt // SUBLANES, scan, h_s[0:1, :])
    h_s[...] = jnp.broadcast_to(h_last, h_s.shape)
    lruh_ref[...] = jnp.broadcast_to(h_last, h_s.shape)
    od_ref[...] = a_s[...] * jax.nn.gelu(gd_ref[...])


def _cd_prompt(zb3, scw, lcw, lcb, wa, ba, wx, bx, lam):
    b, t, _ = zb3.shape
    tt = min(t, 512)
    blk = (None, tt, MIX_W)
    c0 = ZB_C // MIX_W

    def zspec(c):
        return pl.BlockSpec(blk, lambda i, j: (i, j, c))

    def full(shape):
        return pl.BlockSpec(shape, lambda i, j: (0,) * len(shape))

    st = pl.BlockSpec((None, SUBLANES, MIX_W), lambda i, j: (i, 0, 0))
    return pl.pallas_call(
        functools.partial(_cd_body, tt=tt),
        grid=(b, t // tt),
        in_specs=[zspec(c0), zspec(c0 + 1), zspec(c0 + 2), zspec(c0 + 3), zspec(c0 + 4),
                  full((SC_CONV, MIX_W)), full((LRU_CONV, MIX_W)), full((1, MIX_W)),
                  full((MIX_W, MIX_W)), full((1, MIX_W)), full((MIX_W, MIX_W)), full((1, MIX_W)),
                  full((1, MIX_W))],
        out_specs=[pl.BlockSpec(blk, lambda i, j: (i, j, 0)), pl.BlockSpec(blk, lambda i, j: (i, j, 0)), st, st],
        out_shape=[
            jax.ShapeDtypeStruct((b, t, MIX_W), F32),
            jax.ShapeDtypeStruct((b, t, MIX_W), F32),
            jax.ShapeDtypeStruct((b, SUBLANES, MIX_W), F32),
            jax.ShapeDtypeStruct((b, SUBLANES, MIX_W), F32),
        ],
        scratch_shapes=[
            pltpu.VMEM((tt + SUBLANES, MIX_W), F32),
            pltpu.VMEM((tt + SUBLANES, MIX_W), F32),
            pltpu.VMEM((tt, MIX_W), F32),
            pltpu.VMEM((tt, MIX_W), F32),
            pltpu.VMEM((SUBLANES, MIX_W), F32),
        ],
        compiler_params=_cparams(("parallel", "arbitrary")),
        name="cd_prompt",
    )(zb3, zb3, zb3, zb3, zb3, scw, lcw, lcb, wa, ba, wx, bx, lam)


def _merge_body(x_ref, oa_ref, ob_ref, oc_ref, od_ref, g0_ref, g1_ref, g2_ref, g3_ref, wbr_ref, wo_ref, o_ref):
    y = None
    for nbr, (br, g_ref) in enumerate(zip((oa_ref, ob_ref, oc_ref, od_ref), (g0_ref, g1_ref, g2_ref, g3_ref))):
        yb = _dot(br[...], wbr_ref[nbr]) * g_ref[...].astype(F32)
        y = yb if y is None else y + yb
    o_ref[...] = x_ref[...] + _dot(y, wo_ref[...])


def _merge(x, oa, ob, oc, od, gates, wbr, wo):
    m = x.shape[0]
    tm = min(m, 512)
    row = lambda w: pl.BlockSpec((tm, w), lambda i: (i, 0))
    gate_specs = [pl.BlockSpec((tm, D_MODEL), lambda i, n=n: (i, n)) for n in range(N_BRANCH)]
    return pl.pallas_call(
        _merge_body,
        grid=(m // tm,),
        in_specs=[row(D_MODEL)] + [row(MIX_W)] * N_BRANCH + gate_specs +
                 [pl.BlockSpec((N_BRANCH, MIX_W, D_MODEL), lambda i: (0, 0, 0)),
                  pl.BlockSpec((D_MODEL, D_MODEL), lambda i: (0, 0))],
        out_specs=row(D_MODEL),
        out_shape=jax.ShapeDtypeStruct((m, D_MODEL), F32),
        compiler_params=_cparams(("parallel",)),
        name="merge",
    )(x, oa, ob, oc, od, *([gates] * N_BRANCH), wbr, wo)


def _norm_body(x_ref, w_ref, o_ref):
    o_ref[...] = _rms(x_ref[...], w_ref[...])


def _final_norm(x, w):
    m = x.shape[0]
    tm = min(m, 1024)
    return pl.pallas_call(
        _norm_body,
        grid=(m // tm,),
        in_specs=[pl.BlockSpec((tm, D_MODEL), lambda i: (i, 0)), pl.BlockSpec((1, D_MODEL), lambda i: (0, 0))],
        out_specs=pl.BlockSpec((tm, D_MODEL), lambda i: (i, 0)),
        out_shape=jax.ShapeDtypeStruct((m, D_MODEL), F32),
        compiler_params=_cparams(("parallel",)),
        name="final_norm",
    )(x, w)


def _spw_body(za_ref, zb_ref, gcs_ref, scs_ref, lcs_ref, lh_ref, gcw_ref, scw_ref, lcw_ref, lcb_ref, wa_ref,
              ba_ref, wx_ref, bx_ref, lam_ref, act_ref, gcs_o, oc_ref, scs_o, od_ref, lh_o, lcs_o):
    w = A_QKV_W
    x = za_ref[:, 0:w]
    y = gcw_ref[GDN_CONV - 1:GDN_CONV, :] * x
    for i in range(GDN_CONV - 1):
        y = y + gcw_ref[i:i + 1, :] * gcs_ref[:, i * w:(i + 1) * w]
    act_ref[...] = jax.nn.silu(y)
    gcs_o[:, 0:(GDN_CONV - 2) * w] = gcs_ref[:, w:(GDN_CONV - 1) * w]
    gcs_o[:, (GDN_CONV - 2) * w:(GDN_CONV - 1) * w] = x

    w = MIX_W
    gate_b = zb_ref[:, ZB_C:ZB_C + w]
    ci = zb_ref[:, ZB_C + w:ZB_C + 2 * w] * zb_ref[:, ZB_C + 2 * w:ZB_C + 3 * w]
    u = scw_ref[SC_CONV - 1:SC_CONV, :] * ci
    for i in range(SC_CONV - 1):
        u = u + scw_ref[i:i + 1, :] * scs_ref[:, i * w:(i + 1) * w]
    oc_ref[...] = gate_b * u
    scs_o[:, 0:(SC_CONV - 2) * w] = scs_ref[:, w:(SC_CONV - 1) * w]
    scs_o[:, (SC_CONV - 2) * w:(SC_CONV - 1) * w] = ci

    xd = zb_ref[:, ZB_D:ZB_D + w]
    gate_d = zb_ref[:, ZB_D + w:ZB_D + 2 * w]
    xc = lcw_ref[LRU_CONV - 1:LRU_CONV, :] * xd
    for i in range(LRU_CONV - 1):
        xc = xc + lcw_ref[i:i + 1, :] * lcs_ref[:, i * w:(i + 1) * w]
    xc = xc + lcb_ref[...]
    a, bt = _lru_gates(xc, wa_ref, ba_ref, wx_ref, bx_ref, lam_ref)
    hnew = a * lh_ref[...] + bt
    lh_o[...] = hnew
    od_ref[...] = hnew * jax.nn.gelu(gate_d)
    lcs_o[:, 0:(LRU_CONV - 2) * w] = lcs_ref[:, w:(LRU_CONV - 1) * w]
    lcs_o[:, (LRU_CONV - 2) * w:(LRU_CONV - 1) * w] = xd


def _sample_pointwise(za, zb, gcs, scs, lcs, lh, gcw, scw, lcw, lcb, wa, ba, wx, bx, lam):
    bd = za.shape[0]
    shapes = [(bd, A_QKV_W), gcs.shape, (bd, MIX_W), scs.shape, (bd, MIX_W), lh.shape, lcs.shape]
    return pl.pallas_call(
        _spw_body,
        out_shape=[jax.ShapeDtypeStruct(s, F32) for s in shapes],
        compiler_params=pltpu.CompilerParams(vmem_limit_bytes=VMEM_LIMIT),
        name="sample_pointwise",
    )(za, zb, gcs, scs, lcs, lh, gcw, scw, lcw, lcb, wa, ba, wx, bx, lam)


def _col(row, diag):
    return jnp.sum(jnp.where(diag, row, 0.0), axis=1, keepdims=True)


def _sstep_body(act_ref, za_ref, zb_ref, ba_ref, s_ref, alog_ref, dtb_ref, nw_ref,
                k0_ref, v0_ref, k1_ref, v1_ref, k2_ref, v2_ref, bb_ref, bn_ref,
                oa_ref, ob_ref, s_out_ref):
    e = GDN_DK
    ri = lax.broadcasted_iota(jnp.int32, (e, e), 0)
    ci = lax.broadcasted_iota(jnp.int32, (e, e), 1)
    diag = ri == ci

    ba = ba_ref[...]
    beta_all = jax.nn.sigmoid(ba)
    g_all = -jnp.exp(alog_ref[...]) * _softplus(ba + dtb_ref[...])
    hd = range(GDN_HEADS)
    q = [act_ref[:, h * e:(h + 1) * e] for h in hd]
    k = [act_ref[:, GDN_HEADS * e + h * e:GDN_HEADS * e + (h + 1) * e] for h in hd]
    v = [act_ref[:, 2 * GDN_HEADS * e + h * GDN_DV:2 * GDN_HEADS * e + (h + 1) * GDN_DV] for h in hd]
    q = [q[h] * lax.rsqrt(jnp.sum(q[h] * q[h], axis=-1, keepdims=True) + NORM_EPS) * (e ** -0.5) for h in hd]
    k = [k[h] * lax.rsqrt(jnp.sum(k[h] * k[h], axis=-1, keepdims=True) + NORM_EPS) for h in hd]
    beta = [beta_all[:, h:h + 1] for h in hd]
    eg = [jnp.exp(g_all[:, GDN_HEADS + h:GDN_HEADS + h + 1]) for h in hd]
    s = [s_ref[h] for h in hd]
    kcol = [_col(k[h], diag) for h in hd]
    qcol = [_col(q[h], diag) for h in hd]
    ks = [jnp.sum(kcol[h] * s[h], axis=0, keepdims=True) for h in hd]
    qs = [jnp.sum(qcol[h] * s[h], axis=0, keepdims=True) for h in hd]
    v_new = [beta[h] * v[h] - (beta[h] * eg[h]) * ks[h] for h in hd]
    qk = [jnp.sum(q[h] * k[h], axis=-1, keepdims=True) for h in hd]
    for h in hd:
        o = eg[h] * qs[h] + qk[h] * v_new[h]
        s_out_ref[h] = s[h] * eg[h] + kcol[h] * v_new[h]
        hv = slice(h * GDN_DV, (h + 1) * GDN_DV)
        oa_ref[:, hv] = _rms(o, nw_ref[...]) * jax.nn.silu(za_ref[:, A_QKV_W + h * GDN_DV:A_QKV_W + (h + 1) * GDN_DV])

    kv = ((k0_ref, v0_ref), (k1_ref, v1_ref), (k2_ref, v2_ref))
    pairs = [(h, g) for h in range(DIL_GROUP_HEADS) for g in range(DIL_GROUPS)]
    c0 = {(h, g): g * MIX_W + h * e for h, g in pairs}
    q = {p: zb_ref[:, ZB_Q + c0[p]:ZB_Q + c0[p] + e] for p in pairs}
    kn = {p: zb_ref[:, ZB_K + c0[p]:ZB_K + c0[p] + e] for p in pairs}
    vn = {p: zb_ref[:, ZB_V + c0[p]:ZB_V + c0[p] + e] for p in pairs}
    sb = {(h, g): jnp.sum(kv[g][0][:, h, :] * q[(h, g)], axis=-1, keepdims=True) for h, g in pairs}
    sn = {p: jnp.sum(kn[p] * q[p], axis=-1, keepdims=True) for p in pairs}
    for h, g in pairs:
        col = g * DIL_GROUP_HEADS + h
        sb[(h, g)] = sb[(h, g)] * (e ** -0.5) + bb_ref[:, col:col + 1]
        sn[(h, g)] = sn[(h, g)] * (e ** -0.5) + bn_ref[:, col:col + 1]
    m = {p: jnp.maximum(jnp.max(sb[p], axis=0, keepdims=True), sn[p]) for p in pairs}
    pb = {p: jnp.exp(sb[p] - m[p]) for p in pairs}
    pn = {p: jnp.exp(sn[p] - m[p]) for p in pairs}
    l = {p: jnp.sum(pb[p], axis=0, keepdims=True) + pn[p] for p in pairs}
    out = {(h, g): (jnp.sum(pb[(h, g)] * kv[g][1][:, h, :], axis=0, keepdims=True) + pn[(h, g)] * vn[(h, g)]) / l[(h, g)]
           for h, g in pairs}
    lse = {p: m[p] + jnp.log(l[p]) for p in pairs}
    for h in range(DIL_GROUP_HEADS):
        lses = [lse[(h, g)] for g in range(DIL_GROUPS)]
        mm = functools.reduce(jnp.maximum, lses)
        es = [jnp.exp(x - mm) for x in lses]
        den = functools.reduce(lambda p, q_: p + q_, es)
        ob_ref[:, h * e:(h + 1) * e] = functools.reduce(
            lambda p, q_: p + q_, [(es[g] / den) * out[(h, g)] for g in range(DIL_GROUPS)])


def _sample_step(act3, za3, zb3, ba3, s_in, alog_row, dtb_row, nw, caches, layer, bias_buf, bias_new):
    bd = act3.shape[0]

    def vec(w):
        return pl.BlockSpec((None, 1, w), lambda i: (i, 0, 0))

    def full(shape):
        return pl.BlockSpec(shape, lambda i: (0,) * len(shape))

    cache_specs, cache_args = [], []
    for cch in caches:
        for kvi in range(2):
            cache_specs.append(pl.BlockSpec((None, None, None, DIL_BLOCK, None, DIL_GROUP_HEADS, DIL_HEAD_DIM),
                                            lambda i, kvi=kvi: (layer, i, kvi, 0, 0, 0, 0)))
            cache_args.append(cch)
    st = pl.BlockSpec((None, GDN_HEADS, GDN_DK, GDN_DV), lambda i: (i, 0, 0, 0))
    return pl.pallas_call(
        _sstep_body,
        grid=(bd,),
        in_specs=[vec(A_QKV_W), vec(ZA_W), vec(ZB_W), vec(BA_W), st,
                  full((1, BA_W)), full((1, BA_W)), full((1, GDN_DV))] + cache_specs +
                 [full((DIL_BLOCK, DIL_HEADS)), full((1, DIL_HEADS))],
        out_specs=[vec(MIX_W), vec(MIX_W), st],
        out_shape=[jax.ShapeDtypeStruct((bd, 1, MIX_W), F32), jax.ShapeDtypeStruct((bd, 1, MIX_W), F32),
                   jax.ShapeDtypeStruct(s_in.shape, F32)],
        compiler_params=_cparams(("parallel",)),
        name="sample_step",
    )(act3, za3, zb3, ba3, s_in, alog_row, dtb_row, nw, *cache_args, bias_buf, bias_new)


def _shift_body(a_ref, b_ref, o_ref, *, rb):
    o_ref[:, pl.ds(0, rb - 1)] = a_ref[:, pl.ds(1, rb - 1)]
    o_ref[:, pl.ds(rb - 1, 1)] = b_ref[...]


def _shift_cache(cache):
    depth, bd, _, rows, nh, e = cache.shape
    rb = min(rows, 512)
    return pl.pallas_call(
        functools.partial(_shift_body, rb=rb),
        grid=(depth, bd, rows // rb),
        in_specs=[
            pl.BlockSpec((None, None, 2, rb, nh, e), lambda l, i, j: (l, i, 0, j, 0, 0)),
            pl.BlockSpec((None, None, 2, 1, nh, e), lambda l, i, j: (l, i, 0, jnp.minimum((j + 1) * rb, rows - 1), 0, 0)),
        ],
        out_specs=pl.BlockSpec((None, None, 2, rb, nh, e), lambda l, i, j: (l, i, 0, j, 0, 0)),
        out_shape=jax.ShapeDtypeStruct(cache.shape, F32),
        compiler_params=_cparams(("parallel", "parallel", "parallel")),
        name="shift_cache",
    )(cache, cache)


def _setrow_body(buf_ref, k_ref, v_ref, o_ref):
    del buf_ref
    e = DIL_HEAD_DIM
    for h in range(DIL_GROUP_HEADS):
        o_ref[0, 0, h:h + 1, :] = k_ref[:, h * e:(h + 1) * e]
        o_ref[1, 0, h:h + 1, :] = v_ref[:, h * e:(h + 1) * e]


def _set_last_row(buf, layer, zb3, gi):
    _, bd, _, rows, nh, e = buf.shape
    ck = (ZB_K + gi * MIX_W) // MIX_W
    cv = (ZB_V + gi * MIX_W) // MIX_W
    return pl.pallas_call(
        _setrow_body,
        grid=(bd,),
        in_specs=[pl.BlockSpec(memory_space=pl.ANY),
                  pl.BlockSpec((None, 1, MIX_W), lambda i: (i, 0, ck)),
                  pl.BlockSpec((None, 1, MIX_W), lambda i: (i, 0, cv))],
        out_specs=pl.BlockSpec((None, None, 2, 1, nh, e), lambda i: (layer, i, 0, rows - 1, 0, 0)),
        out_shape=jax.ShapeDtypeStruct(buf.shape, F32),
        input_output_aliases={0: 0},
        compiler_params=_cparams(("parallel",)),
        name="set_last_row",
    )(buf, zb3, zb3)


def _t5_bucket(dist):
    exact = REL_BUCKETS // 2
    d = np.maximum(dist, 1).astype(np.float32)
    large = exact + (np.log(d / exact) / math.log(REL_MAX_DIST / exact) * (REL_BUCKETS - exact)).astype(np.int32)
    return np.where(dist < exact, dist, np.minimum(large, REL_BUCKETS - 1)).astype(np.int32)


def _prompt_bias(rel_bias):
    blk = DIL_BLOCK
    period = 4 * blk
    out = []
    for gi, (window, dilation) in enumerate(DIL_PATTERNS):
        n_off = window // dilation
        tab = rel_bias[:, gi * DIL_GROUP_HEADS:(gi + 1) * DIL_GROUP_HEADS]
        vals = tab[_t5_bucket(np.arange(n_off + 1) * dilation)].T.astype(F32)
        w = jnp.full((DIL_GROUP_HEADS, period), -jnp.inf, F32)
        w = lax.dynamic_update_slice(w, vals, (0, blk - 1))
        r = jnp.tile(w, (1, blk + 1))[:, :blk * (period + 1)].reshape(DIL_GROUP_HEADS, blk, period + 1)
        out.append(r[:, :, 0:2 * blk][:, :, ::-1])
    return jnp.stack(out, axis=0)


def _step_bias(rel_bias):
    cols_buf, cols_new = [], []
    for gi, (window, dilation) in enumerate(DIL_PATTERNS):
        n_off = window // dilation
        tab = rel_bias[:, gi * DIL_GROUP_HEADS:(gi + 1) * DIL_GROUP_HEADS]
        j = n_off - np.arange(n_off)
        cols_buf.append(tab[_t5_bucket(j * dilation)])
        cols_new.append(tab[_t5_bucket(np.zeros((1,), np.int64))])
    return jnp.concatenate(cols_buf, axis=1).astype(F32), jnp.concatenate(cols_new, axis=1).astype(F32)


def _block_diag(w):
    n, c, _ = w.shape
    eye = jnp.eye(n, dtype=w.dtype)
    return (eye[:, None, :, None] * w[:, :, None, :]).reshape(n * c, n * c)


def _layer_weights(p, l):
    w_in = p['w_in'][l]
    b0, c0 = A_W, A_W + B_W
    zg0 = A_QKV_W
    ba0 = A_QKV_W + GDN_HEADS * GDN_DV
    w_all = jnp.concatenate([w_in[:, 0:ba0], w_in[:, b0:], p['w_gate'][l]], axis=1).astype(BF16)
    b_all = p['b_gate'][l][None]
    w_ba = jnp.pad(w_in[:, ba0:b0], ((0, 0), (0, BA_W - 2 * GDN_HEADS))).astype(BF16)
    pad_row = lambda v: jnp.pad(v, (GDN_HEADS, BA_W - 2 * GDN_HEADS))[None]
    row = lambda v: v[None].astype(F32)
    del zg0, c0
    return dict(
        n1=row(p['norm_ffn1'][l]), gu1=p['ffn1_w_gu'][l].astype(BF16), dn1=p['ffn1_w_down'][l].astype(BF16),
        nm=row(p['norm_mix'][l]), w_all=w_all, b_all=b_all, w_ba=w_ba,
        gcw=p['gdn_conv_w'][l], alog=pad_row(p['gdn_a_log'][l]), dtb=pad_row(p['gdn_dt_bias'][l]),
        gnw=row(p['gdn_norm_w'][l]), scw=p['sc_conv_w'][l], lcw=p['lru_conv_w'][l], lcb=row(p['lru_conv_b'][l]),
        wa=_block_diag(p['lru_wa'][l]).astype(BF16), ba=row(p['lru_ba'][l]),
        wx=_block_diag(p['lru_wx'][l]).astype(BF16), bx=row(p['lru_bx'][l]), lam=row(p['lru_lambda'][l]),
        wbr=p['w_branch'][l].astype(BF16), wo=p['w_o'][l].astype(BF16),
        n2=row(p['norm_ffn2'][l]), gu2=p['ffn2_w_gu'][l].astype(BF16), dn2=p['ffn2_w_down'][l].astype(BF16),
    )


def _prompt_layer(x, w, bias, b, t):
    m = b * t
    x = _ffn(x, w['n1'], w['gu1'], w['dn1'])
    za, zb, gates = _proj(x, w['nm'], w['w_all'], w['b_all'])
    ba = _proj_ba(x, w['nm'], w['w_ba'])
    za3 = za.reshape(b, t, ZA_W)
    zb3 = zb.reshape(b, t, ZB_W)
    o_a, s_gdn = _gdn_prompt(za3, ba.reshape(b, t, BA_W), w['gcw'], w['alog'], w['dtb'], w['gnw'])
    o_b = _dil_prompt(zb3, bias)
    o_c, o_d, sc_st, lru_h = _cd_prompt(zb3, w['scw'], w['lcw'], w['lcb'], w['wa'], w['ba'], w['wx'], w['bx'], w['lam'])
    x = _merge(x, o_a.reshape(m, MIX_W), o_b.reshape(m, MIX_W), o_c.reshape(m, MIX_W), o_d.reshape(m, MIX_W),
               gates, w['wbr'], w['wo'])
    x = _ffn(x, w['n2'], w['gu2'], w['dn2'])
    gdn_conv = za3[:, t - (GDN_CONV - 1):, 0:A_QKV_W]
    bufs = []
    for gi, (window, _) in enumerate(DIL_PATTERNS):
        rows = min(window, t)
        k = zb3[:, t - rows:, ZB_K + gi * MIX_W:ZB_K + (gi + 1) * MIX_W]
        v = zb3[:, t - rows:, ZB_V + gi * MIX_W:ZB_V + (gi + 1) * MIX_W]
        bufs.append(jnp.stack([k, v], axis=1).reshape(b, 2, rows, DIL_GROUP_HEADS, DIL_HEAD_DIM))
    sc_conv = sc_st[:, SUBLANES - (SC_CONV - 1):]
    lru_conv = zb3[:, t - (LRU_CONV - 1):, ZB_D:ZB_D + MIX_W]
    return x, (s_gdn, gdn_conv, bufs[0], bufs[1], bufs[2], sc_conv, lru_h[:, 0], lru_conv)


def _sample_layer(x, w, l, st, shifted, step_bias):
    bd = x.shape[0]
    s_gdn, gdn_conv, caches, sc_conv, lru_h, lru_conv = st
    x = _ffn(x, w['n1'], w['gu1'], w['dn1'])
    za, zb, gates = _proj(x, w['nm'], w['w_all'], w['b_all'])
    ba = _proj_ba(x, w['nm'], w['w_ba'])
    act, gdn_conv_new, o_c, sc_new, o_d, lru_h_new, lru_conv_new = _sample_pointwise(
        za, zb, gdn_conv.reshape(bd, -1), sc_conv.reshape(bd, -1), lru_conv.reshape(bd, -1), lru_h,
        w['gcw'], w['scw'], w['lcw'], w['lcb'], w['wa'], w['ba'], w['wx'], w['bx'], w['lam'])
    zb3 = zb.reshape(bd, 1, ZB_W)
    o_a, o_b, s_new = _sample_step(act.reshape(bd, 1, A_QKV_W), za.reshape(bd, 1, ZA_W), zb3,
                                   ba.reshape(bd, 1, BA_W), s_gdn, w['alog'], w['dtb'], w['gnw'], caches, l,
                                   *step_bias)
    shifted = [_set_last_row(buf, l, zb3, gi) for gi, buf in enumerate(shifted)]
    x = _merge(x, o_a.reshape(bd, MIX_W), o_b.reshape(bd, MIX_W), o_c, o_d, gates, w['wbr'], w['wo'])
    x = _ffn(x, w['n2'], w['gu2'], w['dn2'])
    new = (s_new, gdn_conv_new.reshape(gdn_conv.shape), sc_new.reshape(sc_conv.shape), lru_h_new,
           lru_conv_new.reshape(lru_conv.shape))
    return x, new, shifted


def kernel(x_prompt, x_sample, state_gdn, state_gdn_conv, cache_dil_w128, cache_dil_w512, cache_dil_w2048,
           state_sc_conv, state_lru, state_lru_conv, norm_ffn1, ffn1_w_gu, ffn1_w_down, norm_mix, w_in,
           gdn_conv_w, gdn_a_log, gdn_dt_bias, gdn_norm_w, rel_bias, sc_conv_w, lru_conv_w, lru_conv_b,
           lru_wa, lru_ba, lru_wx, lru_bx, lru_lambda, w_gate, b_gate, w_branch, w_o, norm_ffn2, ffn2_w_gu,
           ffn2_w_down, final_norm):
    p = dict(norm_ffn1=norm_ffn1, ffn1_w_gu=ffn1_w_gu, ffn1_w_down=ffn1_w_down, norm_mix=norm_mix, w_in=w_in,
             gdn_conv_w=gdn_conv_w, gdn_a_log=gdn_a_log, gdn_dt_bias=gdn_dt_bias, gdn_norm_w=gdn_norm_w,
             sc_conv_w=sc_conv_w, lru_conv_w=lru_conv_w, lru_conv_b=lru_conv_b, lru_wa=lru_wa, lru_ba=lru_ba,
             lru_wx=lru_wx, lru_bx=lru_bx, lru_lambda=lru_lambda, w_gate=w_gate, b_gate=b_gate,
             w_branch=w_branch, w_o=w_o, norm_ffn2=norm_ffn2, ffn2_w_gu=ffn2_w_gu, ffn2_w_down=ffn2_w_down)
    weights = [_layer_weights(p, l) for l in range(DEPTH)]
    fnw = final_norm[None]
    b, t, _ = x_prompt.shape
    bd = x_sample.shape[0]
    bias = _prompt_bias(rel_bias)
    step_bias = _step_bias(rel_bias)

    x = x_prompt.reshape(b * t, D_MODEL)
    p_states = []
    for l in range(DEPTH):
        x, new = _prompt_layer(x, weights[l], bias, b, t)
        p_states.append(new)
    y_prompt = _final_norm(x, fnw).reshape(b, t, D_MODEL)
    p_out = tuple(jnp.stack(zs, axis=0) for zs in zip(*p_states))

    caches = (cache_dil_w128, cache_dil_w512, cache_dil_w2048)
    shifted = [_shift_cache(c) for c in caches]
    strided = tuple(c.reshape(DEPTH, bd, 2, c.shape[3] // d, d, DIL_GROUP_HEADS, DIL_HEAD_DIM)
                    for c, (_, d) in zip(caches, DIL_PATTERNS))
    x = x_sample.reshape(bd, D_MODEL)
    s_states = []
    for l in range(DEPTH):
        st = (state_gdn[l], state_gdn_conv[l], strided, state_sc_conv[l], state_lru[l], state_lru_conv[l])
        x, new, shifted = _sample_layer(x, weights[l], l, st, shifted, step_bias)
        s_states.append(new)
    y_sample = _final_norm(x, fnw).reshape(bd, 1, D_MODEL)
    s_gdn, s_gdn_conv, s_sc_conv, s_lru, s_lru_conv = (jnp.stack(zs, axis=0) for zs in zip(*s_states))

    p_gdn, p_gdn_conv, p_w128, p_w512, p_w2048, p_sc_conv, p_lru, p_lru_conv = p_out
    return (y_prompt, y_sample, p_gdn, s_gdn, p_gdn_conv, s_gdn_conv, p_w128, shifted[0], p_w512, shifted[1],
            p_w2048, shifted[2], p_sc_conv, s_sc_conv, p_lru, s_lru, p_lru_conv, s_lru_conv)
```

```python
import functools
import math

import numpy as np
import jax
import jax.numpy as jnp
from jax import lax
from jax.experimental import pallas as pl
from jax.experimental.pallas import tpu as pltpu

D_MODEL = 1024
DEPTH = 4
MIX_W = D_MODEL // 2
D_FF = 2816
NORM_EPS = 1e-6
N_BRANCH = 4
GDN_HEADS = 4
GDN_DK = 128
GDN_DV = MIX_W // GDN_HEADS
GDN_CONV = 4
GDN_CHUNK = 64
DIL_PATTERNS = ((128, 1), (512, 4), (2048, 16))
DIL_GROUPS = len(DIL_PATTERNS)
DIL_GROUP_HEADS = 4
DIL_HEAD_DIM = MIX_W // DIL_GROUP_HEADS
DIL_HEADS = DIL_GROUPS * DIL_GROUP_HEADS
DIL_BLOCK = 128
REL_BUCKETS = 32
REL_MAX_DIST = 2048
SC_CONV = 3
LRU_BLOCKS = 8
LRU_BLOCK_W = MIX_W // LRU_BLOCKS
LRU_CONV = 4
LRU_C = 8.0
A_QKV_W = GDN_HEADS * (2 * GDN_DK + GDN_DV)
A_W = A_QKV_W + GDN_HEADS * GDN_DV + 2 * GDN_HEADS
B_W = 3 * DIL_HEADS * DIL_HEAD_DIM
C_W = 3 * MIX_W
D_W = 2 * MIX_W

LANES = 128
SUBLANES = 8
VMEM_LIMIT = 56 * 1024 * 1024

ZA_W = A_QKV_W + MIX_W
ZB_Q, ZB_K, ZB_V = 0, B_W // 3, 2 * B_W // 3
ZB_C = B_W
ZB_D = B_W + C_W
ZB_W = B_W + C_W + D_W
G_W = N_BRANCH * D_MODEL
PROJ_TN = 1024
BA_W = LANES
DIL_TB = DIL_BLOCK * max(d for _, d in DIL_PATTERNS)
DIL_UNROLL = 8

F32 = jnp.float32
BF16 = jnp.bfloat16
NT_DIMS = (((1,), (1,)), ((), ()))
TN_DIMS = (((0,), (0,)), ((), ()))


def _cparams(sem):
    return pltpu.CompilerParams(dimension_semantics=sem, vmem_limit_bytes=VMEM_LIMIT)


def _rms(x, w):
    return x * lax.rsqrt(jnp.mean(x * x, axis=-1, keepdims=True) + NORM_EPS) * w


def _softplus(x):
    return jnp.maximum(x, 0.0) + jnp.log1p(jnp.exp(-jnp.abs(x)))


def _causal_conv(xb, w_ref, tt):
    taps = w_ref.shape[0]
    y = None
    for i in range(taps):
        back = taps - 1 - i
        xs = xb if back == 0 else pltpu.roll(xb, back, 0)
        term = w_ref[i:i + 1, :] * xs[SUBLANES:SUBLANES + tt]
        y = term if y is None else y + term
    return y


def _dot(a, b):
    return jnp.dot(a.astype(BF16), b.astype(BF16), preferred_element_type=F32)


def _dot_nt(a, b):
    return lax.dot_general(a.astype(BF16), b.astype(BF16), NT_DIMS, preferred_element_type=F32)


def _dot_tn(a, b):
    return lax.dot_general(a.astype(BF16), b.astype(BF16), TN_DIMS, preferred_element_type=F32)


def _ffn_body(x_ref, nw_ref, wg_ref, wu_ref, wd_ref, fw_ref, o_ref, h_ref, acc_ref, *, final):
    j = pl.program_id(1)

    @pl.when(j == 0)
    def _():
        h_ref[...] = _rms(x_ref[...], nw_ref[...]).astype(BF16)
        acc_ref[...] = jnp.zeros_like(acc_ref)

    h = h_ref[...]
    g = _dot(h, wg_ref[...])
    u = _dot(h, wu_ref[...])
    acc_ref[...] += _dot(jax.nn.silu(g) * u, wd_ref[...])

    @pl.when(j == pl.num_programs(1) - 1)
    def _():
        y = x_ref[...] + 0.5 * acc_ref[...]
        o_ref[...] = _rms(y, fw_ref[...]) if final else y


def _ffn(x, nw, w_gu, w_down, fw, final=False):
    m = x.shape[0]
    tm = min(m, 512)
    nf = 2
    tf = D_FF // nf
    return pl.pallas_call(
        functools.partial(_ffn_body, final=final),
        grid=(m // tm, nf),
        in_specs=[
            pl.BlockSpec((tm, D_MODEL), lambda i, j: (i, 0)),
            pl.BlockSpec((1, D_MODEL), lambda i, j: (0, 0)),
            pl.BlockSpec((D_MODEL, tf), lambda i, j: (0, j)),
            pl.BlockSpec((D_MODEL, tf), lambda i, j: (0, nf + j)),
            pl.BlockSpec((tf, D_MODEL), lambda i, j: (j, 0)),
            pl.BlockSpec((1, D_MODEL), lambda i, j: (0, 0)),
        ],
        out_specs=pl.BlockSpec((tm, D_MODEL), lambda i, j: (i, 0)),
        out_shape=jax.ShapeDtypeStruct((m, D_MODEL), F32),
        scratch_shapes=[pltpu.VMEM((tm, D_MODEL), BF16), pltpu.VMEM((tm, D_MODEL), F32)],
        compiler_params=_cparams(("parallel", "arbitrary")),
        name="ffn",
    )(x, nw, w_gu, w_gu, w_down, fw)


def _proj_body(x_ref, nw_ref, w_ref, b_ref, wba_ref, za_ref, zb_ref, g_ref, ba_ref, h_ref, *, ja, jb):
    j = pl.program_id(1)

    @pl.when(j == 0)
    def _():
        h = _rms(x_ref[...], nw_ref[...]).astype(BF16)
        h_ref[...] = h
        ba_ref[...] = _dot(h, wba_ref[...])

    @pl.when(j < ja)
    def _():
        za_ref[...] = _dot(h_ref[...], w_ref[...])

    @pl.when((j >= ja) & (j < jb))
    def _():
        zb_ref[...] = _dot(h_ref[...], w_ref[...])

    @pl.when(j >= jb)
    def _():
        g_ref[...] = jax.nn.sigmoid(_dot(h_ref[...], w_ref[...]) + b_ref[...]).astype(BF16)


def _proj(x, nw, w, b, w_ba):
    m = x.shape[0]
    tm = min(m, 1024)
    tn = PROJ_TN
    ja, jb = ZA_W // tn, (ZA_W + ZB_W) // tn
    nj = (ZA_W + ZB_W + G_W) // tn
    return pl.pallas_call(
        functools.partial(_proj_body, ja=ja, jb=jb),
        grid=(m // tm, nj),
        in_specs=[
            pl.BlockSpec((tm, D_MODEL), lambda i, j: (i, 0)),
            pl.BlockSpec((1, D_MODEL), lambda i, j: (0, 0)),
            pl.BlockSpec((D_MODEL, tn), lambda i, j: (0, j)),
            pl.BlockSpec((1, tn), lambda i, j: (0, jnp.maximum(j - jb, 0))),
            pl.BlockSpec((D_MODEL, BA_W), lambda i, j: (0, 0)),
        ],
        out_specs=[
            pl.BlockSpec((tm, tn), lambda i, j: (i, jnp.minimum(j, ja - 1))),
            pl.BlockSpec((tm, tn), lambda i, j: (i, jnp.clip(j - ja, 0, jb - ja - 1))),
            pl.BlockSpec((tm, tn), lambda i, j: (i, jnp.maximum(j - jb, 0))),
            pl.BlockSpec((tm, BA_W), lambda i, j: (i, 0)),
        ],
        out_shape=[
            jax.ShapeDtypeStruct((m, ZA_W), F32),
            jax.ShapeDtypeStruct((m, ZB_W), F32),
            jax.ShapeDtypeStruct((m, G_W), BF16),
            jax.ShapeDtypeStruct((m, BA_W), F32),
        ],
        scratch_shapes=[pltpu.VMEM((tm, D_MODEL), BF16)],
        compiler_params=_cparams(("parallel", "arbitrary")),
        name="proj",
    )(x, nw, w, b, w_ba)


def _gdn_body(qkv_ref, zg_ref, ba_ref, cw_ref, alog_ref, dtb_ref, nw_ref, o_ref, s_out_ref,
              xbuf, act, beta_s, gc_s, u_s, wq_s, at_s, kd_s, s_s, *, tt, nb):
    t = pl.program_id(0)
    c = GDN_CHUNK
    nchunk = tt // c

    @pl.when(t == 0)
    def _():
        for b in range(nb):
            xbuf[b, pl.ds(0, SUBLANES), :] = jnp.zeros((SUBLANES, A_QKV_W), F32)
        s_s[...] = jnp.zeros_like(s_s)

    rowmod = lax.broadcasted_iota(jnp.int32, (tt, BA_W), 0) & (c - 1)
    for b in range(nb):
        xbuf[b, pl.ds(SUBLANES, tt), :] = qkv_ref[b]
        act[b] = jax.nn.silu(_causal_conv(xbuf[b], cw_ref, tt))
        xbuf[b, pl.ds(0, SUBLANES), :] = xbuf[b, pl.ds(tt, SUBLANES), :]

        ba = ba_ref[b]
        beta_s[b] = jax.nn.sigmoid(ba)
        g = -jnp.exp(alog_ref[...]) * _softplus(ba + dtb_ref[...])
        sh = 1
        while sh < c:
            g = g + jnp.where(rowmod >= sh, pltpu.roll(g, sh, 0), 0.0)
            sh *= 2
        gc_s[b] = g

    ri = lax.broadcasted_iota(jnp.int32, (c, c), 0)
    ci = lax.broadcasted_iota(jnp.int32, (c, c), 1)
    causal = ri >= ci
    strict = ri > ci
    diag = ri == ci
    eye = jnp.where(diag, 1.0, 0.0).astype(F32)

    heads = range(GDN_HEADS)

    def prep(i, carry):
        chains = [(b, 2 * i + j, h) for b in range(nb) for j in range(2) for h in heads]
        n = len(chains)
        rows_l, gcol_l, decay_l, q_l, k_l, kb_l, vb_l = [], [], [], [], [], [], []
        for b, ch, h in chains:
            rows = pl.ds(pl.multiple_of(ch * c, c), c)
            gcol = gc_s[b, rows, GDN_HEADS + h:GDN_HEADS + h + 1]
            grow = jnp.sum(jnp.where(diag, gcol, 0.0), axis=0, keepdims=True)
            decay = jnp.exp(jnp.where(causal, gcol - grow, -jnp.inf))
            q = act[b, rows, h * GDN_DK:(h + 1) * GDN_DK]
            k = act[b, rows, GDN_HEADS * GDN_DK + h * GDN_DK:GDN_HEADS * GDN_DK + (h + 1) * GDN_DK]
            v = act[b, rows, 2 * GDN_HEADS * GDN_DK + h * GDN_DV:2 * GDN_HEADS * GDN_DK + (h + 1) * GDN_DV]
            q = q * lax.rsqrt(jnp.sum(q * q, axis=-1, keepdims=True) + NORM_EPS) * (GDN_DK ** -0.5)
            k = k * lax.rsqrt(jnp.sum(k * k, axis=-1, keepdims=True) + NORM_EPS)
            beta = beta_s[b, rows, h:h + 1]
            rows_l.append(rows)
            gcol_l.append(gcol)
            decay_l.append(decay)
            q_l.append(q)
            k_l.append(k)
            kb_l.append(k * beta)
            vb_l.append(v * beta)
        kq_l = [_dot_nt(jnp.concatenate([kb_l[j], q_l[j]], axis=0), k_l[j]) for j in range(n)]
        for j, (b, ch, h) in enumerate(chains):
            at_s[b, rows_l[j], h * LANES:h * LANES + c] = kq_l[j][c:] * decay_l[j]
        qq = [-jnp.where(strict, kq_l[j][:c] * decay_l[j], 0.0) for j in range(n)]
        yy = [eye + qq[j] for j in range(n)]
        qq = [_dot(qq[j], qq[j]) for j in range(n)]
        for _ in range(int(math.log2(c)) - 2):
            yq = [_dot(jnp.concatenate([yy[j], qq[j]], axis=0), qq[j]) for j in range(n)]
            yy = [yy[j] + yq[j][:c] for j in range(n)]
            qq = [yq[j][c:] for j in range(n)]
        yq = [_dot(yy[j], qq[j]) for j in range(n)]
        tinv = [yy[j] + yq[j] for j in range(n)]
        egc = [jnp.exp(gcol_l[j]) for j in range(n)]
        sol = [_dot(tinv[j], jnp.concatenate([vb_l[j], kb_l[j] * egc[j]], axis=1)) for j in range(n)]
        for j, (b, ch, h) in enumerate(chains):
            hs = slice(h * GDN_DK, (h + 1) * GDN_DK)
            u_s[b, rows_l[j], h * GDN_DV:(h + 1) * GDN_DV] = sol[j][:, :GDN_DV]
            wq_s[b, pl.ds(pl.multiple_of(ch * 2 * c, 2 * c), c), hs] = sol[j][:, GDN_DV:]
            wq_s[b, pl.ds(pl.multiple_of(ch * 2 * c + c, c), c), hs] = q_l[j] * egc[j]
            kd_s[b, rows_l[j], hs] = k_l[j] * jnp.exp(gcol_l[j][c - 1:c, :] - gcol_l[j])
        return carry

    lax.fori_loop(0, nchunk // 2, prep, 0)

    def step(ch, carry):
        rows = pl.ds(pl.multiple_of(ch * c, c), c)
        wq_rows = pl.ds(pl.multiple_of(ch * 2 * c, 2 * c), 2 * c)
        chains = [(b, h) for b in range(nb) for h in heads]
        hs = {h: slice(h * GDN_DK, (h + 1) * GDN_DK) for h in heads}
        hv = {h: slice(h * GDN_DV, (h + 1) * GDN_DV) for h in heads}
        glast = [jnp.exp(gc_s[b, pl.ds(ch * c + c - 1, 1), :]) for b in range(nb)]
        st = [s_s[b, h] for b, h in chains]
        ws = [_dot(wq_s[b, wq_rows, hs[h]], st[j]) for j, (b, h) in enumerate(chains)]
        v_new = [u_s[b, rows, hv[h]] - ws[j][:c] for j, (b, h) in enumerate(chains)]
        av = [_dot(at_s[b, rows, h * LANES:h * LANES + c], v_new[j]) for j, (b, h) in enumerate(chains)]
        kv = [_dot_tn(kd_s[b, rows, hs[h]], v_new[j]) for j, (b, h) in enumerate(chains)]
        for j, (b, h) in enumerate(chains):
            s_s[b, h] = st[j] * glast[b][:, GDN_HEADS + h:GDN_HEADS + h + 1] + kv[j]
            o = _rms(ws[j][c:] + av[j], nw_ref[...]) * jax.nn.silu(zg_ref[b, rows, hv[h]])
            o_ref[b, rows, hv[h]] = o
        return carry

    lax.fori_loop(0, nchunk, step, 0)

    @pl.when(t == pl.num_programs(0) - 1)
    def _():
        s_out_ref[...] = s_s[...]


def _gdn_prompt(za3, ba3, cw, alog_row, dtb_row, nw):
    b, t, _ = za3.shape
    tt = min(t, 256)
    return pl.pallas_call(
        functools.partial(_gdn_body, tt=tt, nb=b),
        grid=(t // tt,),
        in_specs=[
            pl.BlockSpec((b, tt, A_QKV_W), lambda j: (0, j, 0)),
            pl.BlockSpec((b, tt, MIX_W), lambda j: (0, j, A_QKV_W // MIX_W)),
            pl.BlockSpec((b, tt, BA_W), lambda j: (0, j, 0)),
            pl.BlockSpec((GDN_CONV, A_QKV_W), lambda j: (0, 0)),
            pl.BlockSpec((1, BA_W), lambda j: (0, 0)),
            pl.BlockSpec((1, BA_W), lambda j: (0, 0)),
            pl.BlockSpec((1, GDN_DV), lambda j: (0, 0)),
        ],
        out_specs=[
            pl.BlockSpec((b, tt, MIX_W), lambda j: (0, j, 0)),
            pl.BlockSpec((b, GDN_HEADS, GDN_DK, GDN_DV), lambda j: (0, 0, 0, 0)),
        ],
        out_shape=[
            jax.ShapeDtypeStruct((b, t, MIX_W), F32),
            jax.ShapeDtypeStruct((b, GDN_HEADS, GDN_DK, GDN_DV), F32),
        ],
        scratch_shapes=[
            pltpu.VMEM((b, tt + SUBLANES, A_QKV_W), F32),
            pltpu.VMEM((b, tt, A_QKV_W), F32),
            pltpu.VMEM((b, tt, BA_W), F32),
            pltpu.VMEM((b, tt, BA_W), F32),
            pltpu.VMEM((b, tt, MIX_W), F32),
            pltpu.VMEM((b, 2 * tt, MIX_W), F32),
            pltpu.VMEM((b, tt, GDN_HEADS * LANES), F32),
            pltpu.VMEM((b, tt, MIX_W), F32),
            pltpu.VMEM((b, GDN_HEADS, GDN_DK, GDN_DV), F32),
        ],
        compiler_params=_cparams(("arbitrary",)),
        name="gdn_prompt",
    )(za3, za3, ba3, cw, alog_row, dtb_row, nw)


def _dil_body(*refs):
    qkv = refs[:3 * DIL_GROUPS]
    bias_ref, o_ref = refs[3 * DIL_GROUPS], refs[3 * DIL_GROUPS + 1]
    kvbuf = refs[3 * DIL_GROUPS + 2:3 * DIL_GROUPS + 2 + 2 * DIL_GROUPS]
    og, lg = refs[-2], refs[-1]
    n = pl.program_id(2)
    e = DIL_HEAD_DIM
    blk = DIL_BLOCK
    tb = DIL_TB
    for g, (_, d) in enumerate(DIL_PATTERNS):
        q_ref, k_ref, v_ref = qkv[3 * g:3 * g + 3]
        kb, vb = kvbuf[2 * g], kvbuf[2 * g + 1]
        tail = blk * d

        @pl.when(n == 0)
        def _(kb=kb, vb=vb, tail=tail):
            kb[pl.ds(0, tail), :] = jnp.zeros((tail, e), F32)
            vb[pl.ds(0, tail), :] = jnp.zeros((tail, e), F32)

        kb[pl.ds(tail, tb), :] = k_ref[...]
        vb[pl.ds(tail, tb), :] = v_ref[...]
        bp = bias_ref[g, :, 0:blk]
        bc = bias_ref[g, :, blk:2 * blk]
        shift = int(math.log2(d))

        def rows_at(base, d=d):
            return pl.ds(base, blk) if d == 1 else pl.ds(base, blk, stride=d)

        def body(it, carry, d=d, g=g, q_ref=q_ref, kb=kb, vb=vb, bp=bp, bc=bc, shift=shift, tail=tail,
                 rows_at=rows_at):
            us = range(DIL_UNROLL)
            idx = [it * DIL_UNROLL + u for u in us]
            sub = [i >> shift for i in idx]
            base = [sub[u] * tail + (idx[u] & (d - 1)) for u in us]
            q = [q_ref[rows_at(base[u]), :].astype(BF16) for u in us]
            sp = [_dot_nt(q[u], kb[rows_at(base[u]), :]) for u in us]
            sc = [_dot_nt(q[u], kb[rows_at(base[u] + tail), :]) for u in us]
            pp, pc, l, lse = [], [], [], []
            for u in us:
                spu = jnp.where((n > 0) | (sub[u] > 0), sp[u] * (e ** -0.5) + bp, -jnp.inf)
                scu = sc[u] * (e ** -0.5) + bc
                m = jnp.max(jnp.maximum(spu, scu), axis=-1, keepdims=True)
                ppu = jnp.exp(spu - m)
                pcu = jnp.exp(scu - m)
                lu = jnp.sum(ppu + pcu, axis=-1, keepdims=True)
                pp.append(ppu)
                pc.append(pcu)
                l.append(lu)
                lse.append(m + jnp.log(lu))
            op = [_dot(pp[u], vb[rows_at(base[u]), :]) for u in us]
            oc = [_dot(pc[u], vb[rows_at(base[u] + tail), :]) for u in us]
            for u in us:
                og[g, rows_at(base[u]), :] = (op[u] + oc[u]) / l[u]
                lg[g, rows_at(base[u]), :] = jnp.broadcast_to(lse[u], (blk, e))
            return carry

        lax.fori_loop(0, tb // blk // DIL_UNROLL, body, 0)
        kb[pl.ds(0, tail), :] = kb[pl.ds(tb, tail), :]
        vb[pl.ds(0, tail), :] = vb[pl.ds(tb, tail), :]

    lses = [lg[g] for g in range(DIL_GROUPS)]
    m = functools.reduce(jnp.maximum, lses)
    es = [jnp.exp(l - m) for l in lses]
    den = functools.reduce(lambda p, q_: p + q_, es)
    o_ref[...] = functools.reduce(lambda p, q_: p + q_, [(es[g] / den) * og[g] for g in range(DIL_GROUPS)])


def _dil_prompt(zb3, bias):
    b, t, _ = zb3.shape
    tb = DIL_TB
    e = DIL_HEAD_DIM
    specs, args = [], []
    for g in range(DIL_GROUPS):
        for off in (ZB_Q, ZB_K, ZB_V):
            cb = (off + g * MIX_W) // e
            specs.append(pl.BlockSpec((None, tb, e), lambda i, h, n, cb=cb: (i, n, cb + h)))
            args.append(zb3)
    specs.append(pl.BlockSpec((DIL_GROUPS, None, DIL_BLOCK, 2 * DIL_BLOCK), lambda i, h, n: (0, h, 0, 0)))
    scratch = []
    for _, d in DIL_PATTERNS:
        scratch += [pltpu.VMEM((DIL_BLOCK * d + tb, e), F32)] * 2
    scratch += [pltpu.VMEM((DIL_GROUPS, tb, e), F32)] * 2
    return pl.pallas_call(
        _dil_body,
        grid=(b, DIL_GROUP_HEADS, t // tb),
        in_specs=specs,
        out_specs=pl.BlockSpec((None, tb, e), lambda i, h, n: (i, n, h)),
        out_shape=jax.ShapeDtypeStruct((b, t, MIX_W), F32),
        scratch_shapes=scratch,
        compiler_params=_cparams(("parallel", "parallel", "arbitrary")),
        name="dil_prompt",
    )(*args, bias)


def _lru_gates(xc, wa_ref, ba_ref, wx_ref, bx_ref, lam_ref):
    xb = xc.astype(BF16)
    r = jax.nn.sigmoid(_dot(xb, wa_ref[...]) + ba_ref[...])
    i = jax.nn.sigmoid(_dot(xb, wx_ref[...]) + bx_ref[...])
    log_a = -LRU_C * r * _softplus(-lam_ref[...])
    a = jnp.exp(log_a)
    bt = jnp.sqrt(-jnp.tanh(log_a) * (jnp.exp(2.0 * log_a) + 1.0)) * i * xc
    return a, bt


def _cd_body(gb_ref, gc_ref, xi_ref, xd_ref, gd_ref, scw_ref, lcw_ref, lcb_ref, wa_ref, ba_ref, wx_ref,
             bx_ref, lam_ref, oc_ref, od_ref, scst_ref, lruh_ref, cbuf, dbuf, a_s, b_s, h_s, *, tt):
    t = pl.program_id(1)

    @pl.when(t == 0)
    def _():
        cbuf[pl.ds(0, SUBLANES), :] = jnp.zeros((SUBLANES, MIX_W), F32)
        dbuf[pl.ds(0, SUBLANES), :] = jnp.zeros((SUBLANES, MIX_W), F32)
        h_s[...] = jnp.zeros_like(h_s)

    cbuf[pl.ds(SUBLANES, tt), :] = gc_ref[...] * xi_ref[...]
    oc_ref[...] = gb_ref[...] * _causal_conv(cbuf[...], scw_ref, tt)
    tail = cbuf[pl.ds(tt, SUBLANES), :]
    scst_ref[...] = tail
    cbuf[pl.ds(0, SUBLANES), :] = tail

    dbuf[pl.ds(SUBLANES, tt), :] = xd_ref[...]
    xc = _causal_conv(dbuf[...], lcw_ref, tt) + lcb_ref[...]
    dbuf[pl.ds(0, SUBLANES), :] = dbuf[pl.ds(tt, SUBLANES), :]
    a, bt = _lru_gates(xc, wa_ref, ba_ref, wx_ref, bx_ref, lam_ref)
    a_s[...] = a
    b_s[...] = bt

    def scan(g, h):
        rows = pl.ds(pl.multiple_of(g * SUBLANES, SUBLANES), SUBLANES)
        a8 = a_s[rows, :]
        b8 = b_s[rows, :]
        out = []
        for r in range(SUBLANES):
            h = a8[r:r + 1, :] * h + b8[r:r + 1, :]
            out.append(h)
        a_s[rows, :] = jnp.concatenate(out, axis=0)
        return h

    h_last = lax.fori_loop(0, tt // SUBLANES, scan, h_s[0:1, :])
    h_s[...] = jnp.broadcast_to(h_last, h_s.shape)
    lruh_ref[...] = jnp.broadcast_to(h_last, h_s.shape)
    od_ref[...] = a_s[...] * jax.nn.gelu(gd_ref[...])


def _cd_prompt(zb3, scw, lcw, lcb, wa, ba, wx, bx, lam):
    b, t, _ = zb3.shape
    tt = min(t, 512)
    blk = (None, tt, MIX_W)
    c0 = ZB_C // MIX_W

    def zspec(c):
        return pl.BlockSpec(blk, lambda i, j: (i, j, c))

    def full(shape):
        return pl.BlockSpec(shape, lambda i, j: (0,) * len(shape))

    st = pl.BlockSpec((None, SUBLANES, MIX_W), lambda i, j: (i, 0, 0))
    return pl.pallas_call(
        functools.partial(_cd_body, tt=tt),
        grid=(b, t // tt),
        in_specs=[zspec(c0), zspec(c0 + 1), zspec(c0 + 2), zspec(c0 + 3), zspec(c0 + 4),
                  full((SC_CONV, MIX_W)), full((LRU_CONV, MIX_W)), full((1, MIX_W)),
                  full((MIX_W, MIX_W)), full((1, MIX_W)), full((MIX_W, MIX_W)), full((1, MIX_W)),
                  full((1, MIX_W))],
        out_specs=[pl.BlockSpec(blk, lambda i, j: (i, j, 0)), pl.BlockSpec(blk, lambda i, j: (i, j, 0)), st, st],
        out_shape=[
            jax.ShapeDtypeStruct((b, t, MIX_W), F32),
            jax.ShapeDtypeStruct((b, t, MIX_W), F32),
            jax.ShapeDtypeStruct((b, SUBLANES, MIX_W), F32),
            jax.ShapeDtypeStruct((b, SUBLANES, MIX_W), F32),
        ],
        scratch_shapes=[
            pltpu.VMEM((tt + SUBLANES, MIX_W), F32),
            pltpu.VMEM((tt + SUBLANES, MIX_W), F32),
            pltpu.VMEM((tt, MIX_W), F32),
            pltpu.VMEM((tt, MIX_W), F32),
            pltpu.VMEM((SUBLANES, MIX_W), F32),
        ],
        compiler_params=_cparams(("parallel", "arbitrary")),
        name="cd_prompt",
    )(zb3, zb3, zb3, zb3, zb3, scw, lcw, lcb, wa, ba, wx, bx, lam)


def _merge_body(x_ref, oa_ref, ob_ref, oc_ref, od_ref, g0_ref, g1_ref, g2_ref, g3_ref, wbr_ref, wo_ref, o_ref):
    y = None
    for nbr, (br, g_ref) in enumerate(zip((oa_ref, ob_ref, oc_ref, od_ref), (g0_ref, g1_ref, g2_ref, g3_ref))):
        yb = _dot(br[...], wbr_ref[nbr]) * g_ref[...].astype(F32)
        y = yb if y is None else y + yb
    o_ref[...] = x_ref[...] + _dot(y, wo_ref[...])


def _merge(x, oa, ob, oc, od, gates, wbr, wo):
    m = x.shape[0]
    tm = min(m, 512)
    row = lambda w: pl.BlockSpec((tm, w), lambda i: (i, 0))
    gate_specs = [pl.BlockSpec((tm, D_MODEL), lambda i, n=n: (i, n)) for n in range(N_BRANCH)]
    return pl.pallas_call(
        _merge_body,
        grid=(m // tm,),
        in_specs=[row(D_MODEL)] + [row(MIX_W)] * N_BRANCH + gate_specs +
                 [pl.BlockSpec((N_BRANCH, MIX_W, D_MODEL), lambda i: (0, 0, 0)),
                  pl.BlockSpec((D_MODEL, D_MODEL), lambda i: (0, 0))],
        out_specs=row(D_MODEL),
        out_shape=jax.ShapeDtypeStruct((m, D_MODEL), F32),
        compiler_params=_cparams(("parallel",)),
        name="merge",
    )(x, oa, ob, oc, od, *([gates] * N_BRANCH), wbr, wo)


def _spw_body(za_ref, zb_ref, gcs_ref, scs_ref, lcs_ref, lh_ref, gcw_ref, scw_ref, lcw_ref, lcb_ref, wa_ref,
              ba_ref, wx_ref, bx_ref, lam_ref, act_ref, gcs_o, oc_ref, scs_o, od_ref, lh_o, lcs_o):
    w = A_QKV_W
    x = za_ref[:, 0:w]
    y = gcw_ref[GDN_CONV - 1:GDN_CONV, :] * x
    for i in range(GDN_CONV - 1):
        y = y + gcw_ref[i:i + 1, :] * gcs_ref[:, i * w:(i + 1) * w]
    act_ref[...] = jax.nn.silu(y)
    gcs_o[:, 0:(GDN_CONV - 2) * w] = gcs_ref[:, w:(GDN_CONV - 1) * w]
    gcs_o[:, (GDN_CONV - 2) * w:(GDN_CONV - 1) * w] = x

    w = MIX_W
    gate_b = zb_ref[:, ZB_C:ZB_C + w]
    ci = zb_ref[:, ZB_C + w:ZB_C + 2 * w] * zb_ref[:, ZB_C + 2 * w:ZB_C + 3 * w]
    u = scw_ref[SC_CONV - 1:SC_CONV, :] * ci
    for i in range(SC_CONV - 1):
        u = u + scw_ref[i:i + 1, :] * scs_ref[:, i * w:(i + 1) * w]
    oc_ref[...] = gate_b * u
    scs_o[:, 0:(SC_CONV - 2) * w] = scs_ref[:, w:(SC_CONV - 1) * w]
    scs_o[:, (SC_CONV - 2) * w:(SC_CONV - 1) * w] = ci

    xd = zb_ref[:, ZB_D:ZB_D + w]
    gate_d = zb_ref[:, ZB_D + w:ZB_D + 2 * w]
    xc = lcw_ref[LRU_CONV - 1:LRU_CONV, :] * xd
    for i in range(LRU_CONV - 1):
        xc = xc + lcw_ref[i:i + 1, :] * lcs_ref[:, i * w:(i + 1) * w]
    xc = xc + lcb_ref[...]
    a, bt = _lru_gates(xc, wa_ref, ba_ref, wx_ref, bx_ref, lam_ref)
    hnew = a * lh_ref[...] + bt
    lh_o[...] = hnew
    od_ref[...] = hnew * jax.nn.gelu(gate_d)
    lcs_o[:, 0:(LRU_CONV - 2) * w] = lcs_ref[:, w:(LRU_CONV - 1) * w]
    lcs_o[:, (LRU_CONV - 2) * w:(LRU_CONV - 1) * w] = xd


def _sample_pointwise(za, zb, gcs, scs, lcs, lh, gcw, scw, lcw, lcb, wa, ba, wx, bx, lam):
    bd = za.shape[0]
    shapes = [(bd, A_QKV_W), gcs.shape, (bd, MIX_W), scs.shape, (bd, MIX_W), lh.shape, lcs.shape]
    return pl.pallas_call(
        _spw_body,
        out_shape=[jax.ShapeDtypeStruct(s, F32) for s in shapes],
        compiler_params=pltpu.CompilerParams(vmem_limit_bytes=VMEM_LIMIT),
        name="sample_pointwise",
    )(za, zb, gcs, scs, lcs, lh, gcw, scw, lcw, lcb, wa, ba, wx, bx, lam)


def _col(row, diag):
    return jnp.sum(jnp.where(diag, row, 0.0), axis=1, keepdims=True)


def _sstep_body(act_ref, za_ref, zb_ref, ba_ref, s_ref, alog_ref, dtb_ref, nw_ref,
                k0_ref, v0_ref, k1_ref, v1_ref, k2_ref, v2_ref, bb_ref, bn_ref, sh0_ref, sh1_ref, sh2_ref,
                oa_ref, ob_ref, s_out_ref, w0_ref, w1_ref, w2_ref):
    del sh0_ref, sh1_ref, sh2_ref
    e = GDN_DK
    for g, w_ref in enumerate((w0_ref, w1_ref, w2_ref)):
        for h in range(DIL_GROUP_HEADS):
            c0 = g * MIX_W + h * e
            w_ref[0, 0, h:h + 1, :] = zb_ref[:, ZB_K + c0:ZB_K + c0 + e]
            w_ref[1, 0, h:h + 1, :] = zb_ref[:, ZB_V + c0:ZB_V + c0 + e]

    ri = lax.broadcasted_iota(jnp.int32, (e, e), 0)
    ci = lax.broadcasted_iota(jnp.int32, (e, e), 1)
    diag = ri == ci

    ba = ba_ref[...]
    beta_all = jax.nn.sigmoid(ba)
    g_all = -jnp.exp(alog_ref[...]) * _softplus(ba + dtb_ref[...])
    hd = range(GDN_HEADS)
    q = [act_ref[:, h * e:(h + 1) * e] for h in hd]
    k = [act_ref[:, GDN_HEADS * e + h * e:GDN_HEADS * e + (h + 1) * e] for h in hd]
    v = [act_ref[:, 2 * GDN_HEADS * e + h * GDN_DV:2 * GDN_HEADS * e + (h + 1) * GDN_DV] for h in hd]
    q = [q[h] * lax.rsqrt(jnp.sum(q[h] * q[h], axis=-1, keepdims=True) + NORM_EPS) * (e ** -0.5) for h in hd]
    k = [k[h] * lax.rsqrt(jnp.sum(k[h] * k[h], axis=-1, keepdims=True) + NORM_EPS) for h in hd]
    beta = [beta_all[:, h:h + 1] for h in hd]
    eg = [jnp.exp(g_all[:, GDN_HEADS + h:GDN_HEADS + h + 1]) for h in hd]
    s = [s_ref[h] for h in hd]
    kcol = [_col(k[h], diag) for h in hd]
    qcol = [_col(q[h], diag) for h in hd]
    ks = [jnp.sum(kcol[h] * s[h], axis=0, keepdims=True) for h in hd]
    qs = [jnp.sum(qcol[h] * s[h], axis=0, keepdims=True) for h in hd]
    v_new = [beta[h] * v[h] - (beta[h] * eg[h]) * ks[h] for h in hd]
    qk = [jnp.sum(q[h] * k[h], axis=-1, keepdims=True) for h in hd]
    for h in hd:
        o = eg[h] * qs[h] + qk[h] * v_new[h]
        s_out_ref[h] = s[h] * eg[h] + kcol[h] * v_new[h]
        hv = slice(h * GDN_DV, (h + 1) * GDN_DV)
        oa_ref[:, hv] = _rms(o, nw_ref[...]) * jax.nn.silu(za_ref[:, A_QKV_W + h * GDN_DV:A_QKV_W + (h + 1) * GDN_DV])

    kv = ((k0_ref, v0_ref), (k1_ref, v1_ref), (k2_ref, v2_ref))
    pairs = [(h, g) for h in range(DIL_GROUP_HEADS) for g in range(DIL_GROUPS)]
    c0 = {(h, g): g * MIX_W + h * e for h, g in pairs}
    q = {p: zb_ref[:, ZB_Q + c0[p]:ZB_Q + c0[p] + e] for p in pairs}
    kn = {p: zb_ref[:, ZB_K + c0[p]:ZB_K + c0[p] + e] for p in pairs}
    vn = {p: zb_ref[:, ZB_V + c0[p]:ZB_V + c0[p] + e] for p in pairs}
    sb = {(h, g): jnp.sum(kv[g][0][:, h, :] * q[(h, g)], axis=-1, keepdims=True) for h, g in pairs}
    sn = {p: jnp.sum(kn[p] * q[p], axis=-1, keepdims=True) for p in pairs}
    for h, g in pairs:
        col = g * DIL_GROUP_HEADS + h
        sb[(h, g)] = sb[(h, g)] * (e ** -0.5) + bb_ref[:, col:col + 1]
        sn[(h, g)] = sn[(h, g)] * (e ** -0.5) + bn_ref[:, col:col + 1]
    m = {p: jnp.maximum(jnp.max(sb[p], axis=0, keepdims=True), sn[p]) for p in pairs}
    pb = {p: jnp.exp(sb[p] - m[p]) for p in pairs}
    pn = {p: jnp.exp(sn[p] - m[p]) for p in pairs}
    l = {p: jnp.sum(pb[p], axis=0, keepdims=True) + pn[p] for p in pairs}
    out = {(h, g): (jnp.sum(pb[(h, g)] * kv[g][1][:, h, :], axis=0, keepdims=True) + pn[(h, g)] * vn[(h, g)]) / l[(h, g)]
           for h, g in pairs}
    lse = {p: m[p] + jnp.log(l[p]) for p in pairs}
    for h in range(DIL_GROUP_HEADS):
        lses = [lse[(h, g)] for g in range(DIL_GROUPS)]
        mm = functools.reduce(jnp.maximum, lses)
        es = [jnp.exp(x - mm) for x in lses]
        den = functools.reduce(lambda p, q_: p + q_, es)
        ob_ref[:, h * e:(h + 1) * e] = functools.reduce(
            lambda p, q_: p + q_, [(es[g] / den) * out[(h, g)] for g in range(DIL_GROUPS)])


def _sample_step(act3, za3, zb3, ba3, s_in, alog_row, dtb_row, nw, caches, shifted, layer, bias_buf, bias_new):
    bd = act3.shape[0]
    n_fixed_in = 16
    win_specs = [pl.BlockSpec((None, None, 2, 1, DIL_GROUP_HEADS, DIL_HEAD_DIM),
                              lambda i, rows=buf.shape[3]: (layer, i, 0, rows - 1, 0, 0)) for buf in shifted]

    def vec(w):
        return pl.BlockSpec((None, 1, w), lambda i: (i, 0, 0))

    def full(shape):
        return pl.BlockSpec(shape, lambda i: (0,) * len(shape))

    cache_specs, cache_args = [], []
    for cch in caches:
        for kvi in range(2):
            cache_specs.append(pl.BlockSpec((None, None, None, DIL_BLOCK, None, DIL_GROUP_HEADS, DIL_HEAD_DIM),
                                            lambda i, kvi=kvi: (layer, i, kvi, 0, 0, 0, 0)))
            cache_args.append(cch)
    st = pl.BlockSpec((None, GDN_HEADS, GDN_DK, GDN_DV), lambda i: (i, 0, 0, 0))
    return pl.pallas_call(
        _sstep_body,
        grid=(bd,),
        in_specs=[vec(A_QKV_W), vec(ZA_W), vec(ZB_W), vec(BA_W), st,
                  full((1, BA_W)), full((1, BA_W)), full((1, GDN_DV))] + cache_specs +
                 [full((DIL_BLOCK, DIL_HEADS)), full((1, DIL_HEADS))] +
                 [pl.BlockSpec(memory_space=pl.ANY)] * DIL_GROUPS,
        out_specs=[vec(MIX_W), vec(MIX_W), st] + win_specs,
        out_shape=[jax.ShapeDtypeStruct((bd, 1, MIX_W), F32), jax.ShapeDtypeStruct((bd, 1, MIX_W), F32),
                   jax.ShapeDtypeStruct(s_in.shape, F32)] + [jax.ShapeDtypeStruct(buf.shape, F32) for buf in shifted],
        input_output_aliases={n_fixed_in + g: 3 + g for g in range(DIL_GROUPS)},
        compiler_params=_cparams(("parallel",)),
        name="sample_step",
    )(act3, za3, zb3, ba3, s_in, alog_row, dtb_row, nw, *cache_args, bias_buf, bias_new, *shifted)


def _shift_body(a_ref, b_ref, o_ref, *, rb):
    o_ref[:, pl.ds(0, rb - 1)] = a_ref[:, pl.ds(1, rb - 1)]
    o_ref[:, pl.ds(rb - 1, 1)] = b_ref[...]


def _shift_cache(cache):
    depth, bd, _, rows, nh, e = cache.shape
    rb = min(rows, 512)
    return pl.pallas_call(
        functools.partial(_shift_body, rb=rb),
        grid=(depth, bd, rows // rb),
        in_specs=[
            pl.BlockSpec((None, None, 2, rb, nh, e), lambda l, i, j: (l, i, 0, j, 0, 0)),
            pl.BlockSpec((None, None, 2, 1, nh, e), lambda l, i, j: (l, i, 0, jnp.minimum((j + 1) * rb, rows - 1), 0, 0)),
        ],
        out_specs=pl.BlockSpec((None, None, 2, rb, nh, e), lambda l, i, j: (l, i, 0, j, 0, 0)),
        out_shape=jax.ShapeDtypeStruct(cache.shape, F32),
        compiler_params=_cparams(("parallel", "parallel", "parallel")),
        name="shift_cache",
    )(cache, cache)


def _t5_bucket(dist):
    exact = REL_BUCKETS // 2
    d = np.maximum(dist, 1).astype(np.float32)
    large = exact + (np.log(d / exact) / math.log(REL_MAX_DIST / exact) * (REL_BUCKETS - exact)).astype(np.int32)
    return np.where(dist < exact, dist, np.minimum(large, REL_BUCKETS - 1)).astype(np.int32)


def _prompt_bias(rel_bias):
    blk = DIL_BLOCK
    period = 4 * blk
    out = []
    for gi, (window, dilation) in enumerate(DIL_PATTERNS):
        n_off = window // dilation
        tab = rel_bias[:, gi * DIL_GROUP_HEADS:(gi + 1) * DIL_GROUP_HEADS]
        vals = tab[_t5_bucket(np.arange(n_off + 1) * dilation)].T.astype(F32)
        w = jnp.full((DIL_GROUP_HEADS, period), -jnp.inf, F32)
        w = lax.dynamic_update_slice(w, vals, (0, blk - 1))
        r = jnp.tile(w, (1, blk + 1))[:, :blk * (period + 1)].reshape(DIL_GROUP_HEADS, blk, period + 1)
        out.append(r[:, :, 0:2 * blk][:, :, ::-1])
    return jnp.stack(out, axis=0)


def _step_bias(rel_bias):
    cols_buf, cols_new = [], []
    for gi, (window, dilation) in enumerate(DIL_PATTERNS):
        n_off = window // dilation
        tab = rel_bias[:, gi * DIL_GROUP_HEADS:(gi + 1) * DIL_GROUP_HEADS]
        j = n_off - np.arange(n_off)
        cols_buf.append(tab[_t5_bucket(j * dilation)])
        cols_new.append(tab[_t5_bucket(np.zeros((1,), np.int64))])
    return jnp.concatenate(cols_buf, axis=1).astype(F32), jnp.concatenate(cols_new, axis=1).astype(F32)


def _block_diag(w):
    n, c, _ = w.shape
    eye = jnp.eye(n, dtype=w.dtype)
    return (eye[:, None, :, None] * w[:, :, None, :]).reshape(n * c, n * c)


def _layer_weights(p, l, final_norm):
    w_in = p['w_in'][l]
    b0, c0 = A_W, A_W + B_W
    zg0 = A_QKV_W
    ba0 = A_QKV_W + GDN_HEADS * GDN_DV
    w_all = jnp.concatenate([w_in[:, 0:ba0], w_in[:, b0:], p['w_gate'][l]], axis=1).astype(BF16)
    b_all = p['b_gate'][l][None]
    w_ba = jnp.pad(w_in[:, ba0:b0], ((0, 0), (0, BA_W - 2 * GDN_HEADS))).astype(BF16)
    pad_row = lambda v: jnp.pad(v, (GDN_HEADS, BA_W - 2 * GDN_HEADS))[None]
    row = lambda v: v[None].astype(F32)
    del zg0, c0
    return dict(
        fnw=row(final_norm), last=(l == DEPTH - 1),
        n1=row(p['norm_ffn1'][l]), gu1=p['ffn1_w_gu'][l].astype(BF16), dn1=p['ffn1_w_down'][l].astype(BF16),
        nm=row(p['norm_mix'][l]), w_all=w_all, b_all=b_all, w_ba=w_ba,
        gcw=p['gdn_conv_w'][l], alog=pad_row(p['gdn_a_log'][l]), dtb=pad_row(p['gdn_dt_bias'][l]),
        gnw=row(p['gdn_norm_w'][l]), scw=p['sc_conv_w'][l], lcw=p['lru_conv_w'][l], lcb=row(p['lru_conv_b'][l]),
        wa=_block_diag(p['lru_wa'][l]).astype(BF16), ba=row(p['lru_ba'][l]),
        wx=_block_diag(p['lru_wx'][l]).astype(BF16), bx=row(p['lru_bx'][l]), lam=row(p['lru_lambda'][l]),
        wbr=p['w_branch'][l].astype(BF16), wo=p['w_o'][l].astype(BF16),
        n2=row(p['norm_ffn2'][l]), gu2=p['ffn2_w_gu'][l].astype(BF16), dn2=p['ffn2_w_down'][l].astype(BF16),
    )


def _prompt_layer(x, w, bias, b, t):
    m = b * t
    x = _ffn(x, w['n1'], w['gu1'], w['dn1'], w['fnw'])
    za, zb, gates, ba = _proj(x, w['nm'], w['w_all'], w['b_all'], w['w_ba'])
    za3 = za.reshape(b, t, ZA_W)
    zb3 = zb.reshape(b, t, ZB_W)
    o_a, s_gdn = _gdn_prompt(za3, ba.reshape(b, t, BA_W), w['gcw'], w['alog'], w['dtb'], w['gnw'])
    o_b = _dil_prompt(zb3, bias)
    o_c, o_d, sc_st, lru_h = _cd_prompt(zb3, w['scw'], w['lcw'], w['lcb'], w['wa'], w['ba'], w['wx'], w['bx'], w['lam'])
    x = _merge(x, o_a.reshape(m, MIX_W), o_b.reshape(m, MIX_W), o_c.reshape(m, MIX_W), o_d.reshape(m, MIX_W),
               gates, w['wbr'], w['wo'])
    x = _ffn(x, w['n2'], w['gu2'], w['dn2'], w['fnw'], final=w['last'])
    gdn_conv = za3[:, t - (GDN_CONV - 1):, 0:A_QKV_W]
    bufs = []
    for gi, (window, _) in enumerate(DIL_PATTERNS):
        rows = min(window, t)
        k = zb3[:, t - rows:, ZB_K + gi * MIX_W:ZB_K + (gi + 1) * MIX_W]
        v = zb3[:, t - rows:, ZB_V + gi * MIX_W:ZB_V + (gi + 1) * MIX_W]
        bufs.append(jnp.stack([k, v], axis=1).reshape(b, 2, rows, DIL_GROUP_HEADS, DIL_HEAD_DIM))
    sc_conv = sc_st[:, SUBLANES - (SC_CONV - 1):]
    lru_conv = zb3[:, t - (LRU_CONV - 1):, ZB_D:ZB_D + MIX_W]
    return x, (s_gdn, gdn_conv, bufs[0], bufs[1], bufs[2], sc_conv, lru_h[:, 0], lru_conv)


def _sample_layer(x, w, l, st, shifted, step_bias):
    bd = x.shape[0]
    s_gdn, gdn_conv, caches, sc_conv, lru_h, lru_conv = st
    x = _ffn(x, w['n1'], w['gu1'], w['dn1'], w['fnw'])
    za, zb, gates, ba = _proj(x, w['nm'], w['w_all'], w['b_all'], w['w_ba'])
    act, gdn_conv_new, o_c, sc_new, o_d, lru_h_new, lru_conv_new = _sample_pointwise(
        za, zb, gdn_conv.reshape(bd, -1), sc_conv.reshape(bd, -1), lru_conv.reshape(bd, -1), lru_h,
        w['gcw'], w['scw'], w['lcw'], w['lcb'], w['wa'], w['ba'], w['wx'], w['bx'], w['lam'])
    zb3 = zb.reshape(bd, 1, ZB_W)
    o_a, o_b, s_new, *shifted = _sample_step(act.reshape(bd, 1, A_QKV_W), za.reshape(bd, 1, ZA_W), zb3,
                                             ba.reshape(bd, 1, BA_W), s_gdn, w['alog'], w['dtb'], w['gnw'], caches,
                                             shifted, l, *step_bias)
    x = _merge(x, o_a.reshape(bd, MIX_W), o_b.reshape(bd, MIX_W), o_c, o_d, gates, w['wbr'], w['wo'])
    x = _ffn(x, w['n2'], w['gu2'], w['dn2'], w['fnw'], final=w['last'])
    new = (s_new, gdn_conv_new.reshape(gdn_conv.shape), sc_new.reshape(sc_conv.shape), lru_h_new,
           lru_conv_new.reshape(lru_conv.shape))
    return x, new, shifted


def kernel(x_prompt, x_sample, state_gdn, state_gdn_conv, cache_dil_w128, cache_dil_w512, cache_dil_w2048,
           state_sc_conv, state_lru, state_lru_conv, norm_ffn1, ffn1_w_gu, ffn1_w_down, norm_mix, w_in,
           gdn_conv_w, gdn_a_log, gdn_dt_bias, gdn_norm_w, rel_bias, sc_conv_w, lru_conv_w, lru_conv_b,
           lru_wa, lru_ba, lru_wx, lru_bx, lru_lambda, w_gate, b_gate, w_branch, w_o, norm_ffn2, ffn2_w_gu,
           ffn2_w_down, final_norm):
    p = dict(norm_ffn1=norm_ffn1, ffn1_w_gu=ffn1_w_gu, ffn1_w_down=ffn1_w_down, norm_mix=norm_mix, w_in=w_in,
             gdn_conv_w=gdn_conv_w, gdn_a_log=gdn_a_log, gdn_dt_bias=gdn_dt_bias, gdn_norm_w=gdn_norm_w,
             sc_conv_w=sc_conv_w, lru_conv_w=lru_conv_w, lru_conv_b=lru_conv_b, lru_wa=lru_wa, lru_ba=lru_ba,
             lru_wx=lru_wx, lru_bx=lru_bx, lru_lambda=lru_lambda, w_gate=w_gate, b_gate=b_gate,
             w_branch=w_branch, w_o=w_o, norm_ffn2=norm_ffn2, ffn2_w_gu=ffn2_w_gu, ffn2_w_down=ffn2_w_down)
    weights = [_layer_weights(p, l, final_norm) for l in range(DEPTH)]
    b, t, _ = x_prompt.shape
    bd = x_sample.shape[0]
    bias = _prompt_bias(rel_bias)
    step_bias = _step_bias(rel_bias)

    x = x_prompt.reshape(b * t, D_MODEL)
    p_states = []
    for l in range(DEPTH):
        x, new = _prompt_layer(x, weights[l], bias, b, t)
        p_states.append(new)
    y_prompt = x.reshape(b, t, D_MODEL)
    p_out = tuple(jnp.stack(zs, axis=0) for zs in zip(*p_states))

    caches = (cache_dil_w128, cache_dil_w512, cache_dil_w2048)
    shifted = [_shift_cache(c) for c in caches]
    strided = tuple(c.reshape(DEPTH, bd, 2, c.shape[3] // d, d, DIL_GROUP_HEADS, DIL_HEAD_DIM)
                    for c, (_, d) in zip(caches, DIL_PATTERNS))
    x = x_sample.reshape(bd, D_MODEL)
    s_states = []
    for l in range(DEPTH):
        st = (state_gdn[l], state_gdn_conv[l], strided, state_sc_conv[l], state_lru[l], state_lru_conv[l])
        x, new, shifted = _sample_layer(x, weights[l], l, st, shifted, step_bias)
        s_states.append(new)
    y_sample = x.reshape(bd, 1, D_MODEL)
    s_gdn, s_gdn_conv, s_sc_conv, s_lru, s_lru_conv = (jnp.stack(zs, axis=0) for zs in zip(*s_states))

    p_gdn, p_gdn_conv, p_w128, p_w512, p_w2048, p_sc_conv, p_lru, p_lru_conv = p_out
    return (y_prompt, y_sample, p_gdn, s_gdn, p_gdn_conv, s_gdn_conv, p_w128, shifted[0], p_w512, shifted[1],
            p_w2048, shifted[2], p_sc_conv, s_sc_conv, p_lru, s_lru, p_lru_conv, s_lru_conv)
```

```python
import functools
import math

import numpy as np
import jax
import jax.numpy as jnp
from jax import lax
from jax.experimental import pallas as pl
from jax.experimental.pallas import tpu as pltpu

D_MODEL = 1024
DEPTH = 4
MIX_W = D_MODEL // 2
D_FF = 2816
NORM_EPS = 1e-6
N_BRANCH = 4
GDN_HEADS = 4
GDN_DK = 128
GDN_DV = MIX_W // GDN_HEADS
GDN_CONV = 4
GDN_CHUNK = 64
DIL_PATTERNS = ((128, 1), (512, 4), (2048, 16))
DIL_GROUPS = len(DIL_PATTERNS)
DIL_GROUP_HEADS = 4
DIL_HEAD_DIM = MIX_W // DIL_GROUP_HEADS
DIL_HEADS = DIL_GROUPS * DIL_GROUP_HEADS
DIL_BLOCK = 128
REL_BUCKETS = 32
REL_MAX_DIST = 2048
SC_CONV = 3
LRU_BLOCKS = 8
LRU_BLOCK_W = MIX_W // LRU_BLOCKS
LRU_CONV = 4
LRU_C = 8.0
A_QKV_W = GDN_HEADS * (2 * GDN_DK + GDN_DV)
A_W = A_QKV_W + GDN_HEADS * GDN_DV + 2 * GDN_HEADS
B_W = 3 * DIL_HEADS * DIL_HEAD_DIM
C_W = 3 * MIX_W
D_W = 2 * MIX_W

LANES = 128
SUBLANES = 8
VMEM_LIMIT = 56 * 1024 * 1024

ZA_W = A_QKV_W + MIX_W
ZB_Q, ZB_K, ZB_V = 0, B_W // 3, 2 * B_W // 3
ZB_C = B_W
ZB_D = B_W + C_W
ZB_W = B_W + C_W + D_W
G_W = N_BRANCH * D_MODEL
PROJ_TN = 1024
BA_W = LANES
DIL_TB = DIL_BLOCK * max(d for _, d in DIL_PATTERNS)
DIL_UNROLL = 8

F32 = jnp.float32
BF16 = jnp.bfloat16
NT_DIMS = (((1,), (1,)), ((), ()))
TN_DIMS = (((0,), (0,)), ((), ()))


def _cparams(sem):
    return pltpu.CompilerParams(dimension_semantics=sem, vmem_limit_bytes=VMEM_LIMIT)


def _rms(x, w):
    return x * lax.rsqrt(jnp.mean(x * x, axis=-1, keepdims=True) + NORM_EPS) * w


def _softplus(x):
    return jnp.maximum(x, 0.0) + jnp.log1p(jnp.exp(-jnp.abs(x)))


def _causal_conv(xb, w_ref, tt):
    taps = w_ref.shape[0]
    y = None
    for i in range(taps):
        back = taps - 1 - i
        xs = xb if back == 0 else pltpu.roll(xb, back, 0)
        term = w_ref[i:i + 1, :] * xs[SUBLANES:SUBLANES + tt]
        y = term if y is None else y + term
    return y


def _dot(a, b):
    return jnp.dot(a.astype(BF16), b.astype(BF16), preferred_element_type=F32)


def _dot_nt(a, b):
    return lax.dot_general(a.astype(BF16), b.astype(BF16), NT_DIMS, preferred_element_type=F32)


def _dot_tn(a, b):
    return lax.dot_general(a.astype(BF16), b.astype(BF16), TN_DIMS, preferred_element_type=F32)


def _ffn_body(x_ref, nw_ref, wg_ref, wu_ref, wd_ref, fw_ref, o_ref, h_ref, acc_ref, *, final):
    j = pl.program_id(1)

    @pl.when(j == 0)
    def _():
        h_ref[...] = _rms(x_ref[...], nw_ref[...]).astype(BF16)
        acc_ref[...] = jnp.zeros_like(acc_ref)

    h = h_ref[...]
    g = _dot(h, wg_ref[...])
    u = _dot(h, wu_ref[...])
    acc_ref[...] += _dot(jax.nn.silu(g) * u, wd_ref[...])

    @pl.when(j == pl.num_programs(1) - 1)
    def _():
        y = x_ref[...] + 0.5 * acc_ref[...]
        o_ref[...] = _rms(y, fw_ref[...]) if final else y


def _ffn(x, nw, w_gu, w_down, fw, final=False):
    m = x.shape[0]
    tm = min(m, 1024)
    nf = 2
    tf = D_FF // nf
    return pl.pallas_call(
        functools.partial(_ffn_body, final=final),
        grid=(m // tm, nf),
        in_specs=[
            pl.BlockSpec((tm, D_MODEL), lambda i, j: (i, 0)),
            pl.BlockSpec((1, D_MODEL), lambda i, j: (0, 0)),
            pl.BlockSpec((D_MODEL, tf), lambda i, j: (0, j)),
            pl.BlockSpec((D_MODEL, tf), lambda i, j: (0, nf + j)),
            pl.BlockSpec((tf, D_MODEL), lambda i, j: (j, 0)),
            pl.BlockSpec((1, D_MODEL), lambda i, j: (0, 0)),
        ],
        out_specs=pl.BlockSpec((tm, D_MODEL), lambda i, j: (i, 0)),
        out_shape=jax.ShapeDtypeStruct((m, D_MODEL), F32),
        scratch_shapes=[pltpu.VMEM((tm, D_MODEL), BF16), pltpu.VMEM((tm, D_MODEL), F32)],
        compiler_params=_cparams(("parallel", "arbitrary")),
        name="ffn",
    )(x, nw, w_gu, w_gu, w_down, fw)


def _proj_body(x_ref, nw_ref, w_ref, b_ref, wba_ref, za_ref, zb_ref, g_ref, ba_ref, h_ref, *, ja, jb):
    j = pl.program_id(1)

    @pl.when(j == 0)
    def _():
        h = _rms(x_ref[...], nw_ref[...]).astype(BF16)
        h_ref[...] = h
        ba_ref[...] = _dot(h, wba_ref[...])

    @pl.when(j < ja)
    def _():
        za_ref[...] = _dot(h_ref[...], w_ref[...]).astype(za_ref.dtype)

    @pl.when((j >= ja) & (j < jb))
    def _():
        zb_ref[...] = _dot(h_ref[...], w_ref[...]).astype(zb_ref.dtype)

    @pl.when(j >= jb)
    def _():
        g_ref[...] = jax.nn.sigmoid(_dot(h_ref[...], w_ref[...]) + b_ref[...]).astype(BF16)


def _proj(x, nw, w, b, w_ba, act_dtype):
    m = x.shape[0]
    tm = min(m, 1024)
    tn = PROJ_TN
    ja, jb = ZA_W // tn, (ZA_W + ZB_W) // tn
    nj = (ZA_W + ZB_W + G_W) // tn
    return pl.pallas_call(
        functools.partial(_proj_body, ja=ja, jb=jb),
        grid=(m // tm, nj),
        in_specs=[
            pl.BlockSpec((tm, D_MODEL), lambda i, j: (i, 0)),
            pl.BlockSpec((1, D_MODEL), lambda i, j: (0, 0)),
            pl.BlockSpec((D_MODEL, tn), lambda i, j: (0, j)),
            pl.BlockSpec((1, tn), lambda i, j: (0, jnp.maximum(j - jb, 0))),
            pl.BlockSpec((D_MODEL, BA_W), lambda i, j: (0, 0)),
        ],
        out_specs=[
            pl.BlockSpec((tm, tn), lambda i, j: (i, jnp.minimum(j, ja - 1))),
            pl.BlockSpec((tm, tn), lambda i, j: (i, jnp.clip(j - ja, 0, jb - ja - 1))),
            pl.BlockSpec((tm, tn), lambda i, j: (i, jnp.maximum(j - jb, 0))),
            pl.BlockSpec((tm, BA_W), lambda i, j: (i, 0)),
        ],
        out_shape=[
            jax.ShapeDtypeStruct((m, ZA_W), act_dtype),
            jax.ShapeDtypeStruct((m, ZB_W), act_dtype),
            jax.ShapeDtypeStruct((m, G_W), BF16),
            jax.ShapeDtypeStruct((m, BA_W), F32),
        ],
        scratch_shapes=[pltpu.VMEM((tm, D_MODEL), BF16)],
        compiler_params=_cparams(("parallel", "arbitrary")),
        name="proj",
    )(x, nw, w, b, w_ba)


def _gdn_body(qkv_ref, zg_ref, ba_ref, cw_ref, alog_ref, dtb_ref, nw_ref, o_ref, s_out_ref,
              xbuf, act, beta_s, gc_s, u_s, wq_s, at_s, kd_s, s_s, *, tt, nb):
    t = pl.program_id(0)
    c = GDN_CHUNK
    nchunk = tt // c

    @pl.when(t == 0)
    def _():
        for b in range(nb):
            xbuf[b, pl.ds(0, SUBLANES), :] = jnp.zeros((SUBLANES, A_QKV_W), F32)
        s_s[...] = jnp.zeros_like(s_s)

    rowmod = lax.broadcasted_iota(jnp.int32, (tt, BA_W), 0) & (c - 1)
    for b in range(nb):
        xbuf[b, pl.ds(SUBLANES, tt), :] = qkv_ref[b].astype(F32)
        act[b] = jax.nn.silu(_causal_conv(xbuf[b], cw_ref, tt))
        xbuf[b, pl.ds(0, SUBLANES), :] = xbuf[b, pl.ds(tt, SUBLANES), :]

        ba = ba_ref[b]
        beta_s[b] = jax.nn.sigmoid(ba)
        g = -jnp.exp(alog_ref[...]) * _softplus(ba + dtb_ref[...])
        sh = 1
        while sh < c:
            g = g + jnp.where(rowmod >= sh, pltpu.roll(g, sh, 0), 0.0)
            sh *= 2
        gc_s[b] = g

    ri = lax.broadcasted_iota(jnp.int32, (c, c), 0)
    ci = lax.broadcasted_iota(jnp.int32, (c, c), 1)
    causal = ri >= ci
    strict = ri > ci
    diag = ri == ci
    eye = jnp.where(diag, 1.0, 0.0).astype(F32)

    heads = range(GDN_HEADS)

    def prep(i, carry):
        chains = [(b, 2 * i + j, h) for b in range(nb) for j in range(2) for h in heads]
        n = len(chains)
        rows_l, gcol_l, decay_l, q_l, k_l, kb_l, vb_l = [], [], [], [], [], [], []
        for b, ch, h in chains:
            rows = pl.ds(pl.multiple_of(ch * c, c), c)
            gcol = gc_s[b, rows, GDN_HEADS + h:GDN_HEADS + h + 1]
            grow = jnp.sum(jnp.where(diag, gcol, 0.0), axis=0, keepdims=True)
            decay = jnp.exp(jnp.where(causal, gcol - grow, -jnp.inf))
            q = act[b, rows, h * GDN_DK:(h + 1) * GDN_DK]
            k = act[b, rows, GDN_HEADS * GDN_DK + h * GDN_DK:GDN_HEADS * GDN_DK + (h + 1) * GDN_DK]
            v = act[b, rows, 2 * GDN_HEADS * GDN_DK + h * GDN_DV:2 * GDN_HEADS * GDN_DK + (h + 1) * GDN_DV]
            q = q * lax.rsqrt(jnp.sum(q * q, axis=-1, keepdims=True) + NORM_EPS) * (GDN_DK ** -0.5)
            k = k * lax.rsqrt(jnp.sum(k * k, axis=-1, keepdims=True) + NORM_EPS)
            beta = beta_s[b, rows, h:h + 1]
            rows_l.append(rows)
            gcol_l.append(gcol)
            decay_l.append(decay)
            q_l.append(q)
            k_l.append(k)
            kb_l.append(k * beta)
            vb_l.append(v * beta)
        kq_l = [_dot_nt(jnp.concatenate([kb_l[j], q_l[j]], axis=0), k_l[j]) for j in range(n)]
        for j, (b, ch, h) in enumerate(chains):
            at_s[b, rows_l[j], h * LANES:h * LANES + c] = kq_l[j][c:] * decay_l[j]
        qq = [-jnp.where(strict, kq_l[j][:c] * decay_l[j], 0.0) for j in range(n)]
        yy = [eye + qq[j] for j in range(n)]
        qq = [_dot(qq[j], qq[j]) for j in range(n)]
        for _ in range(int(math.log2(c)) - 2):
            yq = [_dot(jnp.concatenate([yy[j], qq[j]], axis=0), qq[j]) for j in range(n)]
            yy = [yy[j] + yq[j][:c] for j in range(n)]
            qq = [yq[j][c:] for j in range(n)]
        yq = [_dot(yy[j], qq[j]) for j in range(n)]
        tinv = [yy[j] + yq[j] for j in range(n)]
        egc = [jnp.exp(gcol_l[j]) for j in range(n)]
        sol = [_dot(tinv[j], jnp.concatenate([vb_l[j], kb_l[j] * egc[j]], axis=1)) for j in range(n)]
        for j, (b, ch, h) in enumerate(chains):
            hs = slice(h * GDN_DK, (h + 1) * GDN_DK)
            u_s[b, rows_l[j], h * GDN_DV:(h + 1) * GDN_DV] = sol[j][:, :GDN_DV]
            wq_s[b, pl.ds(pl.multiple_of(ch * 2 * c, 2 * c), c), hs] = sol[j][:, GDN_DV:]
            wq_s[b, pl.ds(pl.multiple_of(ch * 2 * c + c, c), c), hs] = q_l[j] * egc[j]
            kd_s[b, rows_l[j], hs] = k_l[j] * jnp.exp(gcol_l[j][c - 1:c, :] - gcol_l[j])
        return carry

    lax.fori_loop(0, nchunk // 2, prep, 0)

    def step(ch, carry):
        rows = pl.ds(pl.multiple_of(ch * c, c), c)
        wq_rows = pl.ds(pl.multiple_of(ch * 2 * c, 2 * c), 2 * c)
        chains = [(b, h) for b in range(nb) for h in heads]
        hs = {h: slice(h * GDN_DK, (h + 1) * GDN_DK) for h in heads}
        hv = {h: slice(h * GDN_DV, (h + 1) * GDN_DV) for h in heads}
        glast = [jnp.exp(gc_s[b, pl.ds(ch * c + c - 1, 1), :]) for b in range(nb)]
        st = [s_s[b, h] for b, h in chains]
        ws = [_dot(wq_s[b, wq_rows, hs[h]], st[j]) for j, (b, h) in enumerate(chains)]
        v_new = [u_s[b, rows, hv[h]] - ws[j][:c] for j, (b, h) in enumerate(chains)]
        av = [_dot(at_s[b, rows, h * LANES:h * LANES + c], v_new[j]) for j, (b, h) in enumerate(chains)]
        kv = [_dot_tn(kd_s[b, rows, hs[h]], v_new[j]) for j, (b, h) in enumerate(chains)]
        for j, (b, h) in enumerate(chains):
            s_s[b, h] = st[j] * glast[b][:, GDN_HEADS + h:GDN_HEADS + h + 1] + kv[j]
            o = _rms(ws[j][c:] + av[j], nw_ref[...]) * jax.nn.silu(zg_ref[b, rows, hv[h]].astype(F32))
            o_ref[b, rows, hv[h]] = o
        return carry

    lax.fori_loop(0, nchunk, step, 0)

    @pl.when(t == pl.num_programs(0) - 1)
    def _():
        s_out_ref[...] = s_s[...]


def _gdn_prompt(za3, ba3, cw, alog_row, dtb_row, nw):
    b, t, _ = za3.shape
    tt = min(t, 256)
    return pl.pallas_call(
        functools.partial(_gdn_body, tt=tt, nb=b),
        grid=(t // tt,),
        in_specs=[
            pl.BlockSpec((b, tt, A_QKV_W), lambda j: (0, j, 0)),
            pl.BlockSpec((b, tt, MIX_W), lambda j: (0, j, A_QKV_W // MIX_W)),
            pl.BlockSpec((b, tt, BA_W), lambda j: (0, j, 0)),
            pl.BlockSpec((GDN_CONV, A_QKV_W), lambda j: (0, 0)),
            pl.BlockSpec((1, BA_W), lambda j: (0, 0)),
            pl.BlockSpec((1, BA_W), lambda j: (0, 0)),
            pl.BlockSpec((1, GDN_DV), lambda j: (0, 0)),
        ],
        out_specs=[
            pl.BlockSpec((b, tt, MIX_W), lambda j: (0, j, 0)),
            pl.BlockSpec((b, GDN_HEADS, GDN_DK, GDN_DV), lambda j: (0, 0, 0, 0)),
        ],
        out_shape=[
            jax.ShapeDtypeStruct((b, t, MIX_W), F32),
            jax.ShapeDtypeStruct((b, GDN_HEADS, GDN_DK, GDN_DV), F32),
        ],
        scratch_shapes=[
            pltpu.VMEM((b, tt + SUBLANES, A_QKV_W), F32),
            pltpu.VMEM((b, tt, A_QKV_W), F32),
            pltpu.VMEM((b, tt, BA_W), F32),
            pltpu.VMEM((b, tt, BA_W), F32),
            pltpu.VMEM((b, tt, MIX_W), F32),
            pltpu.VMEM((b, 2 * tt, MIX_W), F32),
            pltpu.VMEM((b, tt, GDN_HEADS * LANES), F32),
            pltpu.VMEM((b, tt, MIX_W), F32),
            pltpu.VMEM((b, GDN_HEADS, GDN_DK, GDN_DV), F32),
        ],
        compiler_params=_cparams(("arbitrary",)),
        name="gdn_prompt",
    )(za3, za3, ba3, cw, alog_row, dtb_row, nw)


def _dil_body(*refs):
    qkv = refs[:3 * DIL_GROUPS]
    bias_ref, o_ref = refs[3 * DIL_GROUPS], refs[3 * DIL_GROUPS + 1]
    kvbuf = refs[3 * DIL_GROUPS + 2:3 * DIL_GROUPS + 2 + 2 * DIL_GROUPS]
    og, lg, qb = refs[-3], refs[-2], refs[-1]
    n = pl.program_id(2)
    e = DIL_HEAD_DIM
    blk = DIL_BLOCK
    tb = DIL_TB
    for g, (_, d) in enumerate(DIL_PATTERNS):
        q_ref, k_ref, v_ref = qkv[3 * g:3 * g + 3]
        kb, vb = kvbuf[2 * g], kvbuf[2 * g + 1]
        tail = blk * d

        @pl.when(n == 0)
        def _(kb=kb, vb=vb, tail=tail):
            kb[pl.ds(0, tail), :] = jnp.zeros((tail, e), F32)
            vb[pl.ds(0, tail), :] = jnp.zeros((tail, e), F32)

        kb[pl.ds(tail, tb), :] = k_ref[...].astype(F32)
        vb[pl.ds(tail, tb), :] = v_ref[...].astype(F32)
        qb[...] = q_ref[...].astype(F32)
        bp = bias_ref[g, :, 0:blk]
        bc = bias_ref[g, :, blk:2 * blk]
        shift = int(math.log2(d))

        def rows_at(base, d=d):
            return pl.ds(base, blk) if d == 1 else pl.ds(base, blk, stride=d)

        def body(it, carry, d=d, g=g, kb=kb, vb=vb, bp=bp, bc=bc, shift=shift, tail=tail,
                 rows_at=rows_at):
            us = range(DIL_UNROLL)
            idx = [it * DIL_UNROLL + u for u in us]
            sub = [i >> shift for i in idx]
            base = [sub[u] * tail + (idx[u] & (d - 1)) for u in us]
            q = [qb[rows_at(base[u]), :].astype(BF16) for u in us]
            sp = [_dot_nt(q[u], kb[rows_at(base[u]), :]) for u in us]
            sc = [_dot_nt(q[u], kb[rows_at(base[u] + tail), :]) for u in us]
            pp, pc, l, lse = [], [], [], []
            for u in us:
                spu = jnp.where((n > 0) | (sub[u] > 0), sp[u] * (e ** -0.5) + bp, -jnp.inf)
                scu = sc[u] * (e ** -0.5) + bc
                m = jnp.max(jnp.maximum(spu, scu), axis=-1, keepdims=True)
                ppu = jnp.exp(spu - m)
                pcu = jnp.exp(scu - m)
                lu = jnp.sum(ppu + pcu, axis=-1, keepdims=True)
                pp.append(ppu)
                pc.append(pcu)
                l.append(lu)
                lse.append(m + jnp.log(lu))
            op = [_dot(pp[u], vb[rows_at(base[u]), :]) for u in us]
            oc = [_dot(pc[u], vb[rows_at(base[u] + tail), :]) for u in us]
            for u in us:
                og[g, rows_at(base[u]), :] = (op[u] + oc[u]) / l[u]
                lg[g, rows_at(base[u]), :] = jnp.broadcast_to(lse[u], (blk, e))
            return carry

        lax.fori_loop(0, tb // blk // DIL_UNROLL, body, 0)
        kb[pl.ds(0, tail), :] = kb[pl.ds(tb, tail), :]
        vb[pl.ds(0, tail), :] = vb[pl.ds(tb, tail), :]

    lses = [lg[g] for g in range(DIL_GROUPS)]
    m = functools.reduce(jnp.maximum, lses)
    es = [jnp.exp(l - m) for l in lses]
    den = functools.reduce(lambda p, q_: p + q_, es)
    o_ref[...] = functools.reduce(lambda p, q_: p + q_, [(es[g] / den) * og[g] for g in range(DIL_GROUPS)])


def _dil_prompt(zb3, bias):
    b, t, _ = zb3.shape
    tb = DIL_TB
    e = DIL_HEAD_DIM
    specs, args = [], []
    for g in range(DIL_GROUPS):
        for off in (ZB_Q, ZB_K, ZB_V):
            cb = (off + g * MIX_W) // e
            specs.append(pl.BlockSpec((None, tb, e), lambda i, h, n, cb=cb: (i, n, cb + h)))
            args.append(zb3)
    specs.append(pl.BlockSpec((DIL_GROUPS, None, DIL_BLOCK, 2 * DIL_BLOCK), lambda i, h, n: (0, h, 0, 0)))
    scratch = []
    for _, d in DIL_PATTERNS:
        scratch += [pltpu.VMEM((DIL_BLOCK * d + tb, e), F32)] * 2
    scratch += [pltpu.VMEM((DIL_GROUPS, tb, e), F32)] * 2
    scratch += [pltpu.VMEM((tb, e), F32)]
    return pl.pallas_call(
        _dil_body,
        grid=(b, DIL_GROUP_HEADS, t // tb),
        in_specs=specs,
        out_specs=pl.BlockSpec((None, tb, e), lambda i, h, n: (i, n, h)),
        out_shape=jax.ShapeDtypeStruct((b, t, MIX_W), F32),
        scratch_shapes=scratch,
        compiler_params=_cparams(("parallel", "parallel", "arbitrary")),
        name="dil_prompt",
    )(*args, bias)


def _lru_gates(xc, wa_ref, ba_ref, wx_ref, bx_ref, lam_ref):
    xb = xc.astype(BF16)
    r = jax.nn.sigmoid(_dot(xb, wa_ref[...]) + ba_ref[...])
    i = jax.nn.sigmoid(_dot(xb, wx_ref[...]) + bx_ref[...])
    log_a = -LRU_C * r * _softplus(-lam_ref[...])
    a = jnp.exp(log_a)
    bt = jnp.sqrt(-jnp.tanh(log_a) * (jnp.exp(2.0 * log_a) + 1.0)) * i * xc
    return a, bt


def _cd_body(gb_ref, gc_ref, xi_ref, xd_ref, gd_ref, scw_ref, lcw_ref, lcb_ref, wa_ref, ba_ref, wx_ref,
             bx_ref, lam_ref, oc_ref, od_ref, scst_ref, lruh_ref, cbuf, dbuf, a_s, b_s, h_s, *, tt):
    t = pl.program_id(1)

    @pl.when(t == 0)
    def _():
        cbuf[pl.ds(0, SUBLANES), :] = jnp.zeros((SUBLANES, MIX_W), F32)
        dbuf[pl.ds(0, SUBLANES), :] = jnp.zeros((SUBLANES, MIX_W), F32)
        h_s[...] = jnp.zeros_like(h_s)

    cbuf[pl.ds(SUBLANES, tt), :] = gc_ref[...].astype(F32) * xi_ref[...].astype(F32)
    oc_ref[...] = gb_ref[...].astype(F32) * _causal_conv(cbuf[...], scw_ref, tt)
    tail = cbuf[pl.ds(tt, SUBLANES), :]
    scst_ref[...] = tail
    cbuf[pl.ds(0, SUBLANES), :] = tail

    dbuf[pl.ds(SUBLANES, tt), :] = xd_ref[...].astype(F32)
    xc = _causal_conv(dbuf[...], lcw_ref, tt) + lcb_ref[...]
    dbuf[pl.ds(0, SUBLANES), :] = dbuf[pl.ds(tt, SUBLANES), :]
    a, bt = _lru_gates(xc, wa_ref, ba_ref, wx_ref, bx_ref, lam_ref)
    a_s[...] = a
    b_s[...] = bt

    def scan(g, h):
        rows = pl.ds(pl.multiple_of(g * SUBLANES, SUBLANES), SUBLANES)
        a8 = a_s[rows, :]
        b8 = b_s[rows, :]
        out = []
        for r in range(SUBLANES):
            h = a8[r:r + 1, :] * h + b8[r:r + 1, :]
            out.append(h)
        a_s[rows, :] = jnp.concatenate(out, axis=0)
        return h

    h_last = lax.fori_loop(0, tt // SUBLANES, scan, h_s[0:1, :])
    h_s[...] = jnp.broadcast_to(h_last, h_s.shape)
    lruh_ref[...] = jnp.broadcast_to(h_last, h_s.shape)
    od_ref[...] = a_s[...] * jax.nn.gelu(gd_ref[...].astype(F32))


def _cd_prompt(zb3, scw, lcw, lcb, wa, ba, wx, bx, lam):
    b, t, _ = zb3.shape
    tt = min(t, 512)
    blk = (None, tt, MIX_W)
    c0 = ZB_C // MIX_W

    def zspec(c):
        return pl.BlockSpec(blk, lambda i, j: (i, j, c))

    def full(shape):
        return pl.BlockSpec(shape, lambda i, j: (0,) * len(shape))

    st = pl.BlockSpec((None, SUBLANES, MIX_W), lambda i, j: (i, 0, 0))
    return pl.pallas_call(
        functools.partial(_cd_body, tt=tt),
        grid=(b, t // tt),
        in_specs=[zspec(c0), zspec(c0 + 1), zspec(c0 + 2), zspec(c0 + 3), zspec(c0 + 4),
                  full((SC_CONV, MIX_W)), full((LRU_CONV, MIX_W)), full((1, MIX_W)),
                  full((MIX_W, MIX_W)), full((1, MIX_W)), full((MIX_W, MIX_W)), full((1, MIX_W)),
                  full((1, MIX_W))],
        out_specs=[pl.BlockSpec(blk, lambda i, j: (i, j, 0)), pl.BlockSpec(blk, lambda i, j: (i, j, 0)), st, st],
        out_shape=[
            jax.ShapeDtypeStruct((b, t, MIX_W), F32),
            jax.ShapeDtypeStruct((b, t, MIX_W), F32),
            jax.ShapeDtypeStruct((b, SUBLANES, MIX_W), F32),
            jax.ShapeDtypeStruct((b, SUBLANES, MIX_W), F32),
        ],
        scratch_shapes=[
            pltpu.VMEM((tt + SUBLANES, MIX_W), F32),
            pltpu.VMEM((tt + SUBLANES, MIX_W), F32),
            pltpu.VMEM((tt, MIX_W), F32),
            pltpu.VMEM((tt, MIX_W), F32),
            pltpu.VMEM((SUBLANES, MIX_W), F32),
        ],
        compiler_params=_cparams(("parallel", "arbitrary")),
        name="cd_prompt",
    )(zb3, zb3, zb3, zb3, zb3, scw, lcw, lcb, wa, ba, wx, bx, lam)


def _merge_body(x_ref, oa_ref, ob_ref, oc_ref, od_ref, g0_ref, g1_ref, g2_ref, g3_ref, wbr_ref, wo_ref, o_ref):
    y = None
    for nbr, (br, g_ref) in enumerate(zip((oa_ref, ob_ref, oc_ref, od_ref), (g0_ref, g1_ref, g2_ref, g3_ref))):
        yb = _dot(br[...], wbr_ref[nbr]) * g_ref[...].astype(F32)
        y = yb if y is None else y + yb
    o_ref[...] = x_ref[...] + _dot(y, wo_ref[...])


def _merge(x, oa, ob, oc, od, gates, wbr, wo):
    m = x.shape[0]
    tm = min(m, 512)
    row = lambda w: pl.BlockSpec((tm, w), lambda i: (i, 0))
    gate_specs = [pl.BlockSpec((tm, D_MODEL), lambda i, n=n: (i, n)) for n in range(N_BRANCH)]
    return pl.pallas_call(
        _merge_body,
        grid=(m // tm,),
        in_specs=[row(D_MODEL)] + [row(MIX_W)] * N_BRANCH + gate_specs +
                 [pl.BlockSpec((N_BRANCH, MIX_W, D_MODEL), lambda i: (0, 0, 0)),
                  pl.BlockSpec((D_MODEL, D_MODEL), lambda i: (0, 0))],
        out_specs=row(D_MODEL),
        out_shape=jax.ShapeDtypeStruct((m, D_MODEL), F32),
        compiler_params=_cparams(("parallel",)),
        name="merge",
    )(x, oa, ob, oc, od, *([gates] * N_BRANCH), wbr, wo)


def _spw_body(za_ref, zb_ref, gcs_ref, scs_ref, lcs_ref, lh_ref, gcw_ref, scw_ref, lcw_ref, lcb_ref, wa_ref,
              ba_ref, wx_ref, bx_ref, lam_ref, act_ref, gcs_o, oc_ref, scs_o, od_ref, lh_o, lcs_o):
    w = A_QKV_W
    x = za_ref[:, 0:w]
    y = gcw_ref[GDN_CONV - 1:GDN_CONV, :] * x
    for i in range(GDN_CONV - 1):
        y = y + gcw_ref[i:i + 1, :] * gcs_ref[:, i * w:(i + 1) * w]
    act_ref[...] = jax.nn.silu(y)
    gcs_o[:, 0:(GDN_CONV - 2) * w] = gcs_ref[:, w:(GDN_CONV - 1) * w]
    gcs_o[:, (GDN_CONV - 2) * w:(GDN_CONV - 1) * w] = x

    w = MIX_W
    gate_b = zb_ref[:, ZB_C:ZB_C + w]
    ci = zb_ref[:, ZB_C + w:ZB_C + 2 * w] * zb_ref[:, ZB_C + 2 * w:ZB_C + 3 * w]
    u = scw_ref[SC_CONV - 1:SC_CONV, :] * ci
    for i in range(SC_CONV - 1):
        u = u + scw_ref[i:i + 1, :] * scs_ref[:, i * w:(i + 1) * w]
    oc_ref[...] = gate_b * u
    scs_o[:, 0:(SC_CONV - 2) * w] = scs_ref[:, w:(SC_CONV - 1) * w]
    scs_o[:, (SC_CONV - 2) * w:(SC_CONV - 1) * w] = ci

    xd = zb_ref[:, ZB_D:ZB_D + w]
    gate_d = zb_ref[:, ZB_D + w:ZB_D + 2 * w]
    xc = lcw_ref[LRU_CONV - 1:LRU_CONV, :] * xd
    for i in range(LRU_CONV - 1):
        xc = xc + lcw_ref[i:i + 1, :] * lcs_ref[:, i * w:(i + 1) * w]
    xc = xc + lcb_ref[...]
    a, bt = _lru_gates(xc, wa_ref, ba_ref, wx_ref, bx_ref, lam_ref)
    hnew = a * lh_ref[...] + bt
    lh_o[...] = hnew
    od_ref[...] = hnew * jax.nn.gelu(gate_d)
    lcs_o[:, 0:(LRU_CONV - 2) * w] = lcs_ref[:, w:(LRU_CONV - 1) * w]
    lcs_o[:, (LRU_CONV - 2) * w:(LRU_CONV - 1) * w] = xd


def _sample_pointwise(za, zb, gcs, scs, lcs, lh, gcw, scw, lcw, lcb, wa, ba, wx, bx, lam):
    bd = za.shape[0]
    shapes = [(bd, A_QKV_W), gcs.shape, (bd, MIX_W), scs.shape, (bd, MIX_W), lh.shape, lcs.shape]
    return pl.pallas_call(
        _spw_body,
        out_shape=[jax.ShapeDtypeStruct(s, F32) for s in shapes],
        compiler_params=pltpu.CompilerParams(vmem_limit_bytes=VMEM_LIMIT),
        name="sample_pointwise",
    )(za, zb, gcs, scs, lcs, lh, gcw, scw, lcw, lcb, wa, ba, wx, bx, lam)


def _col(row, diag):
    return jnp.sum(jnp.where(diag, row, 0.0), axis=1, keepdims=True)


def _sstep_body(act_ref, za_ref, zb_ref, ba_ref, s_ref, alog_ref, dtb_ref, nw_ref,
                k0_ref, v0_ref, k1_ref, v1_ref, k2_ref, v2_ref, bb_ref, bn_ref, sh0_ref, sh1_ref, sh2_ref,
                oa_ref, ob_ref, s_out_ref, w0_ref, w1_ref, w2_ref):
    del sh0_ref, sh1_ref, sh2_ref
    e = GDN_DK
    for g, w_ref in enumerate((w0_ref, w1_ref, w2_ref)):
        for h in range(DIL_GROUP_HEADS):
            c0 = g * MIX_W + h * e
            w_ref[0, 0, h:h + 1, :] = zb_ref[:, ZB_K + c0:ZB_K + c0 + e]
            w_ref[1, 0, h:h + 1, :] = zb_ref[:, ZB_V + c0:ZB_V + c0 + e]

    ri = lax.broadcasted_iota(jnp.int32, (e, e), 0)
    ci = lax.broadcasted_iota(jnp.int32, (e, e), 1)
    diag = ri == ci

    ba = ba_ref[...]
    beta_all = jax.nn.sigmoid(ba)
    g_all = -jnp.exp(alog_ref[...]) * _softplus(ba + dtb_ref[...])
    hd = range(GDN_HEADS)
    q = [act_ref[:, h * e:(h + 1) * e] for h in hd]
    k = [act_ref[:, GDN_HEADS * e + h * e:GDN_HEADS * e + (h + 1) * e] for h in hd]
    v = [act_ref[:, 2 * GDN_HEADS * e + h * GDN_DV:2 * GDN_HEADS * e + (h + 1) * GDN_DV] for h in hd]
    q = [q[h] * lax.rsqrt(jnp.sum(q[h] * q[h], axis=-1, keepdims=True) + NORM_EPS) * (e ** -0.5) for h in hd]
    k = [k[h] * lax.rsqrt(jnp.sum(k[h] * k[h], axis=-1, keepdims=True) + NORM_EPS) for h in hd]
    beta = [beta_all[:, h:h + 1] for h in hd]
    eg = [jnp.exp(g_all[:, GDN_HEADS + h:GDN_HEADS + h + 1]) for h in hd]
    s = [s_ref[h] for h in hd]
    kcol = [_col(k[h], diag) for h in hd]
    qcol = [_col(q[h], diag) for h in hd]
    ks = [jnp.sum(kcol[h] * s[h], axis=0, keepdims=True) for h in hd]
    qs = [jnp.sum(qcol[h] * s[h], axis=0, keepdims=True) for h in hd]
    v_new = [beta[h] * v[h] - (beta[h] * eg[h]) * ks[h] for h in hd]
    qk = [jnp.sum(q[h] * k[h], axis=-1, keepdims=True) for h in hd]
    for h in hd:
        o = eg[h] * qs[h] + qk[h] * v_new[h]
        s_out_ref[h] = s[h] * eg[h] + kcol[h] * v_new[h]
        hv = slice(h * GDN_DV, (h + 1) * GDN_DV)
        oa_ref[:, hv] = _rms(o, nw_ref[...]) * jax.nn.silu(za_ref[:, A_QKV_W + h * GDN_DV:A_QKV_W + (h + 1) * GDN_DV])

    kv = ((k0_ref, v0_ref), (k1_ref, v1_ref), (k2_ref, v2_ref))
    pairs = [(h, g) for h in range(DIL_GROUP_HEADS) for g in range(DIL_GROUPS)]
    c0 = {(h, g): g * MIX_W + h * e for h, g in pairs}
    q = {p: zb_ref[:, ZB_Q + c0[p]:ZB_Q + c0[p] + e] for p in pairs}
    kn = {p: zb_ref[:, ZB_K + c0[p]:ZB_K + c0[p] + e] for p in pairs}
    vn = {p: zb_ref[:, ZB_V + c0[p]:ZB_V + c0[p] + e] for p in pairs}
    sb = {(h, g): jnp.sum(kv[g][0][:, h, :] * q[(h, g)], axis=-1, keepdims=True) for h, g in pairs}
    sn = {p: jnp.sum(kn[p] * q[p], axis=-1, keepdims=True) for p in pairs}
    for h, g in pairs:
        col = g * DIL_GROUP_HEADS + h
        sb[(h, g)] = sb[(h, g)] * (e ** -0.5) + bb_ref[:, col:col + 1]
        sn[(h, g)] = sn[(h, g)] * (e ** -0.5) + bn_ref[:, col:col + 1]
    m = {p: jnp.maximum(jnp.max(sb[p], axis=0, keepdims=True), sn[p]) for p in pairs}
    pb = {p: jnp.exp(sb[p] - m[p]) for p in pairs}
    pn = {p: jnp.exp(sn[p] - m[p]) for p in pairs}
    l = {p: jnp.sum(pb[p], axis=0, keepdims=True) + pn[p] for p in pairs}
    out = {(h, g): (jnp.sum(pb[(h, g)] * kv[g][1][:, h, :], axis=0, keepdims=True) + pn[(h, g)] * vn[(h, g)]) / l[(h, g)]
           for h, g in pairs}
    lse = {p: m[p] + jnp.log(l[p]) for p in pairs}
    for h in range(DIL_GROUP_HEADS):
        lses = [lse[(h, g)] for g in range(DIL_GROUPS)]
        mm = functools.reduce(jnp.maximum, lses)
        es = [jnp.exp(x - mm) for x in lses]
        den = functools.reduce(lambda p, q_: p + q_, es)
        ob_ref[:, h * e:(h + 1) * e] = functools.reduce(
            lambda p, q_: p + q_, [(es[g] / den) * out[(h, g)] for g in range(DIL_GROUPS)])


def _sample_step(act3, za3, zb3, ba3, s_in, alog_row, dtb_row, nw, caches, shifted, layer, bias_buf, bias_new):
    bd = act3.shape[0]
    n_fixed_in = 16
    win_specs = [pl.BlockSpec((None, None, 2, 1, DIL_GROUP_HEADS, DIL_HEAD_DIM),
                              lambda i, rows=buf.shape[3]: (layer, i, 0, rows - 1, 0, 0)) for buf in shifted]

    def vec(w):
        return pl.BlockSpec((None, 1, w), lambda i: (i, 0, 0))

    def full(shape):
        return pl.BlockSpec(shape, lambda i: (0,) * len(shape))

    cache_specs, cache_args = [], []
    for cch in caches:
        for kvi in range(2):
            cache_specs.append(pl.BlockSpec((None, None, None, DIL_BLOCK, None, DIL_GROUP_HEADS, DIL_HEAD_DIM),
                                            lambda i, kvi=kvi: (layer, i, kvi, 0, 0, 0, 0)))
            cache_args.append(cch)
    st = pl.BlockSpec((None, GDN_HEADS, GDN_DK, GDN_DV), lambda i: (i, 0, 0, 0))
    return pl.pallas_call(
        _sstep_body,
        grid=(bd,),
        in_specs=[vec(A_QKV_W), vec(ZA_W), vec(ZB_W), vec(BA_W), st,
                  full((1, BA_W)), full((1, BA_W)), full((1, GDN_DV))] + cache_specs +
                 [full((DIL_BLOCK, DIL_HEADS)), full((1, DIL_HEADS))] +
                 [pl.BlockSpec(memory_space=pl.ANY)] * DIL_GROUPS,
        out_specs=[vec(MIX_W), vec(MIX_W), st] + win_specs,
        out_shape=[jax.ShapeDtypeStruct((bd, 1, MIX_W), F32), jax.ShapeDtypeStruct((bd, 1, MIX_W), F32),
                   jax.ShapeDtypeStruct(s_in.shape, F32)] + [jax.ShapeDtypeStruct(buf.shape, F32) for buf in shifted],
        input_output_aliases={n_fixed_in + g: 3 + g for g in range(DIL_GROUPS)},
        compiler_params=_cparams(("parallel",)),
        name="sample_step",
    )(act3, za3, zb3, ba3, s_in, alog_row, dtb_row, nw, *cache_args, bias_buf, bias_new, *shifted)


def _shift_body(a_ref, b_ref, o_ref, *, rb):
    o_ref[:, pl.ds(0, rb - 1)] = a_ref[:, pl.ds(1, rb - 1)]
    o_ref[:, pl.ds(rb - 1, 1)] = b_ref[...]


def _shift_cache(cache):
    depth, bd, _, rows, nh, e = cache.shape
    rb = min(rows, 512)
    return pl.pallas_call(
        functools.partial(_shift_body, rb=rb),
        grid=(depth, bd, rows // rb),
        in_specs=[
            pl.BlockSpec((None, None, 2, rb, nh, e), lambda l, i, j: (l, i, 0, j, 0, 0)),
            pl.BlockSpec((None, None, 2, 1, nh, e), lambda l, i, j: (l, i, 0, jnp.minimum((j + 1) * rb, rows - 1), 0, 0)),
        ],
        out_specs=pl.BlockSpec((None, None, 2, rb, nh, e), lambda l, i, j: (l, i, 0, j, 0, 0)),
        out_shape=jax.ShapeDtypeStruct(cache.shape, F32),
        compiler_params=_cparams(("parallel", "parallel", "parallel")),
        name="shift_cache",
    )(cache, cache)


def _t5_bucket(dist):
    exact = REL_BUCKETS // 2
    d = np.maximum(dist, 1).astype(np.float32)
    large = exact + (np.log(d / exact) / math.log(REL_MAX_DIST / exact) * (REL_BUCKETS - exact)).astype(np.int32)
    return np.where(dist < exact, dist, np.minimum(large, REL_BUCKETS - 1)).astype(np.int32)


def _prompt_bias(rel_bias):
    blk = DIL_BLOCK
    period = 4 * blk
    out = []
    for gi, (window, dilation) in enumerate(DIL_PATTERNS):
        n_off = window // dilation
        tab = rel_bias[:, gi * DIL_GROUP_HEADS:(gi + 1) * DIL_GROUP_HEADS]
        vals = tab[_t5_bucket(np.arange(n_off + 1) * dilation)].T.astype(F32)
        w = jnp.full((DIL_GROUP_HEADS, period), -jnp.inf, F32)
        w = lax.dynamic_update_slice(w, vals, (0, blk - 1))
        r = jnp.tile(w, (1, blk + 1))[:, :blk * (period + 1)].reshape(DIL_GROUP_HEADS, blk, period + 1)
        out.append(r[:, :, 0:2 * blk][:, :, ::-1])
    return jnp.stack(out, axis=0)


def _step_bias(rel_bias):
    cols_buf, cols_new = [], []
    for gi, (window, dilation) in enumerate(DIL_PATTERNS):
        n_off = window // dilation
        tab = rel_bias[:, gi * DIL_GROUP_HEADS:(gi + 1) * DIL_GROUP_HEADS]
        j = n_off - np.arange(n_off)
        cols_buf.append(tab[_t5_bucket(j * dilation)])
        cols_new.append(tab[_t5_bucket(np.zeros((1,), np.int64))])
    return jnp.concatenate(cols_buf, axis=1).astype(F32), jnp.concatenate(cols_new, axis=1).astype(F32)


def _block_diag(w):
    n, c, _ = w.shape
    eye = jnp.eye(n, dtype=w.dtype)
    return (eye[:, None, :, None] * w[:, :, None, :]).reshape(n * c, n * c)


def _layer_weights(p, l, final_norm):
    w_in = p['w_in'][l]
    b0, c0 = A_W, A_W + B_W
    zg0 = A_QKV_W
    ba0 = A_QKV_W + GDN_HEADS * GDN_DV
    w_all = jnp.concatenate([w_in[:, 0:ba0], w_in[:, b0:], p['w_gate'][l]], axis=1).astype(BF16)
    b_all = p['b_gate'][l][None]
    w_ba = jnp.pad(w_in[:, ba0:b0], ((0, 0), (0, BA_W - 2 * GDN_HEADS))).astype(BF16)
    pad_row = lambda v: jnp.pad(v, (GDN_HEADS, BA_W - 2 * GDN_HEADS))[None]
    row = lambda v: v[None].astype(F32)
    del zg0, c0
    return dict(
        fnw=row(final_norm), last=(l == DEPTH - 1),
        n1=row(p['norm_ffn1'][l]), gu1=p['ffn1_w_gu'][l].astype(BF16), dn1=p['ffn1_w_down'][l].astype(BF16),
        nm=row(p['norm_mix'][l]), w_all=w_all, b_all=b_all, w_ba=w_ba,
        gcw=p['gdn_conv_w'][l], alog=pad_row(p['gdn_a_log'][l]), dtb=pad_row(p['gdn_dt_bias'][l]),
        gnw=row(p['gdn_norm_w'][l]), scw=p['sc_conv_w'][l], lcw=p['lru_conv_w'][l], lcb=row(p['lru_conv_b'][l]),
        wa=_block_diag(p['lru_wa'][l]).astype(BF16), ba=row(p['lru_ba'][l]),
        wx=_block_diag(p['lru_wx'][l]).astype(BF16), bx=row(p['lru_bx'][l]), lam=row(p['lru_lambda'][l]),
        wbr=p['w_branch'][l].astype(BF16), wo=p['w_o'][l].astype(BF16),
        n2=row(p['norm_ffn2'][l]), gu2=p['ffn2_w_gu'][l].astype(BF16), dn2=p['ffn2_w_down'][l].astype(BF16),
    )


def _prompt_layer(x, w, bias, b, t):
    m = b * t
    x = _ffn(x, w['n1'], w['gu1'], w['dn1'], w['fnw'])
    za, zb, gates, ba = _proj(x, w['nm'], w['w_all'], w['b_all'], w['w_ba'], BF16)
    za3 = za.reshape(b, t, ZA_W)
    zb3 = zb.reshape(b, t, ZB_W)
    o_a, s_gdn = _gdn_prompt(za3, ba.reshape(b, t, BA_W), w['gcw'], w['alog'], w['dtb'], w['gnw'])
    o_b = _dil_prompt(zb3, bias)
    o_c, o_d, sc_st, lru_h = _cd_prompt(zb3, w['scw'], w['lcw'], w['lcb'], w['wa'], w['ba'], w['wx'], w['bx'], w['lam'])
    x = _merge(x, o_a.reshape(m, MIX_W), o_b.reshape(m, MIX_W), o_c.reshape(m, MIX_W), o_d.reshape(m, MIX_W),
               gates, w['wbr'], w['wo'])
    x = _ffn(x, w['n2'], w['gu2'], w['dn2'], w['fnw'], final=w['last'])
    gdn_conv = za3[:, t - (GDN_CONV - 1):, 0:A_QKV_W].astype(F32)
    bufs = []
    for gi, (window, _) in enumerate(DIL_PATTERNS):
        rows = min(window, t)
        k = zb3[:, t - rows:, ZB_K + gi * MIX_W:ZB_K + (gi + 1) * MIX_W]
        v = zb3[:, t - rows:, ZB_V + gi * MIX_W:ZB_V + (gi + 1) * MIX_W]
        bufs.append(jnp.stack([k, v], axis=1).reshape(b, 2, rows, DIL_GROUP_HEADS, DIL_HEAD_DIM).astype(F32))
    sc_conv = sc_st[:, SUBLANES - (SC_CONV - 1):]
    lru_conv = zb3[:, t - (LRU_CONV - 1):, ZB_D:ZB_D + MIX_W].astype(F32)
    return x, (s_gdn, gdn_conv, bufs[0], bufs[1], bufs[2], sc_conv, lru_h[:, 0], lru_conv)


def _sample_layer(x, w, l, st, shifted, step_bias):
    bd = x.shape[0]
    s_gdn, gdn_conv, caches, sc_conv, lru_h, lru_conv = st
    x = _ffn(x, w['n1'], w['gu1'], w['dn1'], w['fnw'])
    za, zb, gates, ba = _proj(x, w['nm'], w['w_all'], w['b_all'], w['w_ba'], F32)
    act, gdn_conv_new, o_c, sc_new, o_d, lru_h_new, lru_conv_new = _sample_pointwise(
        za, zb, gdn_conv.reshape(bd, -1), sc_conv.reshape(bd, -1), lru_conv.reshape(bd, -1), lru_h,
        w['gcw'], w['scw'], w['lcw'], w['lcb'], w['wa'], w['ba'], w['wx'], w['bx'], w['lam'])
    zb3 = zb.reshape(bd, 1, ZB_W)
    o_a, o_b, s_new, *shifted = _sample_step(act.reshape(bd, 1, A_QKV_W), za.reshape(bd, 1, ZA_W), zb3,
                                             ba.reshape(bd, 1, BA_W), s_gdn, w['alog'], w['dtb'], w['gnw'], caches,
                                             shifted, l, *step_bias)
    x = _merge(x, o_a.reshape(bd, MIX_W), o_b.reshape(bd, MIX_W), o_c, o_d, gates, w['wbr'], w['wo'])
    x = _ffn(x, w['n2'], w['gu2'], w['dn2'], w['fnw'], final=w['last'])
    new = (s_new, gdn_conv_new.reshape(gdn_conv.shape), sc_new.reshape(sc_conv.shape), lru_h_new,
           lru_conv_new.reshape(lru_conv.shape))
    return x, new, shifted


def kernel(x_prompt, x_sample, state_gdn, state_gdn_conv, cache_dil_w128, cache_dil_w512, cache_dil_w2048,
           state_sc_conv, state_lru, state_lru_conv, norm_ffn1, ffn1_w_gu, ffn1_w_down, norm_mix, w_in,
           gdn_conv_w, gdn_a_log, gdn_dt_bias, gdn_norm_w, rel_bias, sc_conv_w, lru_conv_w, lru_conv_b,
           lru_wa, lru_ba, lru_wx, lru_bx, lru_lambda, w_gate, b_gate, w_branch, w_o, norm_ffn2, ffn2_w_gu,
           ffn2_w_down, final_norm):
    p = dict(norm_ffn1=norm_ffn1, ffn1_w_gu=ffn1_w_gu, ffn1_w_down=ffn1_w_down, norm_mix=norm_mix, w_in=w_in,
             gdn_conv_w=gdn_conv_w, gdn_a_log=gdn_a_log, gdn_dt_bias=gdn_dt_bias, gdn_norm_w=gdn_norm_w,
             sc_conv_w=sc_conv_w, lru_conv_w=lru_conv_w, lru_conv_b=lru_conv_b, lru_wa=lru_wa, lru_ba=lru_ba,
             lru_wx=lru_wx, lru_bx=lru_bx, lru_lambda=lru_lambda, w_gate=w_gate, b_gate=b_gate,
             w_branch=w_branch, w_o=w_o, norm_ffn2=norm_ffn2, ffn2_w_gu=ffn2_w_gu, ffn2_w_down=ffn2_w_down)
    weights = [_layer_weights(p, l, final_norm) for l in range(DEPTH)]
    b, t, _ = x_prompt.shape
    bd = x_sample.shape[0]
    bias = _prompt_bias(rel_bias)
    step_bias = _step_bias(rel_bias)

    x = x_prompt.reshape(b * t, D_MODEL)
    p_states = []
    for l in range(DEPTH):
        x, new = _prompt_layer(x, weights[l], bias, b, t)
        p_states.append(new)
    y_prompt = x.reshape(b, t, D_MODEL)
    p_out = tuple(jnp.stack(zs, axis=0) for zs in zip(*p_states))

    caches = (cache_dil_w128, cache_dil_w512, cache_dil_w2048)
    shifted = [_shift_cache(c) for c in caches]
    strided = tuple(c.reshape(DEPTH, bd, 2, c.shape[3] // d, d, DIL_GROUP_HEADS, DIL_HEAD_DIM)
                    for c, (_, d) in zip(caches, DIL_PATTERNS))
    x = x_sample.reshape(bd, D_MODEL)
    s_states = []
    for l in range(DEPTH):
        st = (state_gdn[l], state_gdn_conv[l], strided, state_sc_conv[l], state_lru[l], state_lru_conv[l])
        x, new, shifted = _sample_layer(x, weights[l], l, st, shifted, step_bias)
        s_states.append(new)
    y_sample = x.reshape(bd, 1, D_MODEL)
    s_gdn, s_gdn_conv, s_sc_conv, s_lru, s_lru_conv = (jnp.stack(zs, axis=0) for zs in zip(*s_states))

    p_gdn, p_gdn_conv, p_w128, p_w512, p_w2048, p_sc_conv, p_lru, p_lru_conv = p_out
    return (y_prompt, y_sample, p_gdn, s_gdn, p_gdn_conv, s_gdn_conv, p_w128, shifted[0], p_w512, shifted[1],
            p_w2048, shifted[2], p_sc_conv, s_sc_conv, p_lru, s_lru, p_lru_conv, s_lru_conv)
```

```python
import functools
import math

import numpy as np
import jax
import jax.numpy as jnp
from jax import lax
from jax.experimental import pallas as pl
from jax.experimental.pallas import tpu as pltpu

D_MODEL = 1024
DEPTH = 4
MIX_W = D_MODEL // 2
D_FF = 2816
NORM_EPS = 1e-6
N_BRANCH = 4
GDN_HEADS = 4
GDN_DK = 128
GDN_DV = MIX_W // GDN_HEADS
GDN_CONV = 4
GDN_CHUNK = 64
DIL_PATTERNS = ((128, 1), (512, 4), (2048, 16))
DIL_GROUPS = len(DIL_PATTERNS)
DIL_GROUP_HEADS = 4
DIL_HEAD_DIM = MIX_W // DIL_GROUP_HEADS
DIL_HEADS = DIL_GROUPS * DIL_GROUP_HEADS
DIL_BLOCK = 128
REL_BUCKETS = 32
REL_MAX_DIST = 2048
SC_CONV = 3
LRU_BLOCKS = 8
LRU_BLOCK_W = MIX_W // LRU_BLOCKS
LRU_CONV = 4
LRU_C = 8.0
A_QKV_W = GDN_HEADS * (2 * GDN_DK + GDN_DV)
A_W = A_QKV_W + GDN_HEADS * GDN_DV + 2 * GDN_HEADS
B_W = 3 * DIL_HEADS * DIL_HEAD_DIM
C_W = 3 * MIX_W
D_W = 2 * MIX_W

LANES = 128
SUBLANES = 8
VMEM_LIMIT = 56 * 1024 * 1024

ZA_W = A_QKV_W + MIX_W
ZB_Q, ZB_K, ZB_V = 0, B_W // 3, 2 * B_W // 3
ZB_C = B_W
ZB_D = B_W + C_W
ZB_W = B_W + C_W + D_W
G_W = N_BRANCH * D_MODEL
PROJ_TN = 1024
BA_W = LANES
DIL_TB = DIL_BLOCK * max(d for _, d in DIL_PATTERNS)
DIL_UNROLL = 16

F32 = jnp.float32
BF16 = jnp.bfloat16
NT_DIMS = (((1,), (1,)), ((), ()))
TN_DIMS = (((0,), (0,)), ((), ()))


def _cparams(sem):
    return pltpu.CompilerParams(dimension_semantics=sem, vmem_limit_bytes=VMEM_LIMIT)


def _rms(x, w):
    return x * lax.rsqrt(jnp.mean(x * x, axis=-1, keepdims=True) + NORM_EPS) * w


def _softplus(x):
    return jnp.maximum(x, 0.0) + jnp.log1p(jnp.exp(-jnp.abs(x)))


def _causal_conv(xb, w_ref, tt):
    taps = w_ref.shape[0]
    y = None
    for i in range(taps):
        back = taps - 1 - i
        xs = xb if back == 0 else pltpu.roll(xb, back, 0)
        term = w_ref[i:i + 1, :] * xs[SUBLANES:SUBLANES + tt]
        y = term if y is None else y + term
    return y


def _dot(a, b):
    return jnp.dot(a.astype(BF16), b.astype(BF16), preferred_element_type=F32)


def _dot_nt(a, b):
    return lax.dot_general(a.astype(BF16), b.astype(BF16), NT_DIMS, preferred_element_type=F32)


def _dot_tn(a, b):
    return lax.dot_general(a.astype(BF16), b.astype(BF16), TN_DIMS, preferred_element_type=F32)


def _ffn_body(x_ref, nw_ref, wg_ref, wu_ref, wd_ref, fw_ref, o_ref, h_ref, acc_ref, *, final):
    j = pl.program_id(1)

    @pl.when(j == 0)
    def _():
        h_ref[...] = _rms(x_ref[...], nw_ref[...]).astype(BF16)
        acc_ref[...] = jnp.zeros_like(acc_ref)

    h = h_ref[...]
    g = _dot(h, wg_ref[...])
    u = _dot(h, wu_ref[...])
    acc_ref[...] += _dot(jax.nn.silu(g) * u, wd_ref[...])

    @pl.when(j == pl.num_programs(1) - 1)
    def _():
        y = x_ref[...] + 0.5 * acc_ref[...]
        o_ref[...] = _rms(y, fw_ref[...]) if final else y


def _ffn(x, nw, w_gu, w_down, fw, final=False):
    m = x.shape[0]
    tm = min(m, 1024)
    nf = 2
    tf = D_FF // nf
    return pl.pallas_call(
        functools.partial(_ffn_body, final=final),
        grid=(m // tm, nf),
        in_specs=[
            pl.BlockSpec((tm, D_MODEL), lambda i, j: (i, 0)),
            pl.BlockSpec((1, D_MODEL), lambda i, j: (0, 0)),
            pl.BlockSpec((D_MODEL, tf), lambda i, j: (0, j)),
            pl.BlockSpec((D_MODEL, tf), lambda i, j: (0, nf + j)),
            pl.BlockSpec((tf, D_MODEL), lambda i, j: (j, 0)),
            pl.BlockSpec((1, D_MODEL), lambda i, j: (0, 0)),
        ],
        out_specs=pl.BlockSpec((tm, D_MODEL), lambda i, j: (i, 0)),
        out_shape=jax.ShapeDtypeStruct((m, D_MODEL), F32),
        scratch_shapes=[pltpu.VMEM((tm, D_MODEL), BF16), pltpu.VMEM((tm, D_MODEL), F32)],
        compiler_params=_cparams(("parallel", "arbitrary")),
        name="ffn",
    )(x, nw, w_gu, w_gu, w_down, fw)


def _proj_body(x_ref, nw_ref, w_ref, b_ref, wba_ref, za_ref, zb_ref, g_ref, ba_ref, h_ref, *, ja, jb):
    j = pl.program_id(1)

    @pl.when(j == 0)
    def _():
        h = _rms(x_ref[...], nw_ref[...]).astype(BF16)
        h_ref[...] = h
        ba_ref[...] = _dot(h, wba_ref[...])

    @pl.when(j < ja)
    def _():
        za_ref[...] = _dot(h_ref[...], w_ref[...]).astype(za_ref.dtype)

    @pl.when((j >= ja) & (j < jb))
    def _():
        zb_ref[...] = _dot(h_ref[...], w_ref[...]).astype(zb_ref.dtype)

    @pl.when(j >= jb)
    def _():
        g_ref[...] = jax.nn.sigmoid(_dot(h_ref[...], w_ref[...]) + b_ref[...]).astype(BF16)


def _proj(x, nw, w, b, w_ba, act_dtype):
    m = x.shape[0]
    tm = min(m, 1024)
    tn = PROJ_TN
    ja, jb = ZA_W // tn, (ZA_W + ZB_W) // tn
    nj = (ZA_W + ZB_W + G_W) // tn
    return pl.pallas_call(
        functools.partial(_proj_body, ja=ja, jb=jb),
        grid=(m // tm, nj),
        in_specs=[
            pl.BlockSpec((tm, D_MODEL), lambda i, j: (i, 0)),
            pl.BlockSpec((1, D_MODEL), lambda i, j: (0, 0)),
            pl.BlockSpec((D_MODEL, tn), lambda i, j: (0, j)),
            pl.BlockSpec((1, tn), lambda i, j: (0, jnp.maximum(j - jb, 0))),
            pl.BlockSpec((D_MODEL, BA_W), lambda i, j: (0, 0)),
        ],
        out_specs=[
            pl.BlockSpec((tm, tn), lambda i, j: (i, jnp.minimum(j, ja - 1))),
            pl.BlockSpec((tm, tn), lambda i, j: (i, jnp.clip(j - ja, 0, jb - ja - 1))),
            pl.BlockSpec((tm, tn), lambda i, j: (i, jnp.maximum(j - jb, 0))),
            pl.BlockSpec((tm, BA_W), lambda i, j: (i, 0)),
        ],
        out_shape=[
            jax.ShapeDtypeStruct((m, ZA_W), act_dtype),
            jax.ShapeDtypeStruct((m, ZB_W), act_dtype),
            jax.ShapeDtypeStruct((m, G_W), BF16),
            jax.ShapeDtypeStruct((m, BA_W), F32),
        ],
        scratch_shapes=[pltpu.VMEM((tm, D_MODEL), BF16)],
        compiler_params=_cparams(("parallel", "arbitrary")),
        name="proj",
    )(x, nw, w, b, w_ba)


def _gdn_body(qkv_ref, zg_ref, ba_ref, cw_ref, alog_ref, dtb_ref, nw_ref, o_ref, s_out_ref,
              xbuf, act, beta_s, gc_s, u_s, wq_s, at_s, kd_s, s_s, *, tt, nb):
    t = pl.program_id(0)
    c = GDN_CHUNK
    nchunk = tt // c

    @pl.when(t == 0)
    def _():
        for b in range(nb):
            xbuf[b, pl.ds(0, SUBLANES), :] = jnp.zeros((SUBLANES, A_QKV_W), F32)
        s_s[...] = jnp.zeros_like(s_s)

    rowmod = lax.broadcasted_iota(jnp.int32, (tt, BA_W), 0) & (c - 1)
    for b in range(nb):
        xbuf[b, pl.ds(SUBLANES, tt), :] = qkv_ref[b].astype(F32)
        act[b] = jax.nn.silu(_causal_conv(xbuf[b], cw_ref, tt))
        xbuf[b, pl.ds(0, SUBLANES), :] = xbuf[b, pl.ds(tt, SUBLANES), :]

        ba = ba_ref[b]
        beta_s[b] = jax.nn.sigmoid(ba)
        g = -jnp.exp(alog_ref[...]) * _softplus(ba + dtb_ref[...])
        sh = 1
        while sh < c:
            g = g + jnp.where(rowmod >= sh, pltpu.roll(g, sh, 0), 0.0)
            sh *= 2
        gc_s[b] = g

    ri = lax.broadcasted_iota(jnp.int32, (c, c), 0)
    ci = lax.broadcasted_iota(jnp.int32, (c, c), 1)
    causal = ri >= ci
    strict = ri > ci
    diag = ri == ci
    eye = jnp.where(diag, 1.0, 0.0).astype(F32)

    heads = range(GDN_HEADS)

    def prep(i, carry):
        chains = [(b, 2 * i + j, h) for b in range(nb) for j in range(2) for h in heads]
        n = len(chains)
        rows_l, gcol_l, decay_l, q_l, k_l, kb_l, vb_l = [], [], [], [], [], [], []
        for b, ch, h in chains:
            rows = pl.ds(pl.multiple_of(ch * c, c), c)
            gcol = gc_s[b, rows, GDN_HEADS + h:GDN_HEADS + h + 1]
            grow = jnp.sum(jnp.where(diag, gcol, 0.0), axis=0, keepdims=True)
            decay = jnp.exp(jnp.where(causal, gcol - grow, -jnp.inf))
            q = act[b, rows, h * GDN_DK:(h + 1) * GDN_DK]
            k = act[b, rows, GDN_HEADS * GDN_DK + h * GDN_DK:GDN_HEADS * GDN_DK + (h + 1) * GDN_DK]
            v = act[b, rows, 2 * GDN_HEADS * GDN_DK + h * GDN_DV:2 * GDN_HEADS * GDN_DK + (h + 1) * GDN_DV]
            q = q * lax.rsqrt(jnp.sum(q * q, axis=-1, keepdims=True) + NORM_EPS) * (GDN_DK ** -0.5)
            k = k * lax.rsqrt(jnp.sum(k * k, axis=-1, keepdims=True) + NORM_EPS)
            beta = beta_s[b, rows, h:h + 1]
            rows_l.append(rows)
            gcol_l.append(gcol)
            decay_l.append(decay)
            q_l.append(q)
            k_l.append(k)
            kb_l.append(k * beta)
            vb_l.append(v * beta)
        kq_l = [_dot_nt(jnp.concatenate([kb_l[j], q_l[j]], axis=0), k_l[j]) for j in range(n)]
        for j, (b, ch, h) in enumerate(chains):
            at_s[b, rows_l[j], h * LANES:h * LANES + c] = kq_l[j][c:] * decay_l[j]
        qq = [-jnp.where(strict, kq_l[j][:c] * decay_l[j], 0.0) for j in range(n)]
        yy = [eye + qq[j] for j in range(n)]
        qq = [_dot(qq[j], qq[j]) for j in range(n)]
        for _ in range(int(math.log2(c)) - 2):
            yq = [_dot(jnp.concatenate([yy[j], qq[j]], axis=0), qq[j]) for j in range(n)]
            yy = [yy[j] + yq[j][:c] for j in range(n)]
            qq = [yq[j][c:] for j in range(n)]
        yq = [_dot(yy[j], qq[j]) for j in range(n)]
        tinv = [yy[j] + yq[j] for j in range(n)]
        egc = [jnp.exp(gcol_l[j]) for j in range(n)]
        sol = [_dot(tinv[j], jnp.concatenate([vb_l[j], kb_l[j] * egc[j]], axis=1)) for j in range(n)]
        for j, (b, ch, h) in enumerate(chains):
            hs = slice(h * GDN_DK, (h + 1) * GDN_DK)
            u_s[b, rows_l[j], h * GDN_DV:(h + 1) * GDN_DV] = sol[j][:, :GDN_DV]
            wq_s[b, pl.ds(pl.multiple_of(ch * 2 * c, 2 * c), c), hs] = sol[j][:, GDN_DV:]
            wq_s[b, pl.ds(pl.multiple_of(ch * 2 * c + c, c), c), hs] = q_l[j] * egc[j]
            kd_s[b, rows_l[j], hs] = k_l[j] * jnp.exp(gcol_l[j][c - 1:c, :] - gcol_l[j])
        return carry

    lax.fori_loop(0, nchunk // 2, prep, 0)

    def step(ch, carry):
        rows = pl.ds(pl.multiple_of(ch * c, c), c)
        wq_rows = pl.ds(pl.multiple_of(ch * 2 * c, 2 * c), 2 * c)
        chains = [(b, h) for b in range(nb) for h in heads]
        hs = {h: slice(h * GDN_DK, (h + 1) * GDN_DK) for h in heads}
        hv = {h: slice(h * GDN_DV, (h + 1) * GDN_DV) for h in heads}
        glast = [jnp.exp(gc_s[b, pl.ds(ch * c + c - 1, 1), :]) for b in range(nb)]
        st = [s_s[b, h] for b, h in chains]
        ws = [_dot(wq_s[b, wq_rows, hs[h]], st[j]) for j, (b, h) in enumerate(chains)]
        v_new = [u_s[b, rows, hv[h]] - ws[j][:c] for j, (b, h) in enumerate(chains)]
        av = [_dot(at_s[b, rows, h * LANES:h * LANES + c], v_new[j]) for j, (b, h) in enumerate(chains)]
        kv = [_dot_tn(kd_s[b, rows, hs[h]], v_new[j]) for j, (b, h) in enumerate(chains)]
        for j, (b, h) in enumerate(chains):
            s_s[b, h] = st[j] * glast[b][:, GDN_HEADS + h:GDN_HEADS + h + 1] + kv[j]
            o = _rms(ws[j][c:] + av[j], nw_ref[...]) * jax.nn.silu(zg_ref[b, rows, hv[h]].astype(F32))
            o_ref[b, rows, hv[h]] = o
        return carry

    lax.fori_loop(0, nchunk, step, 0)

    @pl.when(t == pl.num_programs(0) - 1)
    def _():
        s_out_ref[...] = s_s[...]


def _gdn_prompt(za3, ba3, cw, alog_row, dtb_row, nw):
    b, t, _ = za3.shape
    tt = min(t, 256)
    return pl.pallas_call(
        functools.partial(_gdn_body, tt=tt, nb=b),
        grid=(t // tt,),
        in_specs=[
            pl.BlockSpec((b, tt, A_QKV_W), lambda j: (0, j, 0)),
            pl.BlockSpec((b, tt, MIX_W), lambda j: (0, j, A_QKV_W // MIX_W)),
            pl.BlockSpec((b, tt, BA_W), lambda j: (0, j, 0)),
            pl.BlockSpec((GDN_CONV, A_QKV_W), lambda j: (0, 0)),
            pl.BlockSpec((1, BA_W), lambda j: (0, 0)),
            pl.BlockSpec((1, BA_W), lambda j: (0, 0)),
            pl.BlockSpec((1, GDN_DV), lambda j: (0, 0)),
        ],
        out_specs=[
            pl.BlockSpec((b, tt, MIX_W), lambda j: (0, j, 0)),
            pl.BlockSpec((b, GDN_HEADS, GDN_DK, GDN_DV), lambda j: (0, 0, 0, 0)),
        ],
        out_shape=[
            jax.ShapeDtypeStruct((b, t, MIX_W), F32),
            jax.ShapeDtypeStruct((b, GDN_HEADS, GDN_DK, GDN_DV), F32),
        ],
        scratch_shapes=[
            pltpu.VMEM((b, tt + SUBLANES, A_QKV_W), F32),
            pltpu.VMEM((b, tt, A_QKV_W), F32),
            pltpu.VMEM((b, tt, BA_W), F32),
            pltpu.VMEM((b, tt, BA_W), F32),
            pltpu.VMEM((b, tt, MIX_W), F32),
            pltpu.VMEM((b, 2 * tt, MIX_W), F32),
            pltpu.VMEM((b, tt, GDN_HEADS * LANES), F32),
            pltpu.VMEM((b, tt, MIX_W), F32),
            pltpu.VMEM((b, GDN_HEADS, GDN_DK, GDN_DV), F32),
        ],
        compiler_params=_cparams(("arbitrary",)),
        name="gdn_prompt",
    )(za3, za3, ba3, cw, alog_row, dtb_row, nw)


def _dil_body(*refs):
    qkv = refs[:3 * DIL_GROUPS]
    bias_ref, o_ref = refs[3 * DIL_GROUPS], refs[3 * DIL_GROUPS + 1]
    kvbuf = refs[3 * DIL_GROUPS + 2:3 * DIL_GROUPS + 2 + 2 * DIL_GROUPS]
    og, lg, qb = refs[-3], refs[-2], refs[-1]
    n = pl.program_id(2)
    e = DIL_HEAD_DIM
    blk = DIL_BLOCK
    tb = DIL_TB
    for g, (_, d) in enumerate(DIL_PATTERNS):
        q_ref, k_ref, v_ref = qkv[3 * g:3 * g + 3]
        kb, vb = kvbuf[2 * g], kvbuf[2 * g + 1]
        tail = blk * d

        @pl.when(n == 0)
        def _(kb=kb, vb=vb, tail=tail):
            kb[pl.ds(0, tail), :] = jnp.zeros((tail, e), F32)
            vb[pl.ds(0, tail), :] = jnp.zeros((tail, e), F32)

        kb[pl.ds(tail, tb), :] = k_ref[...].astype(F32)
        vb[pl.ds(tail, tb), :] = v_ref[...].astype(F32)
        qb[...] = q_ref[...].astype(F32)
        bp = bias_ref[g, :, 0:blk]
        bc = bias_ref[g, :, blk:2 * blk]
        shift = int(math.log2(d))

        def rows_at(base, d=d):
            return pl.ds(base, blk) if d == 1 else pl.ds(base, blk, stride=d)

        def body(it, carry, d=d, g=g, kb=kb, vb=vb, bp=bp, bc=bc, shift=shift, tail=tail,
                 rows_at=rows_at):
            us = range(DIL_UNROLL)
            idx = [it * DIL_UNROLL + u for u in us]
            sub = [i >> shift for i in idx]
            base = [sub[u] * tail + (idx[u] & (d - 1)) for u in us]
            q = [qb[rows_at(base[u]), :].astype(BF16) for u in us]
            sp = [_dot_nt(q[u], kb[rows_at(base[u]), :]) for u in us]
            sc = [_dot_nt(q[u], kb[rows_at(base[u] + tail), :]) for u in us]
            pp, pc, l, lse = [], [], [], []
            for u in us:
                spu = jnp.where((n > 0) | (sub[u] > 0), sp[u] * (e ** -0.5) + bp, -jnp.inf)
                scu = sc[u] * (e ** -0.5) + bc
                m = jnp.max(jnp.maximum(spu, scu), axis=-1, keepdims=True)
                ppu = jnp.exp(spu - m)
                pcu = jnp.exp(scu - m)
                lu = jnp.sum(ppu + pcu, axis=-1, keepdims=True)
                pp.append(ppu)
                pc.append(pcu)
                l.append(lu)
                lse.append(m + jnp.log(lu))
            op = [_dot(pp[u], vb[rows_at(base[u]), :]) for u in us]
            oc = [_dot(pc[u], vb[rows_at(base[u] + tail), :]) for u in us]
            for u in us:
                og[g, rows_at(base[u]), :] = (op[u] + oc[u]) / l[u]
                lg[g, rows_at(base[u]), :] = jnp.broadcast_to(lse[u], (blk, e))
            return carry

        lax.fori_loop(0, tb // blk // DIL_UNROLL, body, 0)
        kb[pl.ds(0, tail), :] = kb[pl.ds(tb, tail), :]
        vb[pl.ds(0, tail), :] = vb[pl.ds(tb, tail), :]

    lses = [lg[g] for g in range(DIL_GROUPS)]
    m = functools.reduce(jnp.maximum, lses)
    es = [jnp.exp(l - m) for l in lses]
    den = functools.reduce(lambda p, q_: p + q_, es)
    o_ref[...] = functools.reduce(lambda p, q_: p + q_, [(es[g] / den) * og[g] for g in range(DIL_GROUPS)])


def _dil_prompt(zb3, bias):
    b, t, _ = zb3.shape
    tb = DIL_TB
    e = DIL_HEAD_DIM
    specs, args = [], []
    for g in range(DIL_GROUPS):
        for off in (ZB_Q, ZB_K, ZB_V):
            cb = (off + g * MIX_W) // e
            specs.append(pl.BlockSpec((None, tb, e), lambda i, h, n, cb=cb: (i, n, cb + h)))
            args.append(zb3)
    specs.append(pl.BlockSpec((DIL_GROUPS, None, DIL_BLOCK, 2 * DIL_BLOCK), lambda i, h, n: (0, h, 0, 0)))
    scratch = []
    for _, d in DIL_PATTERNS:
        scratch += [pltpu.VMEM((DIL_BLOCK * d + tb, e), F32)] * 2
    scratch += [pltpu.VMEM((DIL_GROUPS, tb, e), F32)] * 2
    scratch += [pltpu.VMEM((tb, e), F32)]
    return pl.pallas_call(
        _dil_body,
        grid=(b, DIL_GROUP_HEADS, t // tb),
        in_specs=specs,
        out_specs=pl.BlockSpec((None, tb, e), lambda i, h, n: (i, n, h)),
        out_shape=jax.ShapeDtypeStruct((b, t, MIX_W), F32),
        scratch_shapes=scratch,
        compiler_params=_cparams(("parallel", "parallel", "arbitrary")),
        name="dil_prompt",
    )(*args, bias)


def _lru_gates(xc, wa_ref, ba_ref, wx_ref, bx_ref, lam_ref):
    xb = xc.astype(BF16)
    r = jax.nn.sigmoid(_dot(xb, wa_ref[...]) + ba_ref[...])
    i = jax.nn.sigmoid(_dot(xb, wx_ref[...]) + bx_ref[...])
    log_a = -LRU_C * r * _softplus(-lam_ref[...])
    a = jnp.exp(log_a)
    bt = jnp.sqrt(-jnp.tanh(log_a) * (jnp.exp(2.0 * log_a) + 1.0)) * i * xc
    return a, bt


def _cd_body(gb_ref, gc_ref, xi_ref, xd_ref, gd_ref, scw_ref, lcw_ref, lcb_ref, wa_ref, ba_ref, wx_ref,
             bx_ref, lam_ref, oc_ref, od_ref, scst_ref, lruh_ref, cbuf, dbuf, a_s, b_s, h_s, *, tt):
    t = pl.program_id(1)

    @pl.when(t == 0)
    def _():
        cbuf[pl.ds(0, SUBLANES), :] = jnp.zeros((SUBLANES, MIX_W), F32)
        dbuf[pl.ds(0, SUBLANES), :] = jnp.zeros((SUBLANES, MIX_W), F32)
        h_s[...] = jnp.zeros_like(h_s)

    cbuf[pl.ds(SUBLANES, tt), :] = gc_ref[...].astype(F32) * xi_ref[...].astype(F32)
    oc_ref[...] = gb_ref[...].astype(F32) * _causal_conv(cbuf[...], scw_ref, tt)
    tail = cbuf[pl.ds(tt, SUBLANES), :]
    scst_ref[...] = tail
    cbuf[pl.ds(0, SUBLANES), :] = tail

    dbuf[pl.ds(SUBLANES, tt), :] = xd_ref[...].astype(F32)
    xc = _causal_conv(dbuf[...], lcw_ref, tt) + lcb_ref[...]
    dbuf[pl.ds(0, SUBLANES), :] = dbuf[pl.ds(tt, SUBLANES), :]
    a, bt = _lru_gates(xc, wa_ref, ba_ref, wx_ref, bx_ref, lam_ref)
    a_s[...] = a
    b_s[...] = bt

    def scan(g, h):
        rows = pl.ds(pl.multiple_of(g * SUBLANES, SUBLANES), SUBLANES)
        a8 = a_s[rows, :]
        b8 = b_s[rows, :]
        out = []
        for r in range(SUBLANES):
            h = a8[r:r + 1, :] * h + b8[r:r + 1, :]
            out.append(h)
        a_s[rows, :] = jnp.concatenate(out, axis=0)
        return h

    h_last = lax.fori_loop(0, tt // SUBLANES, scan, h_s[0:1, :])
    h_s[...] = jnp.broadcast_to(h_last, h_s.shape)
    lruh_ref[...] = jnp.broadcast_to(h_last, h_s.shape)
    od_ref[...] = a_s[...] * jax.nn.gelu(gd_ref[...].astype(F32))


def _cd_prompt(zb3, scw, lcw, lcb, wa, ba, wx, bx, lam):
    b, t, _ = zb3.shape
    tt = min(t, 512)
    blk = (None, tt, MIX_W)
    c0 = ZB_C // MIX_W

    def zspec(c):
        return pl.BlockSpec(blk, lambda i, j: (i, j, c))

    def full(shape):
        return pl.BlockSpec(shape, lambda i, j: (0,) * len(shape))

    st = pl.BlockSpec((None, SUBLANES, MIX_W), lambda i, j: (i, 0, 0))
    return pl.pallas_call(
        functools.partial(_cd_body, tt=tt),
        grid=(b, t // tt),
        in_specs=[zspec(c0), zspec(c0 + 1), zspec(c0 + 2), zspec(c0 + 3), zspec(c0 + 4),
                  full((SC_CONV, MIX_W)), full((LRU_CONV, MIX_W)), full((1, MIX_W)),
                  full((MIX_W, MIX_W)), full((1, MIX_W)), full((MIX_W, MIX_W)), full((1, MIX_W)),
                  full((1, MIX_W))],
        out_specs=[pl.BlockSpec(blk, lambda i, j: (i, j, 0)), pl.BlockSpec(blk, lambda i, j: (i, j, 0)), st, st],
        out_shape=[
            jax.ShapeDtypeStruct((b, t, MIX_W), F32),
            jax.ShapeDtypeStruct((b, t, MIX_W), F32),
            jax.ShapeDtypeStruct((b, SUBLANES, MIX_W), F32),
            jax.ShapeDtypeStruct((b, SUBLANES, MIX_W), F32),
        ],
        scratch_shapes=[
            pltpu.VMEM((tt + SUBLANES, MIX_W), F32),
            pltpu.VMEM((tt + SUBLANES, MIX_W), F32),
            pltpu.VMEM((tt, MIX_W), F32),
            pltpu.VMEM((tt, MIX_W), F32),
            pltpu.VMEM((SUBLANES, MIX_W), F32),
        ],
        compiler_params=_cparams(("parallel", "arbitrary")),
        name="cd_prompt",
    )(zb3, zb3, zb3, zb3, zb3, scw, lcw, lcb, wa, ba, wx, bx, lam)


def _merge_body(x_ref, oa_ref, ob_ref, oc_ref, od_ref, g0_ref, g1_ref, g2_ref, g3_ref, wbr_ref, wo_ref, o_ref):
    y = None
    for nbr, (br, g_ref) in enumerate(zip((oa_ref, ob_ref, oc_ref, od_ref), (g0_ref, g1_ref, g2_ref, g3_ref))):
        yb = _dot(br[...], wbr_ref[nbr]) * g_ref[...].astype(F32)
        y = yb if y is None else y + yb
    o_ref[...] = x_ref[...] + _dot(y, wo_ref[...])


def _merge(x, oa, ob, oc, od, gates, wbr, wo):
    m = x.shape[0]
    tm = min(m, 512)
    row = lambda w: pl.BlockSpec((tm, w), lambda i: (i, 0))
    gate_specs = [pl.BlockSpec((tm, D_MODEL), lambda i, n=n: (i, n)) for n in range(N_BRANCH)]
    return pl.pallas_call(
        _merge_body,
        grid=(m // tm,),
        in_specs=[row(D_MODEL)] + [row(MIX_W)] * N_BRANCH + gate_specs +
                 [pl.BlockSpec((N_BRANCH, MIX_W, D_MODEL), lambda i: (0, 0, 0)),
                  pl.BlockSpec((D_MODEL, D_MODEL), lambda i: (0, 0))],
        out_specs=row(D_MODEL),
        out_shape=jax.ShapeDtypeStruct((m, D_MODEL), F32),
        compiler_params=_cparams(("parallel",)),
        name="merge",
    )(x, oa, ob, oc, od, *([gates] * N_BRANCH), wbr, wo)


def _spw_body(za_ref, zb_ref, gcs_ref, scs_ref, lcs_ref, lh_ref, gcw_ref, scw_ref, lcw_ref, lcb_ref, wa_ref,
              ba_ref, wx_ref, bx_ref, lam_ref, act_ref, gcs_o, oc_ref, scs_o, od_ref, lh_o, lcs_o):
    w = A_QKV_W
    x = za_ref[:, 0:w]
    y = gcw_ref[GDN_CONV - 1:GDN_CONV, :] * x
    for i in range(GDN_CONV - 1):
        y = y + gcw_ref[i:i + 1, :] * gcs_ref[:, i * w:(i + 1) * w]
    act_ref[...] = jax.nn.silu(y)
    gcs_o[:, 0:(GDN_CONV - 2) * w] = gcs_ref[:, w:(GDN_CONV - 1) * w]
    gcs_o[:, (GDN_CONV - 2) * w:(GDN_CONV - 1) * w] = x

    w = MIX_W
    gate_b = zb_ref[:, ZB_C:ZB_C + w]
    ci = zb_ref[:, ZB_C + w:ZB_C + 2 * w] * zb_ref[:, ZB_C + 2 * w:ZB_C + 3 * w]
    u = scw_ref[SC_CONV - 1:SC_CONV, :] * ci
    for i in range(SC_CONV - 1):
        u = u + scw_ref[i:i + 1, :] * scs_ref[:, i * w:(i + 1) * w]
    oc_ref[...] = gate_b * u
    scs_o[:, 0:(SC_CONV - 2) * w] = scs_ref[:, w:(SC_CONV - 1) * w]
    scs_o[:, (SC_CONV - 2) * w:(SC_CONV - 1) * w] = ci

    xd = zb_ref[:, ZB_D:ZB_D + w]
    gate_d = zb_ref[:, ZB_D + w:ZB_D + 2 * w]
    xc = lcw_ref[LRU_CONV - 1:LRU_CONV, :] * xd
    for i in range(LRU_CONV - 1):
        xc = xc + lcw_ref[i:i + 1, :] * lcs_ref[:, i * w:(i + 1) * w]
    xc = xc + lcb_ref[...]
    a, bt = _lru_gates(xc, wa_ref, ba_ref, wx_ref, bx_ref, lam_ref)
    hnew = a * lh_ref[...] + bt
    lh_o[...] = hnew
    od_ref[...] = hnew * jax.nn.gelu(gate_d)
    lcs_o[:, 0:(LRU_CONV - 2) * w] = lcs_ref[:, w:(LRU_CONV - 1) * w]
    lcs_o[:, (LRU_CONV - 2) * w:(LRU_CONV - 1) * w] = xd


def _sample_pointwise(za, zb, gcs, scs, lcs, lh, gcw, scw, lcw, lcb, wa, ba, wx, bx, lam):
    bd = za.shape[0]
    shapes = [(bd, A_QKV_W), gcs.shape, (bd, MIX_W), scs.shape, (bd, MIX_W), lh.shape, lcs.shape]
    return pl.pallas_call(
        _spw_body,
        out_shape=[jax.ShapeDtypeStruct(s, F32) for s in shapes],
        compiler_params=pltpu.CompilerParams(vmem_limit_bytes=VMEM_LIMIT),
        name="sample_pointwise",
    )(za, zb, gcs, scs, lcs, lh, gcw, scw, lcw, lcb, wa, ba, wx, bx, lam)


def _col(row, diag):
    return jnp.sum(jnp.where(diag, row, 0.0), axis=1, keepdims=True)


def _sstep_body(act_ref, za_ref, zb_ref, ba_ref, s_ref, alog_ref, dtb_ref, nw_ref,
                k0_ref, v0_ref, k1_ref, v1_ref, k2_ref, v2_ref, bb_ref, bn_ref, sh0_ref, sh1_ref, sh2_ref,
                oa_ref, ob_ref, s_out_ref, w0_ref, w1_ref, w2_ref):
    del sh0_ref, sh1_ref, sh2_ref
    e = GDN_DK
    for g, w_ref in enumerate((w0_ref, w1_ref, w2_ref)):
        for h in range(DIL_GROUP_HEADS):
            c0 = g * MIX_W + h * e
            w_ref[0, 0, h:h + 1, :] = zb_ref[:, ZB_K + c0:ZB_K + c0 + e]
            w_ref[1, 0, h:h + 1, :] = zb_ref[:, ZB_V + c0:ZB_V + c0 + e]

    ri = lax.broadcasted_iota(jnp.int32, (e, e), 0)
    ci = lax.broadcasted_iota(jnp.int32, (e, e), 1)
    diag = ri == ci

    ba = ba_ref[...]
    beta_all = jax.nn.sigmoid(ba)
    g_all = -jnp.exp(alog_ref[...]) * _softplus(ba + dtb_ref[...])
    hd = range(GDN_HEADS)
    q = [act_ref[:, h * e:(h + 1) * e] for h in hd]
    k = [act_ref[:, GDN_HEADS * e + h * e:GDN_HEADS * e + (h + 1) * e] for h in hd]
    v = [act_ref[:, 2 * GDN_HEADS * e + h * GDN_DV:2 * GDN_HEADS * e + (h + 1) * GDN_DV] for h in hd]
    q = [q[h] * lax.rsqrt(jnp.sum(q[h] * q[h], axis=-1, keepdims=True) + NORM_EPS) * (e ** -0.5) for h in hd]
    k = [k[h] * lax.rsqrt(jnp.sum(k[h] * k[h], axis=-1, keepdims=True) + NORM_EPS) for h in hd]
    beta = [beta_all[:, h:h + 1] for h in hd]
    eg = [jnp.exp(g_all[:, GDN_HEADS + h:GDN_HEADS + h + 1]) for h in hd]
    s = [s_ref[h] for h in hd]
    kcol = [_col(k[h], diag) for h in hd]
    qcol = [_col(q[h], diag) for h in hd]
    ks = [jnp.sum(kcol[h] * s[h], axis=0, keepdims=True) for h in hd]
    qs = [jnp.sum(qcol[h] * s[h], axis=0, keepdims=True) for h in hd]
    v_new = [beta[h] * v[h] - (beta[h] * eg[h]) * ks[h] for h in hd]
    qk = [jnp.sum(q[h] * k[h], axis=-1, keepdims=True) for h in hd]
    for h in hd:
        o = eg[h] * qs[h] + qk[h] * v_new[h]
        s_out_ref[h] = s[h] * eg[h] + kcol[h] * v_new[h]
        hv = slice(h * GDN_DV, (h + 1) * GDN_DV)
        oa_ref[:, hv] = _rms(o, nw_ref[...]) * jax.nn.silu(za_ref[:, A_QKV_W + h * GDN_DV:A_QKV_W + (h + 1) * GDN_DV])

    kv = ((k0_ref, v0_ref), (k1_ref, v1_ref), (k2_ref, v2_ref))
    pairs = [(h, g) for h in range(DIL_GROUP_HEADS) for g in range(DIL_GROUPS)]
    c0 = {(h, g): g * MIX_W + h * e for h, g in pairs}
    q = {p: zb_ref[:, ZB_Q + c0[p]:ZB_Q + c0[p] + e] for p in pairs}
    kn = {p: zb_ref[:, ZB_K + c0[p]:ZB_K + c0[p] + e] for p in pairs}
    vn = {p: zb_ref[:, ZB_V + c0[p]:ZB_V + c0[p] + e] for p in pairs}
    sb = {(h, g): jnp.sum(kv[g][0][:, h, :] * q[(h, g)], axis=-1, keepdims=True) for h, g in pairs}
    sn = {p: jnp.sum(kn[p] * q[p], axis=-1, keepdims=True) for p in pairs}
    for h, g in pairs:
        col = g * DIL_GROUP_HEADS + h
        sb[(h, g)] = sb[(h, g)] * (e ** -0.5) + bb_ref[:, col:col + 1]
        sn[(h, g)] = sn[(h, g)] * (e ** -0.5) + bn_ref[:, col:col + 1]
    m = {p: jnp.maximum(jnp.max(sb[p], axis=0, keepdims=True), sn[p]) for p in pairs}
    pb = {p: jnp.exp(sb[p] - m[p]) for p in pairs}
    pn = {p: jnp.exp(sn[p] - m[p]) for p in pairs}
    l = {p: jnp.sum(pb[p], axis=0, keepdims=True) + pn[p] for p in pairs}
    out = {(h, g): (jnp.sum(pb[(h, g)] * kv[g][1][:, h, :], axis=0, keepdims=True) + pn[(h, g)] * vn[(h, g)]) / l[(h, g)]
           for h, g in pairs}
    lse = {p: m[p] + jnp.log(l[p]) for p in pairs}
    for h in range(DIL_GROUP_HEADS):
        lses = [lse[(h, g)] for g in range(DIL_GROUPS)]
        mm = functools.reduce(jnp.maximum, lses)
        es = [jnp.exp(x - mm) for x in lses]
        den = functools.reduce(lambda p, q_: p + q_, es)
        ob_ref[:, h * e:(h + 1) * e] = functools.reduce(
            lambda p, q_: p + q_, [(es[g] / den) * out[(h, g)] for g in range(DIL_GROUPS)])


def _sample_step(act3, za3, zb3, ba3, s_in, alog_row, dtb_row, nw, caches, shifted, layer, bias_buf, bias_new):
    bd = act3.shape[0]
    n_fixed_in = 16
    win_specs = [pl.BlockSpec((None, None, 2, 1, DIL_GROUP_HEADS, DIL_HEAD_DIM),
                              lambda i, rows=buf.shape[3]: (layer, i, 0, rows - 1, 0, 0)) for buf in shifted]

    def vec(w):
        return pl.BlockSpec((None, 1, w), lambda i: (i, 0, 0))

    def full(shape):
        return pl.BlockSpec(shape, lambda i: (0,) * len(shape))

    cache_specs, cache_args = [], []
    for cch in caches:
        for kvi in range(2):
            cache_specs.append(pl.BlockSpec((None, None, None, DIL_BLOCK, None, DIL_GROUP_HEADS, DIL_HEAD_DIM),
                                            lambda i, kvi=kvi: (layer, i, kvi, 0, 0, 0, 0)))
            cache_args.append(cch)
    st = pl.BlockSpec((None, GDN_HEADS, GDN_DK, GDN_DV), lambda i: (i, 0, 0, 0))
    return pl.pallas_call(
        _sstep_body,
        grid=(bd,),
        in_specs=[vec(A_QKV_W), vec(ZA_W), vec(ZB_W), vec(BA_W), st,
                  full((1, BA_W)), full((1, BA_W)), full((1, GDN_DV))] + cache_specs +
                 [full((DIL_BLOCK, DIL_HEADS)), full((1, DIL_HEADS))] +
                 [pl.BlockSpec(memory_space=pl.ANY)] * DIL_GROUPS,
        out_specs=[vec(MIX_W), vec(MIX_W), st] + win_specs,
        out_shape=[jax.ShapeDtypeStruct((bd, 1, MIX_W), F32), jax.ShapeDtypeStruct((bd, 1, MIX_W), F32),
                   jax.ShapeDtypeStruct(s_in.shape, F32)] + [jax.ShapeDtypeStruct(buf.shape, F32) for buf in shifted],
        input_output_aliases={n_fixed_in + g: 3 + g for g in range(DIL_GROUPS)},
        compiler_params=_cparams(("parallel",)),
        name="sample_step",
    )(act3, za3, zb3, ba3, s_in, alog_row, dtb_row, nw, *cache_args, bias_buf, bias_new, *shifted)


def _shift_body(a_ref, b_ref, o_ref, *, rb):
    o_ref[:, pl.ds(0, rb - 1)] = a_ref[:, pl.ds(1, rb - 1)]
    o_ref[:, pl.ds(rb - 1, 1)] = b_ref[...]


def _shift_cache(cache):
    depth, bd, _, rows, nh, e = cache.shape
    rb = min(rows, 512)
    return pl.pallas_call(
        functools.partial(_shift_body, rb=rb),
        grid=(depth, bd, rows // rb),
        in_specs=[
            pl.BlockSpec((None, None, 2, rb, nh, e), lambda l, i, j: (l, i, 0, j, 0, 0)),
            pl.BlockSpec((None, None, 2, 1, nh, e), lambda l, i, j: (l, i, 0, jnp.minimum((j + 1) * rb, rows - 1), 0, 0)),
        ],
        out_specs=pl.BlockSpec((None, None, 2, rb, nh, e), lambda l, i, j: (l, i, 0, j, 0, 0)),
        out_shape=jax.ShapeDtypeStruct(cache.shape, F32),
        compiler_params=_cparams(("parallel", "parallel", "parallel")),
        name="shift_cache",
    )(cache, cache)


def _t5_bucket(dist):
    exact = REL_BUCKETS // 2
    d = np.maximum(dist, 1).astype(np.float32)
    large = exact + (np.log(d / exact) / math.log(REL_MAX_DIST / exact) * (REL_BUCKETS - exact)).astype(np.int32)
    return np.where(dist < exact, dist, np.minimum(large, REL_BUCKETS - 1)).astype(np.int32)


def _prompt_bias(rel_bias):
    blk = DIL_BLOCK
    period = 4 * blk
    out = []
    for gi, (window, dilation) in enumerate(DIL_PATTERNS):
        n_off = window // dilation
        tab = rel_bias[:, gi * DIL_GROUP_HEADS:(gi + 1) * DIL_GROUP_HEADS]
        vals = tab[_t5_bucket(np.arange(n_off + 1) * dilation)].T.astype(F32)
        w = jnp.full((DIL_GROUP_HEADS, period), -jnp.inf, F32)
        w = lax.dynamic_update_slice(w, vals, (0, blk - 1))
        r = jnp.tile(w, (1, blk + 1))[:, :blk * (period + 1)].reshape(DIL_GROUP_HEADS, blk, period + 1)
        out.append(r[:, :, 0:2 * blk][:, :, ::-1])
    return jnp.stack(out, axis=0)


def _step_bias(rel_bias):
    cols_buf, cols_new = [], []
    for gi, (window, dilation) in enumerate(DIL_PATTERNS):
        n_off = window // dilation
        tab = rel_bias[:, gi * DIL_GROUP_HEADS:(gi + 1) * DIL_GROUP_HEADS]
        j = n_off - np.arange(n_off)
        cols_buf.append(tab[_t5_bucket(j * dilation)])
        cols_new.append(tab[_t5_bucket(np.zeros((1,), np.int64))])
    return jnp.concatenate(cols_buf, axis=1).astype(F32), jnp.concatenate(cols_new, axis=1).astype(F32)


def _block_diag(w):
    n, c, _ = w.shape
    eye = jnp.eye(n, dtype=w.dtype)
    return (eye[:, None, :, None] * w[:, :, None, :]).reshape(n * c, n * c)


def _layer_weights(p, l, final_norm):
    w_in = p['w_in'][l]
    b0, c0 = A_W, A_W + B_W
    zg0 = A_QKV_W
    ba0 = A_QKV_W + GDN_HEADS * GDN_DV
    w_all = jnp.concatenate([w_in[:, 0:ba0], w_in[:, b0:], p['w_gate'][l]], axis=1).astype(BF16)
    b_all = p['b_gate'][l][None]
    w_ba = jnp.pad(w_in[:, ba0:b0], ((0, 0), (0, BA_W - 2 * GDN_HEADS))).astype(BF16)
    pad_row = lambda v: jnp.pad(v, (GDN_HEADS, BA_W - 2 * GDN_HEADS))[None]
    row = lambda v: v[None].astype(F32)
    del zg0, c0
    return dict(
        fnw=row(final_norm), last=(l == DEPTH - 1),
        n1=row(p['norm_ffn1'][l]), gu1=p['ffn1_w_gu'][l].astype(BF16), dn1=p['ffn1_w_down'][l].astype(BF16),
        nm=row(p['norm_mix'][l]), w_all=w_all, b_all=b_all, w_ba=w_ba,
        gcw=p['gdn_conv_w'][l], alog=pad_row(p['gdn_a_log'][l]), dtb=pad_row(p['gdn_dt_bias'][l]),
        gnw=row(p['gdn_norm_w'][l]), scw=p['sc_conv_w'][l], lcw=p['lru_conv_w'][l], lcb=row(p['lru_conv_b'][l]),
        wa=_block_diag(p['lru_wa'][l]).astype(BF16), ba=row(p['lru_ba'][l]),
        wx=_block_diag(p['lru_wx'][l]).astype(BF16), bx=row(p['lru_bx'][l]), lam=row(p['lru_lambda'][l]),
        wbr=p['w_branch'][l].astype(BF16), wo=p['w_o'][l].astype(BF16),
        n2=row(p['norm_ffn2'][l]), gu2=p['ffn2_w_gu'][l].astype(BF16), dn2=p['ffn2_w_down'][l].astype(BF16),
    )


def _prompt_layer(x, w, bias, b, t):
    m = b * t
    x = _ffn(x, w['n1'], w['gu1'], w['dn1'], w['fnw'])
    za, zb, gates, ba = _proj(x, w['nm'], w['w_all'], w['b_all'], w['w_ba'], BF16)
    za3 = za.reshape(b, t, ZA_W)
    zb3 = zb.reshape(b, t, ZB_W)
    o_a, s_gdn = _gdn_prompt(za3, ba.reshape(b, t, BA_W), w['gcw'], w['alog'], w['dtb'], w['gnw'])
    o_b = _dil_prompt(zb3, bias)
    o_c, o_d, sc_st, lru_h = _cd_prompt(zb3, w['scw'], w['lcw'], w['lcb'], w['wa'], w['ba'], w['wx'], w['bx'], w['lam'])
    x = _merge(x, o_a.reshape(m, MIX_W), o_b.reshape(m, MIX_W), o_c.reshape(m, MIX_W), o_d.reshape(m, MIX_W),
               gates, w['wbr'], w['wo'])
    x = _ffn(x, w['n2'], w['gu2'], w['dn2'], w['fnw'], final=w['last'])
    gdn_conv = za3[:, t - (GDN_CONV - 1):, 0:A_QKV_W].astype(F32)
    bufs = []
    for gi, (window, _) in enumerate(DIL_PATTERNS):
        rows = min(window, t)
        k = zb3[:, t - rows:, ZB_K + gi * MIX_W:ZB_K + (gi + 1) * MIX_W]
        v = zb3[:, t - rows:, ZB_V + gi * MIX_W:ZB_V + (gi + 1) * MIX_W]
        bufs.append(jnp.stack([k, v], axis=1).reshape(b, 2, rows, DIL_GROUP_HEADS, DIL_HEAD_DIM).astype(F32))
    sc_conv = sc_st[:, SUBLANES - (SC_CONV - 1):]
    lru_conv = zb3[:, t - (LRU_CONV - 1):, ZB_D:ZB_D + MIX_W].astype(F32)
    return x, (s_gdn, gdn_conv, bufs[0], bufs[1], bufs[2], sc_conv, lru_h[:, 0], lru_conv)


def _sample_layer(x, w, l, st, shifted, step_bias):
    bd = x.shape[0]
    s_gdn, gdn_conv, caches, sc_conv, lru_h, lru_conv = st
    x = _ffn(x, w['n1'], w['gu1'], w['dn1'], w['fnw'])
    za, zb, gates, ba = _proj(x, w['nm'], w['w_all'], w['b_all'], w['w_ba'], F32)
    act, gdn_conv_new, o_c, sc_new, o_d, lru_h_new, lru_conv_new = _sample_pointwise(
        za, zb, gdn_conv.reshape(bd, -1), sc_conv.reshape(bd, -1), lru_conv.reshape(bd, -1), lru_h,
        w['gcw'], w['scw'], w['lcw'], w['lcb'], w['wa'], w['ba'], w['wx'], w['bx'], w['lam'])
    zb3 = zb.reshape(bd, 1, ZB_W)
    o_a, o_b, s_new, *shifted = _sample_step(act.reshape(bd, 1, A_QKV_W), za.reshape(bd, 1, ZA_W), zb3,
                                             ba.reshape(bd, 1, BA_W), s_gdn, w['alog'], w['dtb'], w['gnw'], caches,
                                             shifted, l, *step_bias)
    x = _merge(x, o_a.reshape(bd, MIX_W), o_b.reshape(bd, MIX_W), o_c, o_d, gates, w['wbr'], w['wo'])
    x = _ffn(x, w['n2'], w['gu2'], w['dn2'], w['fnw'], final=w['last'])
    new = (s_new, gdn_conv_new.reshape(gdn_conv.shape), sc_new.reshape(sc_conv.shape), lru_h_new,
           lru_conv_new.reshape(lru_conv.shape))
    return x, new, shifted


def kernel(x_prompt, x_sample, state_gdn, state_gdn_conv, cache_dil_w128, cache_dil_w512, cache_dil_w2048,
           state_sc_conv, state_lru, state_lru_conv, norm_ffn1, ffn1_w_gu, ffn1_w_down, norm_mix, w_in,
           gdn_conv_w, gdn_a_log, gdn_dt_bias, gdn_norm_w, rel_bias, sc_conv_w, lru_conv_w, lru_conv_b,
           lru_wa, lru_ba, lru_wx, lru_bx, lru_lambda, w_gate, b_gate, w_branch, w_o, norm_ffn2, ffn2_w_gu,
           ffn2_w_down, final_norm):
    p = dict(norm_ffn1=norm_ffn1, ffn1_w_gu=ffn1_w_gu, ffn1_w_down=ffn1_w_down, norm_mix=norm_mix, w_in=w_in,
             gdn_conv_w=gdn_conv_w, gdn_a_log=gdn_a_log, gdn_dt_bias=gdn_dt_bias, gdn_norm_w=gdn_norm_w,
             sc_conv_w=sc_conv_w, lru_conv_w=lru_conv_w, lru_conv_b=lru_conv_b, lru_wa=lru_wa, lru_ba=lru_ba,
             lru_wx=lru_wx, lru_bx=lru_bx, lru_lambda=lru_lambda, w_gate=w_gate, b_gate=b_gate,
             w_branch=w_branch, w_o=w_o, norm_ffn2=norm_ffn2, ffn2_w_gu=ffn2_w_gu, ffn2_w_down=ffn2_w_down)
    weights = [_layer_weights(p, l, final_norm) for l in range(DEPTH)]
    b, t, _ = x_prompt.shape
    bd = x_sample.shape[0]
    bias = _prompt_bias(rel_bias)
    step_bias = _step_bias(rel_bias)

    x = x_prompt.reshape(b * t, D_MODEL)
    p_states = []
    for l in range(DEPTH):
        x, new = _prompt_layer(x, weights[l], bias, b, t)
        p_states.append(new)
    y_prompt = x.reshape(b, t, D_MODEL)
    p_out = tuple(jnp.stack(zs, axis=0) for zs in zip(*p_states))

    caches = (cache_dil_w128, cache_dil_w512, cache_dil_w2048)
    shifted = [_shift_cache(c) for c in caches]
    strided = tuple(c.reshape(DEPTH, bd, 2, c.shape[3] // d, d, DIL_GROUP_HEADS, DIL_HEAD_DIM)
                    for c, (_, d) in zip(caches, DIL_PATTERNS))
    x = x_sample.reshape(bd, D_MODEL)
    s_states = []
    for l in range(DEPTH):
        st = (state_gdn[l], state_gdn_conv[l], strided, state_sc_conv[l], state_lru[l], state_lru_conv[l])
        x, new, shifted = _sample_layer(x, weights[l], l, st, shifted, step_bias)
        s_states.append(new)
    y_sample = x.reshape(bd, 1, D_MODEL)
    s_gdn, s_gdn_conv, s_sc_conv, s_lru, s_lru_conv = (jnp.stack(zs, axis=0) for zs in zip(*s_states))

    p_gdn, p_gdn_conv, p_w128, p_w512, p_w2048, p_sc_conv, p_lru, p_lru_conv = p_out
    return (y_prompt, y_sample, p_gdn, s_gdn, p_gdn_conv, s_gdn_conv, p_w128, shifted[0], p_w512, shifted[1],
            p_w2048, shifted[2], p_sc_conv, s_sc_conv, p_lru, s_lru, p_lru_conv, s_lru_conv)
```

```python
import functools
import math

import numpy as np
import jax
import jax.numpy as jnp
from jax import lax
from jax.experimental import pallas as pl
from jax.experimental.pallas import tpu as pltpu

D_MODEL = 1024
DEPTH = 4
MIX_W = D_MODEL // 2
D_FF = 2816
NORM_EPS = 1e-6
N_BRANCH = 4
GDN_HEADS = 4
GDN_DK = 128
GDN_DV = MIX_W // GDN_HEADS
GDN_CONV = 4
GDN_CHUNK = 64
DIL_PATTERNS = ((128, 1), (512, 4), (2048, 16))
DIL_GROUPS = len(DIL_PATTERNS)
DIL_GROUP_HEADS = 4
DIL_HEAD_DIM = MIX_W // DIL_GROUP_HEADS
DIL_HEADS = DIL_GROUPS * DIL_GROUP_HEADS
DIL_BLOCK = 128
REL_BUCKETS = 32
REL_MAX_DIST = 2048
SC_CONV = 3
LRU_BLOCKS = 8
LRU_BLOCK_W = MIX_W // LRU_BLOCKS
LRU_CONV = 4
LRU_C = 8.0
A_QKV_W = GDN_HEADS * (2 * GDN_DK + GDN_DV)
A_W = A_QKV_W + GDN_HEADS * GDN_DV + 2 * GDN_HEADS
B_W = 3 * DIL_HEADS * DIL_HEAD_DIM
C_W = 3 * MIX_W
D_W = 2 * MIX_W

LANES = 128
SUBLANES = 8
VMEM_LIMIT = 56 * 1024 * 1024

ZA_W = A_QKV_W + MIX_W
ZB_Q, ZB_K, ZB_V = 0, B_W // 3, 2 * B_W // 3
ZB_C = B_W
ZB_D = B_W + C_W
ZB_W = B_W + C_W + D_W
G_W = N_BRANCH * D_MODEL
PROJ_TN = 1024
BA_W = LANES
DIL_TB = DIL_BLOCK * max(d for _, d in DIL_PATTERNS)
DIL_UNROLL = 16

F32 = jnp.float32
BF16 = jnp.bfloat16
NT_DIMS = (((1,), (1,)), ((), ()))
TN_DIMS = (((0,), (0,)), ((), ()))


def _cparams(sem):
    return pltpu.CompilerParams(dimension_semantics=sem, vmem_limit_bytes=VMEM_LIMIT)


def _rms(x, w):
    return x * lax.rsqrt(jnp.mean(x * x, axis=-1, keepdims=True) + NORM_EPS) * w


def _softplus(x):
    return jnp.maximum(x, 0.0) + jnp.log1p(jnp.exp(-jnp.abs(x)))


def _causal_conv(xb, w_ref, tt):
    taps = w_ref.shape[0]
    y = None
    for i in range(taps):
        back = taps - 1 - i
        xs = xb if back == 0 else pltpu.roll(xb, back, 0)
        term = w_ref[i:i + 1, :] * xs[SUBLANES:SUBLANES + tt]
        y = term if y is None else y + term
    return y


def _dot(a, b):
    return jnp.dot(a.astype(BF16), b.astype(BF16), preferred_element_type=F32)


def _dot_nt(a, b):
    return lax.dot_general(a.astype(BF16), b.astype(BF16), NT_DIMS, preferred_element_type=F32)


def _dot_tn(a, b):
    return lax.dot_general(a.astype(BF16), b.astype(BF16), TN_DIMS, preferred_element_type=F32)


FFN_CHUNKS = 2


def _ffn_body(x_ref, nw_ref, wgu_ref, wd_ref, fw_ref, o_ref, *, final):
    tf = D_FF // FFN_CHUNKS
    x = x_ref[...]
    h = _rms(x, nw_ref[...]).astype(BF16)
    acc = None
    for c in range(FFN_CHUNKS):
        g = _dot(h, wgu_ref[:, c * tf:(c + 1) * tf])
        u = _dot(h, wgu_ref[:, D_FF + c * tf:D_FF + (c + 1) * tf])
        d = _dot(jax.nn.silu(g) * u, wd_ref[c * tf:(c + 1) * tf, :])
        acc = d if acc is None else acc + d
    y = x + 0.5 * acc
    o_ref[...] = _rms(y, fw_ref[...]) if final else y


def _ffn(x, nw, w_gu, w_down, fw, final=False):
    m = x.shape[0]
    tm = min(m, 512)
    resident = pl.Buffered(1)
    return pl.pallas_call(
        functools.partial(_ffn_body, final=final),
        grid=(m // tm,),
        in_specs=[
            pl.BlockSpec((tm, D_MODEL), lambda i: (i, 0)),
            pl.BlockSpec((1, D_MODEL), lambda i: (0, 0)),
            pl.BlockSpec((D_MODEL, 2 * D_FF), lambda i: (0, 0), pipeline_mode=resident),
            pl.BlockSpec((D_FF, D_MODEL), lambda i: (0, 0), pipeline_mode=resident),
            pl.BlockSpec((1, D_MODEL), lambda i: (0, 0)),
        ],
        out_specs=pl.BlockSpec((tm, D_MODEL), lambda i: (i, 0)),
        out_shape=jax.ShapeDtypeStruct((m, D_MODEL), F32),
        compiler_params=_cparams(("parallel",)),
        name="ffn",
    )(x, nw, w_gu, w_down, fw)


def _proj_body(x_ref, nw_ref, w_ref, b_ref, wba_ref, za_ref, zb_ref, g_ref, ba_ref, h_ref, *, ja, jb):
    j = pl.program_id(1)

    @pl.when(j == 0)
    def _():
        h = _rms(x_ref[...], nw_ref[...]).astype(BF16)
        h_ref[...] = h
        ba_ref[...] = _dot(h, wba_ref[...])

    @pl.when(j < ja)
    def _():
        za_ref[...] = _dot(h_ref[...], w_ref[...]).astype(za_ref.dtype)

    @pl.when((j >= ja) & (j < jb))
    def _():
        zb_ref[...] = _dot(h_ref[...], w_ref[...]).astype(zb_ref.dtype)

    @pl.when(j >= jb)
    def _():
        g_ref[...] = jax.nn.sigmoid(_dot(h_ref[...], w_ref[...]) + b_ref[...]).astype(BF16)


def _proj(x, nw, w, b, w_ba, act_dtype):
    m = x.shape[0]
    tm = min(m, 1024)
    tn = PROJ_TN
    ja, jb = ZA_W // tn, (ZA_W + ZB_W) // tn
    nj = (ZA_W + ZB_W + G_W) // tn
    return pl.pallas_call(
        functools.partial(_proj_body, ja=ja, jb=jb),
        grid=(m // tm, nj),
        in_specs=[
            pl.BlockSpec((tm, D_MODEL), lambda i, j: (i, 0)),
            pl.BlockSpec((1, D_MODEL), lambda i, j: (0, 0)),
            pl.BlockSpec((D_MODEL, tn), lambda i, j: (0, j)),
            pl.BlockSpec((1, tn), lambda i, j: (0, jnp.maximum(j - jb, 0))),
            pl.BlockSpec((D_MODEL, BA_W), lambda i, j: (0, 0)),
        ],
        out_specs=[
            pl.BlockSpec((tm, tn), lambda i, j: (i, jnp.minimum(j, ja - 1))),
            pl.BlockSpec((tm, tn), lambda i, j: (i, jnp.clip(j - ja, 0, jb - ja - 1))),
            pl.BlockSpec((tm, tn), lambda i, j: (i, jnp.maximum(j - jb, 0))),
            pl.BlockSpec((tm, BA_W), lambda i, j: (i, 0)),
        ],
        out_shape=[
            jax.ShapeDtypeStruct((m, ZA_W), act_dtype),
            jax.ShapeDtypeStruct((m, ZB_W), act_dtype),
            jax.ShapeDtypeStruct((m, G_W), BF16),
            jax.ShapeDtypeStruct((m, BA_W), F32),
        ],
        scratch_shapes=[pltpu.VMEM((tm, D_MODEL), BF16)],
        compiler_params=_cparams(("parallel", "arbitrary")),
        name="proj",
    )(x, nw, w, b, w_ba)


def _gdn_body(qkv_ref, zg_ref, ba_ref, cw_ref, alog_ref, dtb_ref, nw_ref, o_ref, s_out_ref,
              xbuf, act, beta_s, gc_s, u_s, wq_s, at_s, kd_s, s_s, *, tt, nb):
    t = pl.program_id(0)
    c = GDN_CHUNK
    nchunk = tt // c

    @pl.when(t == 0)
    def _():
        for b in range(nb):
            xbuf[b, pl.ds(0, SUBLANES), :] = jnp.zeros((SUBLANES, A_QKV_W), F32)
        s_s[...] = jnp.zeros_like(s_s)

    rowmod = lax.broadcasted_iota(jnp.int32, (tt, BA_W), 0) & (c - 1)
    for b in range(nb):
        xbuf[b, pl.ds(SUBLANES, tt), :] = qkv_ref[b].astype(F32)
        act[b] = jax.nn.silu(_causal_conv(xbuf[b], cw_ref, tt))
        xbuf[b, pl.ds(0, SUBLANES), :] = xbuf[b, pl.ds(tt, SUBLANES), :]

        ba = ba_ref[b]
        beta_s[b] = jax.nn.sigmoid(ba)
        g = -jnp.exp(alog_ref[...]) * _softplus(ba + dtb_ref[...])
        sh = 1
        while sh < c:
            g = g + jnp.where(rowmod >= sh, pltpu.roll(g, sh, 0), 0.0)
            sh *= 2
        gc_s[b] = g

    ri = lax.broadcasted_iota(jnp.int32, (c, c), 0)
    ci = lax.broadcasted_iota(jnp.int32, (c, c), 1)
    causal = ri >= ci
    strict = ri > ci
    diag = ri == ci
    eye = jnp.where(diag, 1.0, 0.0).astype(F32)

    heads = range(GDN_HEADS)

    def prep(i, carry):
        chains = [(b, 2 * i + j, h) for b in range(nb) for j in range(2) for h in heads]
        n = len(chains)
        rows_l, gcol_l, decay_l, q_l, k_l, kb_l, vb_l = [], [], [], [], [], [], []
        for b, ch, h in chains:
            rows = pl.ds(pl.multiple_of(ch * c, c), c)
            gcol = gc_s[b, rows, GDN_HEADS + h:GDN_HEADS + h + 1]
            grow = jnp.sum(jnp.where(diag, gcol, 0.0), axis=0, keepdims=True)
            decay = jnp.exp(jnp.where(causal, gcol - grow, -jnp.inf))
            q = act[b, rows, h * GDN_DK:(h + 1) * GDN_DK]
            k = act[b, rows, GDN_HEADS * GDN_DK + h * GDN_DK:GDN_HEADS * GDN_DK + (h + 1) * GDN_DK]
            v = act[b, rows, 2 * GDN_HEADS * GDN_DK + h * GDN_DV:2 * GDN_HEADS * GDN_DK + (h + 1) * GDN_DV]
            q = q * lax.rsqrt(jnp.sum(q * q, axis=-1, keepdims=True) + NORM_EPS) * (GDN_DK ** -0.5)
            k = k * lax.rsqrt(jnp.sum(k * k, axis=-1, keepdims=True) + NORM_EPS)
            beta = beta_s[b, rows, h:h + 1]
            rows_l.append(rows)
            gcol_l.append(gcol)
            decay_l.append(decay)
            q_l.append(q)
            k_l.append(k)
            kb_l.append(k * beta)
            vb_l.append(v * beta)
        kq_l = [_dot_nt(jnp.concatenate([kb_l[j], q_l[j]], axis=0), k_l[j]) for j in range(n)]
        for j, (b, ch, h) in enumerate(chains):
            at_s[b, rows_l[j], h * LANES:h * LANES + c] = kq_l[j][c:] * decay_l[j]
        qq = [-jnp.where(strict, kq_l[j][:c] * decay_l[j], 0.0) for j in range(n)]
        yy = [eye + qq[j] for j in range(n)]
        qq = [_dot(qq[j], qq[j]) for j in range(n)]
        for _ in range(int(math.log2(c)) - 2):
            yq = [_dot(jnp.concatenate([yy[j], qq[j]], axis=0), qq[j]) for j in range(n)]
            yy = [yy[j] + yq[j][:c] for j in range(n)]
            qq = [yq[j][c:] for j in range(n)]
        yq = [_dot(yy[j], qq[j]) for j in range(n)]
        tinv = [yy[j] + yq[j] for j in range(n)]
        egc = [jnp.exp(gcol_l[j]) for j in range(n)]
        sol = [_dot(tinv[j], jnp.concatenate([vb_l[j], kb_l[j] * egc[j]], axis=1)) for j in range(n)]
        for j, (b, ch, h) in enumerate(chains):
            hs = slice(h * GDN_DK, (h + 1) * GDN_DK)
            u_s[b, rows_l[j], h * GDN_DV:(h + 1) * GDN_DV] = sol[j][:, :GDN_DV]
            wq_s[b, pl.ds(pl.multiple_of(ch * 2 * c, 2 * c), c), hs] = sol[j][:, GDN_DV:]
            wq_s[b, pl.ds(pl.multiple_of(ch * 2 * c + c, c), c), hs] = q_l[j] * egc[j]
            kd_s[b, rows_l[j], hs] = k_l[j] * jnp.exp(gcol_l[j][c - 1:c, :] - gcol_l[j])
        return carry

    lax.fori_loop(0, nchunk // 2, prep, 0)

    def step(ch, carry):
        rows = pl.ds(pl.multiple_of(ch * c, c), c)
        wq_rows = pl.ds(pl.multiple_of(ch * 2 * c, 2 * c), 2 * c)
        chains = [(b, h) for b in range(nb) for h in heads]
        hs = {h: slice(h * GDN_DK, (h + 1) * GDN_DK) for h in heads}
        hv = {h: slice(h * GDN_DV, (h + 1) * GDN_DV) for h in heads}
        glast = [jnp.exp(gc_s[b, pl.ds(ch * c + c - 1, 1), :]) for b in range(nb)]
        st = [s_s[b, h] for b, h in chains]
        ws = [_dot(wq_s[b, wq_rows, hs[h]], st[j]) for j, (b, h) in enumerate(chains)]
        v_new = [u_s[b, rows, hv[h]] - ws[j][:c] for j, (b, h) in enumerate(chains)]
        av = [_dot(at_s[b, rows, h * LANES:h * LANES + c], v_new[j]) for j, (b, h) in enumerate(chains)]
        kv = [_dot_tn(kd_s[b, rows, hs[h]], v_new[j]) for j, (b, h) in enumerate(chains)]
        for j, (b, h) in enumerate(chains):
            s_s[b, h] = st[j] * glast[b][:, GDN_HEADS + h:GDN_HEADS + h + 1] + kv[j]
            o = _rms(ws[j][c:] + av[j], nw_ref[...]) * jax.nn.silu(zg_ref[b, rows, hv[h]].astype(F32))
            o_ref[b, rows, hv[h]] = o
        return carry

    lax.fori_loop(0, nchunk, step, 0)

    @pl.when(t == pl.num_programs(0) - 1)
    def _():
        s_out_ref[...] = s_s[...]


def _gdn_prompt(za3, ba3, cw, alog_row, dtb_row, nw):
    b, t, _ = za3.shape
    tt = min(t, 256)
    return pl.pallas_call(
        functools.partial(_gdn_body, tt=tt, nb=b),
        grid=(t // tt,),
        in_specs=[
            pl.BlockSpec((b, tt, A_QKV_W), lambda j: (0, j, 0)),
            pl.BlockSpec((b, tt, MIX_W), lambda j: (0, j, A_QKV_W // MIX_W)),
            pl.BlockSpec((b, tt, BA_W), lambda j: (0, j, 0)),
            pl.BlockSpec((GDN_CONV, A_QKV_W), lambda j: (0, 0)),
            pl.BlockSpec((1, BA_W), lambda j: (0, 0)),
            pl.BlockSpec((1, BA_W), lambda j: (0, 0)),
            pl.BlockSpec((1, GDN_DV), lambda j: (0, 0)),
        ],
        out_specs=[
            pl.BlockSpec((b, tt, MIX_W), lambda j: (0, j, 0)),
            pl.BlockSpec((b, GDN_HEADS, GDN_DK, GDN_DV), lambda j: (0, 0, 0, 0)),
        ],
        out_shape=[
            jax.ShapeDtypeStruct((b, t, MIX_W), F32),
            jax.ShapeDtypeStruct((b, GDN_HEADS, GDN_DK, GDN_DV), F32),
        ],
        scratch_shapes=[
            pltpu.VMEM((b, tt + SUBLANES, A_QKV_W), F32),
            pltpu.VMEM((b, tt, A_QKV_W), F32),
            pltpu.VMEM((b, tt, BA_W), F32),
            pltpu.VMEM((b, tt, BA_W), F32),
            pltpu.VMEM((b, tt, MIX_W), F32),
            pltpu.VMEM((b, 2 * tt, MIX_W), F32),
            pltpu.VMEM((b, tt, GDN_HEADS * LANES), F32),
            pltpu.VMEM((b, tt, MIX_W), F32),
            pltpu.VMEM((b, GDN_HEADS, GDN_DK, GDN_DV), F32),
        ],
        compiler_params=_cparams(("arbitrary",)),
        name="gdn_prompt",
    )(za3, za3, ba3, cw, alog_row, dtb_row, nw)


def _dil_body(*refs):
    qkv = refs[:3 * DIL_GROUPS]
    bias_ref, o_ref = refs[3 * DIL_GROUPS], refs[3 * DIL_GROUPS + 1]
    kvbuf = refs[3 * DIL_GROUPS + 2:3 * DIL_GROUPS + 2 + 2 * DIL_GROUPS]
    og, lg, qb = refs[-3], refs[-2], refs[-1]
    n = pl.program_id(2)
    e = DIL_HEAD_DIM
    blk = DIL_BLOCK
    tb = DIL_TB
    for g, (_, d) in enumerate(DIL_PATTERNS):
        q_ref, k_ref, v_ref = qkv[3 * g:3 * g + 3]
        kb, vb = kvbuf[2 * g], kvbuf[2 * g + 1]
        tail = blk * d

        @pl.when(n == 0)
        def _(kb=kb, vb=vb, tail=tail):
            kb[pl.ds(0, tail), :] = jnp.zeros((tail, e), F32)
            vb[pl.ds(0, tail), :] = jnp.zeros((tail, e), F32)

        kb[pl.ds(tail, tb), :] = k_ref[...].astype(F32)
        vb[pl.ds(tail, tb), :] = v_ref[...].astype(F32)
        qb[...] = q_ref[...].astype(F32)
        bp = bias_ref[g, :, 0:blk]
        bc = bias_ref[g, :, blk:2 * blk]
        shift = int(math.log2(d))

        def rows_at(base, d=d):
            return pl.ds(base, blk) if d == 1 else pl.ds(base, blk, stride=d)

        def body(it, carry, d=d, g=g, kb=kb, vb=vb, bp=bp, bc=bc, shift=shift, tail=tail,
                 rows_at=rows_at):
            us = range(DIL_UNROLL)
            idx = [it * DIL_UNROLL + u for u in us]
            sub = [i >> shift for i in idx]
            base = [sub[u] * tail + (idx[u] & (d - 1)) for u in us]
            q = [qb[rows_at(base[u]), :].astype(BF16) for u in us]
            sp = [_dot_nt(q[u], kb[rows_at(base[u]), :]) for u in us]
            sc = [_dot_nt(q[u], kb[rows_at(base[u] + tail), :]) for u in us]
            pp, pc, l, lse = [], [], [], []
            for u in us:
                spu = jnp.where((n > 0) | (sub[u] > 0), sp[u] * (e ** -0.5) + bp, -jnp.inf)
                scu = sc[u] * (e ** -0.5) + bc
                m = jnp.max(jnp.maximum(spu, scu), axis=-1, keepdims=True)
                ppu = jnp.exp(spu - m)
                pcu = jnp.exp(scu - m)
                lu = jnp.sum(ppu + pcu, axis=-1, keepdims=True)
                pp.append(ppu)
                pc.append(pcu)
                l.append(lu)
                lse.append(m + jnp.log(lu))
            op = [_dot(pp[u], vb[rows_at(base[u]), :]) for u in us]
            oc = [_dot(pc[u], vb[rows_at(base[u] + tail), :]) for u in us]
            for u in us:
                og[g, rows_at(base[u]), :] = (op[u] + oc[u]) / l[u]
                lg[g, rows_at(base[u]), :] = jnp.broadcast_to(lse[u], (blk, e))
            return carry

        lax.fori_loop(0, tb // blk // DIL_UNROLL, body, 0)
        kb[pl.ds(0, tail), :] = kb[pl.ds(tb, tail), :]
        vb[pl.ds(0, tail), :] = vb[pl.ds(tb, tail), :]

    lses = [lg[g] for g in range(DIL_GROUPS)]
    m = functools.reduce(jnp.maximum, lses)
    es = [jnp.exp(l - m) for l in lses]
    den = functools.reduce(lambda p, q_: p + q_, es)
    o_ref[...] = functools.reduce(lambda p, q_: p + q_, [(es[g] / den) * og[g] for g in range(DIL_GROUPS)])


def _dil_prompt(zb3, bias):
    b, t, _ = zb3.shape
    tb = DIL_TB
    e = DIL_HEAD_DIM
    specs, args = [], []
    for g in range(DIL_GROUPS):
        for off in (ZB_Q, ZB_K, ZB_V):
            cb = (off + g * MIX_W) // e
            specs.append(pl.BlockSpec((None, tb, e), lambda i, h, n, cb=cb: (i, n, cb + h)))
            args.append(zb3)
    specs.append(pl.BlockSpec((DIL_GROUPS, None, DIL_BLOCK, 2 * DIL_BLOCK), lambda i, h, n: (0, h, 0, 0)))
    scratch = []
    for _, d in DIL_PATTERNS:
        scratch += [pltpu.VMEM((DIL_BLOCK * d + tb, e), F32)] * 2
    scratch += [pltpu.VMEM((DIL_GROUPS, tb, e), F32)] * 2
    scratch += [pltpu.VMEM((tb, e), F32)]
    return pl.pallas_call(
        _dil_body,
        grid=(b, DIL_GROUP_HEADS, t // tb),
        in_specs=specs,
        out_specs=pl.BlockSpec((None, tb, e), lambda i, h, n: (i, n, h)),
        out_shape=jax.ShapeDtypeStruct((b, t, MIX_W), F32),
        scratch_shapes=scratch,
        compiler_params=_cparams(("parallel", "parallel", "arbitrary")),
        name="dil_prompt",
    )(*args, bias)


def _lru_gates(xc, wa_ref, ba_ref, wx_ref, bx_ref, lam_ref):
    xb = xc.astype(BF16)
    r = jax.nn.sigmoid(_dot(xb, wa_ref[...]) + ba_ref[...])
    i = jax.nn.sigmoid(_dot(xb, wx_ref[...]) + bx_ref[...])
    log_a = -LRU_C * r * _softplus(-lam_ref[...])
    a = jnp.exp(log_a)
    bt = jnp.sqrt(-jnp.tanh(log_a) * (jnp.exp(2.0 * log_a) + 1.0)) * i * xc
    return a, bt


def _cd_body(gb_ref, gc_ref, xi_ref, xd_ref, gd_ref, scw_ref, lcw_ref, lcb_ref, wa_ref, ba_ref, wx_ref,
             bx_ref, lam_ref, oc_ref, od_ref, scst_ref, lruh_ref, cbuf, dbuf, a_s, b_s, h_s, *, tt):
    t = pl.program_id(1)

    @pl.when(t == 0)
    def _():
        cbuf[pl.ds(0, SUBLANES), :] = jnp.zeros((SUBLANES, MIX_W), F32)
        dbuf[pl.ds(0, SUBLANES), :] = jnp.zeros((SUBLANES, MIX_W), F32)
        h_s[...] = jnp.zeros_like(h_s)

    cbuf[pl.ds(SUBLANES, tt), :] = gc_ref[...].astype(F32) * xi_ref[...].astype(F32)
    oc_ref[...] = gb_ref[...].astype(F32) * _causal_conv(cbuf[...], scw_ref, tt)
    tail = cbuf[pl.ds(tt, SUBLANES), :]
    scst_ref[...] = tail
    cbuf[pl.ds(0, SUBLANES), :] = tail

    dbuf[pl.ds(SUBLANES, tt), :] = xd_ref[...].astype(F32)
    xc = _causal_conv(dbuf[...], lcw_ref, tt) + lcb_ref[...]
    dbuf[pl.ds(0, SUBLANES), :] = dbuf[pl.ds(tt, SUBLANES), :]
    a, bt = _lru_gates(xc, wa_ref, ba_ref, wx_ref, bx_ref, lam_ref)
    a_s[...] = a
    b_s[...] = bt

    def scan(g, h):
        rows = pl.ds(pl.multiple_of(g * SUBLANES, SUBLANES), SUBLANES)
        a8 = a_s[rows, :]
        b8 = b_s[rows, :]
        out = []
        for r in range(SUBLANES):
            h = a8[r:r + 1, :] * h + b8[r:r + 1, :]
            out.append(h)
        a_s[rows, :] = jnp.concatenate(out, axis=0)
        return h

    h_last = lax.fori_loop(0, tt // SUBLANES, scan, h_s[0:1, :])
    h_s[...] = jnp.broadcast_to(h_last, h_s.shape)
    lruh_ref[...] = jnp.broadcast_to(h_last, h_s.shape)
    od_ref[...] = a_s[...] * jax.nn.gelu(gd_ref[...].astype(F32))


def _cd_prompt(zb3, scw, lcw, lcb, wa, ba, wx, bx, lam):
    b, t, _ = zb3.shape
    tt = min(t, 512)
    blk = (None, tt, MIX_W)
    c0 = ZB_C // MIX_W

    def zspec(c):
        return pl.BlockSpec(blk, lambda i, j: (i, j, c))

    def full(shape):
        return pl.BlockSpec(shape, lambda i, j: (0,) * len(shape))

    st = pl.BlockSpec((None, SUBLANES, MIX_W), lambda i, j: (i, 0, 0))
    return pl.pallas_call(
        functools.partial(_cd_body, tt=tt),
        grid=(b, t // tt),
        in_specs=[zspec(c0), zspec(c0 + 1), zspec(c0 + 2), zspec(c0 + 3), zspec(c0 + 4),
                  full((SC_CONV, MIX_W)), full((LRU_CONV, MIX_W)), full((1, MIX_W)),
                  full((MIX_W, MIX_W)), full((1, MIX_W)), full((MIX_W, MIX_W)), full((1, MIX_W)),
                  full((1, MIX_W))],
        out_specs=[pl.BlockSpec(blk, lambda i, j: (i, j, 0)), pl.BlockSpec(blk, lambda i, j: (i, j, 0)), st, st],
        out_shape=[
            jax.ShapeDtypeStruct((b, t, MIX_W), F32),
            jax.ShapeDtypeStruct((b, t, MIX_W), F32),
            jax.ShapeDtypeStruct((b, SUBLANES, MIX_W), F32),
            jax.ShapeDtypeStruct((b, SUBLANES, MIX_W), F32),
        ],
        scratch_shapes=[
            pltpu.VMEM((tt + SUBLANES, MIX_W), F32),
            pltpu.VMEM((tt + SUBLANES, MIX_W), F32),
            pltpu.VMEM((tt, MIX_W), F32),
            pltpu.VMEM((tt, MIX_W), F32),
            pltpu.VMEM((SUBLANES, MIX_W), F32),
        ],
        compiler_params=_cparams(("parallel", "arbitrary")),
        name="cd_prompt",
    )(zb3, zb3, zb3, zb3, zb3, scw, lcw, lcb, wa, ba, wx, bx, lam)


def _merge_body(x_ref, oa_ref, ob_ref, oc_ref, od_ref, g0_ref, g1_ref, g2_ref, g3_ref, wbr_ref, wo_ref, o_ref):
    y = None
    for nbr, (br, g_ref) in enumerate(zip((oa_ref, ob_ref, oc_ref, od_ref), (g0_ref, g1_ref, g2_ref, g3_ref))):
        yb = _dot(br[...], wbr_ref[nbr]) * g_ref[...].astype(F32)
        y = yb if y is None else y + yb
    o_ref[...] = x_ref[...] + _dot(y, wo_ref[...])


def _merge(x, oa, ob, oc, od, gates, wbr, wo):
    m = x.shape[0]
    tm = min(m, 512)
    row = lambda w: pl.BlockSpec((tm, w), lambda i: (i, 0))
    gate_specs = [pl.BlockSpec((tm, D_MODEL), lambda i, n=n: (i, n)) for n in range(N_BRANCH)]
    return pl.pallas_call(
        _merge_body,
        grid=(m // tm,),
        in_specs=[row(D_MODEL)] + [row(MIX_W)] * N_BRANCH + gate_specs +
                 [pl.BlockSpec((N_BRANCH, MIX_W, D_MODEL), lambda i: (0, 0, 0)),
                  pl.BlockSpec((D_MODEL, D_MODEL), lambda i: (0, 0))],
        out_specs=row(D_MODEL),
        out_shape=jax.ShapeDtypeStruct((m, D_MODEL), F32),
        compiler_params=_cparams(("parallel",)),
        name="merge",
    )(x, oa, ob, oc, od, *([gates] * N_BRANCH), wbr, wo)


def _spw_body(za_ref, zb_ref, gcs_ref, scs_ref, lcs_ref, lh_ref, gcw_ref, scw_ref, lcw_ref, lcb_ref, wa_ref,
              ba_ref, wx_ref, bx_ref, lam_ref, act_ref, gcs_o, oc_ref, scs_o, od_ref, lh_o, lcs_o):
    w = A_QKV_W
    x = za_ref[:, 0:w]
    y = gcw_ref[GDN_CONV - 1:GDN_CONV, :] * x
    for i in range(GDN_CONV - 1):
        y = y + gcw_ref[i:i + 1, :] * gcs_ref[:, i * w:(i + 1) * w]
    act_ref[...] = jax.nn.silu(y)
    gcs_o[:, 0:(GDN_CONV - 2) * w] = gcs_ref[:, w:(GDN_CONV - 1) * w]
    gcs_o[:, (GDN_CONV - 2) * w:(GDN_CONV - 1) * w] = x

    w = MIX_W
    gate_b = zb_ref[:, ZB_C:ZB_C + w]
    ci = zb_ref[:, ZB_C + w:ZB_C + 2 * w] * zb_ref[:, ZB_C + 2 * w:ZB_C + 3 * w]
    u = scw_ref[SC_CONV - 1:SC_CONV, :] * ci
    for i in range(SC_CONV - 1):
        u = u + scw_ref[i:i + 1, :] * scs_ref[:, i * w:(i + 1) * w]
    oc_ref[...] = gate_b * u
    scs_o[:, 0:(SC_CONV - 2) * w] = scs_ref[:, w:(SC_CONV - 1) * w]
    scs_o[:, (SC_CONV - 2) * w:(SC_CONV - 1) * w] = ci

    xd = zb_ref[:, ZB_D:ZB_D + w]
    gate_d = zb_ref[:, ZB_D + w:ZB_D + 2 * w]
    xc = lcw_ref[LRU_CONV - 1:LRU_CONV, :] * xd
    for i in range(LRU_CONV - 1):
        xc = xc + lcw_ref[i:i + 1, :] * lcs_ref[:, i * w:(i + 1) * w]
    xc = xc + lcb_ref[...]
    a, bt = _lru_gates(xc, wa_ref, ba_ref, wx_ref, bx_ref, lam_ref)
    hnew = a * lh_ref[...] + bt
    lh_o[...] = hnew
    od_ref[...] = hnew * jax.nn.gelu(gate_d)
    lcs_o[:, 0:(LRU_CONV - 2) * w] = lcs_ref[:, w:(LRU_CONV - 1) * w]
    lcs_o[:, (LRU_CONV - 2) * w:(LRU_CONV - 1) * w] = xd


def _sample_pointwise(za, zb, gcs, scs, lcs, lh, gcw, scw, lcw, lcb, wa, ba, wx, bx, lam):
    bd = za.shape[0]
    shapes = [(bd, A_QKV_W), gcs.shape, (bd, MIX_W), scs.shape, (bd, MIX_W), lh.shape, lcs.shape]
    return pl.pallas_call(
        _spw_body,
        out_shape=[jax.ShapeDtypeStruct(s, F32) for s in shapes],
        compiler_params=pltpu.CompilerParams(vmem_limit_bytes=VMEM_LIMIT),
        name="sample_pointwise",
    )(za, zb, gcs, scs, lcs, lh, gcw, scw, lcw, lcb, wa, ba, wx, bx, lam)


def _col(row, diag):
    return jnp.sum(jnp.where(diag, row, 0.0), axis=1, keepdims=True)


def _sstep_body(act_ref, za_ref, zb_ref, ba_ref, s_ref, alog_ref, dtb_ref, nw_ref,
                k0_ref, v0_ref, k1_ref, v1_ref, k2_ref, v2_ref, bb_ref, bn_ref, sh0_ref, sh1_ref, sh2_ref,
                oa_ref, ob_ref, s_out_ref, w0_ref, w1_ref, w2_ref):
    del sh0_ref, sh1_ref, sh2_ref
    e = GDN_DK
    for g, w_ref in enumerate((w0_ref, w1_ref, w2_ref)):
        for h in range(DIL_GROUP_HEADS):
            c0 = g * MIX_W + h * e
            w_ref[0, 0, h:h + 1, :] = zb_ref[:, ZB_K + c0:ZB_K + c0 + e]
            w_ref[1, 0, h:h + 1, :] = zb_ref[:, ZB_V + c0:ZB_V + c0 + e]

    ri = lax.broadcasted_iota(jnp.int32, (e, e), 0)
    ci = lax.broadcasted_iota(jnp.int32, (e, e), 1)
    diag = ri == ci

    ba = ba_ref[...]
    beta_all = jax.nn.sigmoid(ba)
    g_all = -jnp.exp(alog_ref[...]) * _softplus(ba + dtb_ref[...])
    hd = range(GDN_HEADS)
    q = [act_ref[:, h * e:(h + 1) * e] for h in hd]
    k = [act_ref[:, GDN_HEADS * e + h * e:GDN_HEADS * e + (h + 1) * e] for h in hd]
    v = [act_ref[:, 2 * GDN_HEADS * e + h * GDN_DV:2 * GDN_HEADS * e + (h + 1) * GDN_DV] for h in hd]
    q = [q[h] * lax.rsqrt(jnp.sum(q[h] * q[h], axis=-1, keepdims=True) + NORM_EPS) * (e ** -0.5) for h in hd]
    k = [k[h] * lax.rsqrt(jnp.sum(k[h] * k[h], axis=-1, keepdims=True) + NORM_EPS) for h in hd]
    beta = [beta_all[:, h:h + 1] for h in hd]
    eg = [jnp.exp(g_all[:, GDN_HEADS + h:GDN_HEADS + h + 1]) for h in hd]
    s = [s_ref[h] for h in hd]
    kcol = [_col(k[h], diag) for h in hd]
    qcol = [_col(q[h], diag) for h in hd]
    ks = [jnp.sum(kcol[h] * s[h], axis=0, keepdims=True) for h in hd]
    qs = [jnp.sum(qcol[h] * s[h], axis=0, keepdims=True) for h in hd]
    v_new = [beta[h] * v[h] - (beta[h] * eg[h]) * ks[h] for h in hd]
    qk = [jnp.sum(q[h] * k[h], axis=-1, keepdims=True) for h in hd]
    for h in hd:
        o = eg[h] * qs[h] + qk[h] * v_new[h]
        s_out_ref[h] = s[h] * eg[h] + kcol[h] * v_new[h]
        hv = slice(h * GDN_DV, (h + 1) * GDN_DV)
        oa_ref[:, hv] = _rms(o, nw_ref[...]) * jax.nn.silu(za_ref[:, A_QKV_W + h * GDN_DV:A_QKV_W + (h + 1) * GDN_DV])

    kv = ((k0_ref, v0_ref), (k1_ref, v1_ref), (k2_ref, v2_ref))
    pairs = [(h, g) for h in range(DIL_GROUP_HEADS) for g in range(DIL_GROUPS)]
    c0 = {(h, g): g * MIX_W + h * e for h, g in pairs}
    q = {p: zb_ref[:, ZB_Q + c0[p]:ZB_Q + c0[p] + e] for p in pairs}
    kn = {p: zb_ref[:, ZB_K + c0[p]:ZB_K + c0[p] + e] for p in pairs}
    vn = {p: zb_ref[:, ZB_V + c0[p]:ZB_V + c0[p] + e] for p in pairs}
    sb = {(h, g): jnp.sum(kv[g][0][:, h, :] * q[(h, g)], axis=-1, keepdims=True) for h, g in pairs}
    sn = {p: jnp.sum(kn[p] * q[p], axis=-1, keepdims=True) for p in pairs}
    for h, g in pairs:
        col = g * DIL_GROUP_HEADS + h
        sb[(h, g)] = sb[(h, g)] * (e ** -0.5) + bb_ref[:, col:col + 1]
        sn[(h, g)] = sn[(h, g)] * (e ** -0.5) + bn_ref[:, col:col + 1]
    m = {p: jnp.maximum(jnp.max(sb[p], axis=0, keepdims=True), sn[p]) for p in pairs}
    pb = {p: jnp.exp(sb[p] - m[p]) for p in pairs}
    pn = {p: jnp.exp(sn[p] - m[p]) for p in pairs}
    l = {p: jnp.sum(pb[p], axis=0, keepdims=True) + pn[p] for p in pairs}
    out = {(h, g): (jnp.sum(pb[(h, g)] * kv[g][1][:, h, :], axis=0, keepdims=True) + pn[(h, g)] * vn[(h, g)]) / l[(h, g)]
           for h, g in pairs}
    lse = {p: m[p] + jnp.log(l[p]) for p in pairs}
    for h in range(DIL_GROUP_HEADS):
        lses = [lse[(h, g)] for g in range(DIL_GROUPS)]
        mm = functools.reduce(jnp.maximum, lses)
        es = [jnp.exp(x - mm) for x in lses]
        den = functools.reduce(lambda p, q_: p + q_, es)
        ob_ref[:, h * e:(h + 1) * e] = functools.reduce(
            lambda p, q_: p + q_, [(es[g] / den) * out[(h, g)] for g in range(DIL_GROUPS)])


def _sample_step(act3, za3, zb3, ba3, s_in, alog_row, dtb_row, nw, caches, shifted, layer, bias_buf, bias_new):
    bd = act3.shape[0]
    n_fixed_in = 16
    win_specs = [pl.BlockSpec((None, None, 2, 1, DIL_GROUP_HEADS, DIL_HEAD_DIM),
                              lambda i, rows=buf.shape[3]: (layer, i, 0, rows - 1, 0, 0)) for buf in shifted]

    def vec(w):
        return pl.BlockSpec((None, 1, w), lambda i: (i, 0, 0))

    def full(shape):
        return pl.BlockSpec(shape, lambda i: (0,) * len(shape))

    cache_specs, cache_args = [], []
    for cch in caches:
        for kvi in range(2):
            cache_specs.append(pl.BlockSpec((None, None, None, DIL_BLOCK, None, DIL_GROUP_HEADS, DIL_HEAD_DIM),
                                            lambda i, kvi=kvi: (layer, i, kvi, 0, 0, 0, 0)))
            cache_args.append(cch)
    st = pl.BlockSpec((None, GDN_HEADS, GDN_DK, GDN_DV), lambda i: (i, 0, 0, 0))
    return pl.pallas_call(
        _sstep_body,
        grid=(bd,),
        in_specs=[vec(A_QKV_W), vec(ZA_W), vec(ZB_W), vec(BA_W), st,
                  full((1, BA_W)), full((1, BA_W)), full((1, GDN_DV))] + cache_specs +
                 [full((DIL_BLOCK, DIL_HEADS)), full((1, DIL_HEADS))] +
                 [pl.BlockSpec(memory_space=pl.ANY)] * DIL_GROUPS,
        out_specs=[vec(MIX_W), vec(MIX_W), st] + win_specs,
        out_shape=[jax.ShapeDtypeStruct((bd, 1, MIX_W), F32), jax.ShapeDtypeStruct((bd, 1, MIX_W), F32),
                   jax.ShapeDtypeStruct(s_in.shape, F32)] + [jax.ShapeDtypeStruct(buf.shape, F32) for buf in shifted],
        input_output_aliases={n_fixed_in + g: 3 + g for g in range(DIL_GROUPS)},
        compiler_params=_cparams(("parallel",)),
        name="sample_step",
    )(act3, za3, zb3, ba3, s_in, alog_row, dtb_row, nw, *cache_args, bias_buf, bias_new, *shifted)


def _shift_body(a_ref, b_ref, o_ref, *, rb):
    o_ref[:, pl.ds(0, rb - 1)] = a_ref[:, pl.ds(1, rb - 1)]
    o_ref[:, pl.ds(rb - 1, 1)] = b_ref[...]


def _shift_cache(cache):
    depth, bd, _, rows, nh, e = cache.shape
    rb = min(rows, 512)
    return pl.pallas_call(
        functools.partial(_shift_body, rb=rb),
        grid=(depth, bd, rows // rb),
        in_specs=[
            pl.BlockSpec((None, None, 2, rb, nh, e), lambda l, i, j: (l, i, 0, j, 0, 0)),
            pl.BlockSpec((None, None, 2, 1, nh, e), lambda l, i, j: (l, i, 0, jnp.minimum((j + 1) * rb, rows - 1), 0, 0)),
        ],
        out_specs=pl.BlockSpec((None, None, 2, rb, nh, e), lambda l, i, j: (l, i, 0, j, 0, 0)),
        out_shape=jax.ShapeDtypeStruct(cache.shape, F32),
        compiler_params=_cparams(("parallel", "parallel", "parallel")),
        name="shift_cache",
    )(cache, cache)


def _t5_bucket(dist):
    exact = REL_BUCKETS // 2
    d = np.maximum(dist, 1).astype(np.float32)
    large = exact + (np.log(d / exact) / math.log(REL_MAX_DIST / exact) * (REL_BUCKETS - exact)).astype(np.int32)
    return np.where(dist < exact, dist, np.minimum(large, REL_BUCKETS - 1)).astype(np.int32)


def _prompt_bias(rel_bias):
    blk = DIL_BLOCK
    period = 4 * blk
    out = []
    for gi, (window, dilation) in enumerate(DIL_PATTERNS):
        n_off = window // dilation
        tab = rel_bias[:, gi * DIL_GROUP_HEADS:(gi + 1) * DIL_GROUP_HEADS]
        vals = tab[_t5_bucket(np.arange(n_off + 1) * dilation)].T.astype(F32)
        w = jnp.full((DIL_GROUP_HEADS, period), -jnp.inf, F32)
        w = lax.dynamic_update_slice(w, vals, (0, blk - 1))
        r = jnp.tile(w, (1, blk + 1))[:, :blk * (period + 1)].reshape(DIL_GROUP_HEADS, blk, period + 1)
        out.append(r[:, :, 0:2 * blk][:, :, ::-1])
    return jnp.stack(out, axis=0)


def _step_bias(rel_bias):
    cols_buf, cols_new = [], []
    for gi, (window, dilation) in enumerate(DIL_PATTERNS):
        n_off = window // dilation
        tab = rel_bias[:, gi * DIL_GROUP_HEADS:(gi + 1) * DIL_GROUP_HEADS]
        j = n_off - np.arange(n_off)
        cols_buf.append(tab[_t5_bucket(j * dilation)])
        cols_new.append(tab[_t5_bucket(np.zeros((1,), np.int64))])
    return jnp.concatenate(cols_buf, axis=1).astype(F32), jnp.concatenate(cols_new, axis=1).astype(F32)


def _block_diag(w):
    n, c, _ = w.shape
    eye = jnp.eye(n, dtype=w.dtype)
    return (eye[:, None, :, None] * w[:, :, None, :]).reshape(n * c, n * c)


def _layer_weights(p, l, final_norm):
    w_in = p['w_in'][l]
    b0, c0 = A_W, A_W + B_W
    zg0 = A_QKV_W
    ba0 = A_QKV_W + GDN_HEADS * GDN_DV
    w_all = jnp.concatenate([w_in[:, 0:ba0], w_in[:, b0:], p['w_gate'][l]], axis=1).astype(BF16)
    b_all = p['b_gate'][l][None]
    w_ba = jnp.pad(w_in[:, ba0:b0], ((0, 0), (0, BA_W - 2 * GDN_HEADS))).astype(BF16)
    pad_row = lambda v: jnp.pad(v, (GDN_HEADS, BA_W - 2 * GDN_HEADS))[None]
    row = lambda v: v[None].astype(F32)
    del zg0, c0
    return dict(
        fnw=row(final_norm), last=(l == DEPTH - 1),
        n1=row(p['norm_ffn1'][l]), gu1=p['ffn1_w_gu'][l].astype(BF16), dn1=p['ffn1_w_down'][l].astype(BF16),
        nm=row(p['norm_mix'][l]), w_all=w_all, b_all=b_all, w_ba=w_ba,
        gcw=p['gdn_conv_w'][l], alog=pad_row(p['gdn_a_log'][l]), dtb=pad_row(p['gdn_dt_bias'][l]),
        gnw=row(p['gdn_norm_w'][l]), scw=p['sc_conv_w'][l], lcw=p['lru_conv_w'][l], lcb=row(p['lru_conv_b'][l]),
        wa=_block_diag(p['lru_wa'][l]).astype(BF16), ba=row(p['lru_ba'][l]),
        wx=_block_diag(p['lru_wx'][l]).astype(BF16), bx=row(p['lru_bx'][l]), lam=row(p['lru_lambda'][l]),
        wbr=p['w_branch'][l].astype(BF16), wo=p['w_o'][l].astype(BF16),
        n2=row(p['norm_ffn2'][l]), gu2=p['ffn2_w_gu'][l].astype(BF16), dn2=p['ffn2_w_down'][l].astype(BF16),
    )


def _prompt_layer(x, w, bias, b, t):
    m = b * t
    x = _ffn(x, w['n1'], w['gu1'], w['dn1'], w['fnw'])
    za, zb, gates, ba = _proj(x, w['nm'], w['w_all'], w['b_all'], w['w_ba'], BF16)
    za3 = za.reshape(b, t, ZA_W)
    zb3 = zb.reshape(b, t, ZB_W)
    o_a, s_gdn = _gdn_prompt(za3, ba.reshape(b, t, BA_W), w['gcw'], w['alog'], w['dtb'], w['gnw'])
    o_b = _dil_prompt(zb3, bias)
    o_c, o_d, sc_st, lru_h = _cd_prompt(zb3, w['scw'], w['lcw'], w['lcb'], w['wa'], w['ba'], w['wx'], w['bx'], w['lam'])
    x = _merge(x, o_a.reshape(m, MIX_W), o_b.reshape(m, MIX_W), o_c.reshape(m, MIX_W), o_d.reshape(m, MIX_W),
               gates, w['wbr'], w['wo'])
    x = _ffn(x, w['n2'], w['gu2'], w['dn2'], w['fnw'], final=w['last'])
    gdn_conv = za3[:, t - (GDN_CONV - 1):, 0:A_QKV_W].astype(F32)
    bufs = []
    for gi, (window, _) in enumerate(DIL_PATTERNS):
        rows = min(window, t)
        k = zb3[:, t - rows:, ZB_K + gi * MIX_W:ZB_K + (gi + 1) * MIX_W]
        v = zb3[:, t - rows:, ZB_V + gi * MIX_W:ZB_V + (gi + 1) * MIX_W]
        bufs.append(jnp.stack([k, v], axis=1).reshape(b, 2, rows, DIL_GROUP_HEADS, DIL_HEAD_DIM).astype(F32))
    sc_conv = sc_st[:, SUBLANES - (SC_CONV - 1):]
    lru_conv = zb3[:, t - (LRU_CONV - 1):, ZB_D:ZB_D + MIX_W].astype(F32)
    return x, (s_gdn, gdn_conv, bufs[0], bufs[1], bufs[2], sc_conv, lru_h[:, 0], lru_conv)


def _sample_layer(x, w, l, st, shifted, step_bias):
    bd = x.shape[0]
    s_gdn, gdn_conv, caches, sc_conv, lru_h, lru_conv = st
    x = _ffn(x, w['n1'], w['gu1'], w['dn1'], w['fnw'])
    za, zb, gates, ba = _proj(x, w['nm'], w['w_all'], w['b_all'], w['w_ba'], F32)
    act, gdn_conv_new, o_c, sc_new, o_d, lru_h_new, lru_conv_new = _sample_pointwise(
        za, zb, gdn_conv.reshape(bd, -1), sc_conv.reshape(bd, -1), lru_conv.reshape(bd, -1), lru_h,
        w['gcw'], w['scw'], w['lcw'], w['lcb'], w['wa'], w['ba'], w['wx'], w['bx'], w['lam'])
    zb3 = zb.reshape(bd, 1, ZB_W)
    o_a, o_b, s_new, *shifted = _sample_step(act.reshape(bd, 1, A_QKV_W), za.reshape(bd, 1, ZA_W), zb3,
                                             ba.reshape(bd, 1, BA_W), s_gdn, w['alog'], w['dtb'], w['gnw'], caches,
                                             shifted, l, *step_bias)
    x = _merge(x, o_a.reshape(bd, MIX_W), o_b.reshape(bd, MIX_W), o_c, o_d, gates, w['wbr'], w['wo'])
    x = _ffn(x, w['n2'], w['gu2'], w['dn2'], w['fnw'], final=w['last'])
    new = (s_new, gdn_conv_new.reshape(gdn_conv.shape), sc_new.reshape(sc_conv.shape), lru_h_new,
           lru_conv_new.reshape(lru_conv.shape))
    return x, new, shifted


def kernel(x_prompt, x_sample, state_gdn, state_gdn_conv, cache_dil_w128, cache_dil_w512, cache_dil_w2048,
           state_sc_conv, state_lru, state_lru_conv, norm_ffn1, ffn1_w_gu, ffn1_w_down, norm_mix, w_in,
           gdn_conv_w, gdn_a_log, gdn_dt_bias, gdn_norm_w, rel_bias, sc_conv_w, lru_conv_w, lru_conv_b,
           lru_wa, lru_ba, lru_wx, lru_bx, lru_lambda, w_gate, b_gate, w_branch, w_o, norm_ffn2, ffn2_w_gu,
           ffn2_w_down, final_norm):
    p = dict(norm_ffn1=norm_ffn1, ffn1_w_gu=ffn1_w_gu, ffn1_w_down=ffn1_w_down, norm_mix=norm_mix, w_in=w_in,
             gdn_conv_w=gdn_conv_w, gdn_a_log=gdn_a_log, gdn_dt_bias=gdn_dt_bias, gdn_norm_w=gdn_norm_w,
             sc_conv_w=sc_conv_w, lru_conv_w=lru_conv_w, lru_conv_b=lru_conv_b, lru_wa=lru_wa, lru_ba=lru_ba,
             lru_wx=lru_wx, lru_bx=lru_bx, lru_lambda=lru_lambda, w_gate=w_gate, b_gate=b_gate,
             w_branch=w_branch, w_o=w_o, norm_ffn2=norm_ffn2, ffn2_w_gu=ffn2_w_gu, ffn2_w_down=ffn2_w_down)
    weights = [_layer_weights(p, l, final_norm) for l in range(DEPTH)]
    b, t, _ = x_prompt.shape
    bd = x_sample.shape[0]
    bias = _prompt_bias(rel_bias)
    step_bias = _step_bias(rel_bias)

    x = x_prompt.reshape(b * t, D_MODEL)
    p_states = []
    for l in range(DEPTH):
        x, new = _prompt_layer(x, weights[l], bias, b, t)
        p_states.append(new)
    y_prompt = x.reshape(b, t, D_MODEL)
    p_out = tuple(jnp.stack(zs, axis=0) for zs in zip(*p_states))

    caches = (cache_dil_w128, cache_dil_w512, cache_dil_w2048)
    shifted = [_shift_cache(c) for c in caches]
    strided = tuple(c.reshape(DEPTH, bd, 2, c.shape[3] // d, d, DIL_GROUP_HEADS, DIL_HEAD_DIM)
                    for c, (_, d) in zip(caches, DIL_PATTERNS))
    x = x_sample.reshape(bd, D_MODEL)
    s_states = []
    for l in range(DEPTH):
        st = (state_gdn[l], state_gdn_conv[l], strided, state_sc_conv[l], state_lru[l], state_lru_conv[l])
        x, new, shifted = _sample_layer(x, weights[l], l, st, shifted, step_bias)
        s_states.append(new)
    y_sample = x.reshape(bd, 1, D_MODEL)
    s_gdn, s_gdn_conv, s_sc_conv, s_lru, s_lru_conv = (jnp.stack(zs, axis=0) for zs in zip(*s_states))

    p_gdn, p_gdn_conv, p_w128, p_w512, p_w2048, p_sc_conv, p_lru, p_lru_conv = p_out
    return (y_prompt, y_sample, p_gdn, s_gdn, p_gdn_conv, s_gdn_conv, p_w128, shifted[0], p_w512, shifted[1],
            p_w2048, shifted[2], p_sc_conv, s_sc_conv, p_lru, s_lru, p_lru_conv, s_lru_conv)
```
